```python
import jax, jax.numpy as jnp
from jax import lax
import numpy as np

D_MODEL = 1024
BATCH = 32
SEQ = 256
DEPTH = 2
DEC_BATCH = 8
DEC_SEQ = 2048
PAST_LEN = 512

F32 = jnp.float32
GRID_W = 64
EPS = 1e-6
D_FF = 4 * D_MODEL
MIX_WIDTH = D_MODEL
N_EVEN = (DEPTH + 1) // 2
N_ODD = DEPTH // 2
N_DIR = 2

ATT_HEADS = 8
ATT_KV_HEADS = 2
HEAD_DIM = 64
ATT_WIDTH = ATT_HEADS * HEAD_DIM
ATT_KV_WIDTH = ATT_KV_HEADS * HEAD_DIM
ATT_GROUP = ATT_HEADS // ATT_KV_HEADS
ROPE_THETA = 10000.0
ROPE_PAIRS_PER_AXIS = HEAD_DIM // 4
Q_BLOCK = 128

MLSTM_HEADS = 4
MLSTM_HD = 128
MLSTM_WIDTH = MLSTM_HEADS * MLSTM_HD
MLSTM_CHUNK = 128
M_INIT = -1e30

LRU_WIDTH = 512
LRU_BLOCKS = 8
LRU_BD = LRU_WIDTH // LRU_BLOCKS
LRU_C = 8.0
CONV_W = 4
CONV_LEFT = 2

GLA_HEADS = 4
GLA_DK = 64
GLA_DV = 128
GLA_KW = GLA_HEADS * GLA_DK
GLA_WIDTH = GLA_HEADS * GLA_DV
GLA_RANK = 16
GLA_TAU = 16.0
GLA_CHUNK = 64

EVEN_SIZES = (ATT_WIDTH, ATT_KV_WIDTH, ATT_KV_WIDTH, MLSTM_WIDTH, MLSTM_WIDTH, MLSTM_WIDTH, MLSTM_WIDTH, N_DIR * MLSTM_HEADS, N_DIR * MLSTM_HEADS)
IN_EVEN = sum(EVEN_SIZES)
ODD_SIZES = (LRU_WIDTH, LRU_WIDTH, GLA_KW, GLA_KW, GLA_WIDTH, GLA_WIDTH, N_DIR * GLA_RANK)
IN_ODD = sum(ODD_SIZES)

kernel_name = 'hybrid_flow_trunk_step'


def split_cols(a, sizes):
    out, start = [], 0
    for s in sizes:
        out.append(a[..., start:start + s])
        start += s
    return out


def rms_norm(x, g):
    xf = x.astype(F32)
    y = xf * lax.rsqrt(jnp.mean(xf * xf, axis=-1, keepdims=True) + EPS)
    return (y * g.astype(F32)).astype(x.dtype)


def split_mod(mod):
    return jnp.split(mod[:, None, :], 6, axis=-1)


def modulate(x, g, shift, scale):
    return rms_norm(x, g) * (1 + scale) + shift


def gated_residual(x, y, g, gate):
    return x + gate * rms_norm(y, g)


def sq_relu_ffn(h, w1, w2):
    return jnp.square(jax.nn.relu(h @ w1)) @ w2


def flip(a):
    return jnp.flip(a, axis=1)


def axial_rope(n_tokens):
    rows = n_tokens // GRID_W
    row = jnp.repeat(jnp.arange(rows), GRID_W).astype(F32)
    col = jnp.tile(jnp.arange(GRID_W), rows).astype(F32)
    inv = jnp.power(ROPE_THETA, -jnp.arange(ROPE_PAIRS_PER_AXIS, dtype=F32) / ROPE_PAIRS_PER_AXIS)
    ang = jnp.concatenate([row[:, None] * inv, col[:, None] * inv], axis=-1)
    return jnp.cos(ang), jnp.sin(ang)


def apply_rope(x, cos, sin):
    x1, x2 = jnp.split(x, 2, axis=-1)
    c = cos[None, :, None, :].astype(x.dtype)
    s = sin[None, :, None, :].astype(x.dtype)
    return jnp.concatenate([x1 * c - x2 * s, x1 * s + x2 * c], axis=-1)


def block_attention(q, k, v):
    B, T, KV, G, HD = q.shape
    nb = T // Q_BLOCK
    qb = jnp.moveaxis(q.reshape(B, nb, Q_BLOCK, KV, G, HD), 1, 0)

    def one_block(qblk):
        s = jnp.einsum('bqkgd,bskd->bkgqs', qblk, k).astype(F32)
        p = jax.nn.softmax(s, axis=-1).astype(v.dtype)
        return jnp.einsum('bkgqs,bskd->bqkgd', p, v)

    o = lax.map(one_block, qb)
    return jnp.moveaxis(o, 0, 1).reshape(B, T, KV * G * HD)


def mlstm_scan(q, k, v, log_i, log_f, C0, n0, m0):
    B, T, H, D = q.shape
    L = MLSTM_CHUNK
    nc = T // L

    def chunk(a):
        return jnp.moveaxis(a.astype(F32).reshape((B, nc, L) + a.shape[2:]), 1, 0)

    causal = jnp.tril(jnp.ones((L, L), dtype=bool))[None, :, :, None]

    def step(carry, inp):
        C, n, m = carry
        qx, kx, vx, ix, fx = inp
        b = jnp.cumsum(fx, axis=1)
        g = b + m[:, None, :]
        dlog = jnp.where(causal, b[:, :, None] - b[:, None, :] + ix[:, None, :], -jnp.inf)
        m_t = jnp.maximum(g, jnp.max(dlog, axis=2))
        w = jnp.exp(dlog - m_t[:, :, None])
        w_inter = jnp.exp(g - m_t)
        qk = jnp.einsum('bthd,bshd->btsh', qx, kx) * w
        num = jnp.einsum('btsh,bshe->bthe', qk, vx) + w_inter[..., None] * jnp.einsum('bthd,bhde->bthe', qx, C)
        den = jnp.sum(qk, axis=2) + w_inter * jnp.einsum('bthd,bhd->bth', qx, n)
        h = num / jnp.maximum(jnp.abs(den), jnp.exp(-m_t))[..., None]
        b_last = b[:, -1]
        wlog = b_last[:, None] - b + ix
        m_new = jnp.maximum(b_last + m, jnp.max(wlog, axis=1))
        ws = jnp.exp(wlog - m_new[:, None])
        decay = jnp.exp(b_last + m - m_new)
        C_new = decay[..., None, None] * C + jnp.einsum('bsh,bshd,bshe->bhde', ws, kx, vx)
        n_new = decay[..., None] * n + jnp.einsum('bsh,bshd->bhd', ws, kx)
        return (C_new, n_new, m_new), h

    init = (C0.astype(F32), n0.astype(F32), m0.astype(F32))
    (C, n, m), h = lax.scan(step, init, (chunk(q), chunk(k), chunk(v), chunk(log_i), chunk(log_f)))
    return jnp.moveaxis(h, 0, 1).reshape(B, T, H, v.shape[-1]), (C, n, m)


def gla_scan(q, k, v, log_a, S0):
    B, T, H, DK = q.shape
    L = GLA_CHUNK
    nc = T // L

    def chunk(a):
        return jnp.moveaxis(a.astype(F32).reshape((B, nc, L) + a.shape[2:]), 1, 0)

    causal = jnp.tril(jnp.ones((L, L), dtype=bool))[None, :, :, None, None]

    def step(S, inp):
        qx, kx, vx, ax = inp
        b = jnp.cumsum(ax, axis=1)
        diff = b[:, :, None] - b[:, None, :]
        decay = jnp.where(causal, jnp.exp(jnp.minimum(diff, 0.0)), 0.0)
        scores = jnp.einsum('bthc,bshc,btshc->btsh', qx, kx, decay)
        o = jnp.einsum('btsh,bshe->bthe', scores, vx) + jnp.einsum('bthc,bhce->bthe', qx * jnp.exp(b), S)
        b_last = b[:, -1]
        S_new = jnp.exp(b_last)[..., None] * S + jnp.einsum('bshc,bshe->bhce', kx * jnp.exp(b_last[:, None] - b), vx)
        return S_new, o

    S_fin, o = lax.scan(step, S0.astype(F32), (chunk(q), chunk(k), chunk(v), chunk(log_a)))
    return jnp.moveaxis(o, 0, 1).reshape(B, T, H, v.shape[-1]), S_fin


def centred_depthwise_conv(x, w, b):
    T = x.shape[1]
    xp = jnp.pad(x, ((0, 0), (CONV_LEFT, CONV_W - 1 - CONV_LEFT), (0, 0)))
    y = b
    for j in range(CONV_W):
        y = y + xp[:, j:j + T] * w[j]
    return y


def _lin_combine(e1, e2):
    a1, b1 = e1
    a2, b2 = e2
    return a1 * a2, a2 * b1 + b2


def rglru(xc, w_r, b_r, w_i, b_i, lam, h0):
    B, T, W = xc.shape
    xb = xc.reshape(B, T, LRU_BLOCKS, LRU_BD)
    r = jax.nn.sigmoid(jnp.einsum('btnd,nde->btne', xb, w_r).reshape(B, T, W) + b_r)
    i = jax.nn.sigmoid(jnp.einsum('btnd,nde->btne', xb, w_i).reshape(B, T, W) + b_i)
    log_a = (-LRU_C * r.astype(F32)) * jax.nn.softplus(-lam.astype(F32))
    a = jnp.exp(log_a)
    u = jnp.sqrt(-jnp.expm1(2.0 * log_a)) * (i * xc).astype(F32)
    a_cum, h = lax.associative_scan(_lin_combine, (a, u), axis=1)
    h = h + a_cum * h0.astype(F32)[:, None]
    return h, h[:, -1]


def even_mixer(h, w_in, q_gain, k_gain, i_bias, f_bias, m_gain, rope, ctx_k, ctx_v, C0, n0, m0):
    B, T, _ = h.shape
    q, k, v, mq, mk, mv, mo, mi, mfg = split_cols(h @ w_in, EVEN_SIZES)
    q = rms_norm(q.reshape(B, T, ATT_HEADS, HEAD_DIM), q_gain)
    k = rms_norm(k.reshape(B, T, ATT_KV_HEADS, HEAD_DIM), k_gain)
    v = v.reshape(B, T, ATT_KV_HEADS, HEAD_DIM)
    if rope is None:
        k_all, v_all = k, v
    else:
        q = apply_rope(q, rope[0], rope[1])
        k = apply_rope(k, rope[0], rope[1])
        k_all = jnp.concatenate([jnp.swapaxes(ctx_k, 1, 2).astype(k.dtype), k], axis=1)
        v_all = jnp.concatenate([jnp.swapaxes(ctx_v, 1, 2).astype(v.dtype), v], axis=1)
    q = q.reshape(B, T, ATT_KV_HEADS, ATT_GROUP, HEAD_DIM) * HEAD_DIM ** -0.5
    attn = block_attention(q, k_all, v_all)
    mq = mq.reshape(B, T, MLSTM_HEADS, MLSTM_HD) * MLSTM_HD ** -0.5
    mk = mk.reshape(B, T, MLSTM_HEADS, MLSTM_HD)
    mv = mv.reshape(B, T, MLSTM_HEADS, MLSTM_HD)
    log_i = (mi.reshape(B, T, N_DIR, MLSTM_HEADS) + i_bias).astype(F32)
    log_f = jax.nn.log_sigmoid((mfg.reshape(B, T, N_DIR, MLSTM_HEADS) + f_bias).astype(F32))
    hf, (Cf, nf, mf_last) = mlstm_scan(mq, mk, mv, log_i[:, :, 0], log_f[:, :, 0], C0[:, 0], n0[:, 0], m0[:, 0])
    hb, (Cb, nb, mb_last) = mlstm_scan(flip(mq), flip(mk), flip(mv), flip(log_i[:, :, 1]), flip(log_f[:, :, 1]), C0[:, 1], n0[:, 1], m0[:, 1])
    hm = (hf + flip(hb)).astype(h.dtype)
    hm = rms_norm(hm, m_gain.reshape(MLSTM_HEADS, MLSTM_HD)).reshape(B, T, MLSTM_WIDTH) * jax.nn.sigmoid(mo)
    out = jnp.concatenate([attn, hm], axis=-1)
    ctx_state = (jnp.swapaxes(k, 1, 2), jnp.swapaxes(v, 1, 2), jnp.stack([Cf, Cb], axis=1), jnp.stack([nf, nb], axis=1), jnp.stack([mf_last, mb_last], axis=1))
    return out, ctx_state


def odd_mixer(h, w_in, conv_w, conv_b, w_r, b_r, w_i, b_i, lam, w_lr, b_lr, o_gain, h0, S0):
    B, T, _ = h.shape
    rx, rg, gq, gk, gv, gg, glr = split_cols(h @ w_in, ODD_SIZES)
    xc = centred_depthwise_conv(rx, conv_w, conv_b)
    hf, hf_last = rglru(xc, w_r[0], b_r[0], w_i[0], b_i[0], lam[0], h0[:, 0])
    hb, hb_last = rglru(flip(xc), w_r[1], b_r[1], w_i[1], b_i[1], lam[1], h0[:, 1])
    lru = jax.nn.gelu(rg) * (hf + flip(hb)).astype(h.dtype)
    gq = gq.reshape(B, T, GLA_HEADS, GLA_DK) * GLA_DK ** -0.5
    gk = gk.reshape(B, T, GLA_HEADS, GLA_DK)
    gv = gv.reshape(B, T, GLA_HEADS, GLA_DV)
    z = jnp.einsum('btrl,rlc->btrc', glr.reshape(B, T, N_DIR, GLA_RANK), w_lr) + b_lr
    log_a = (jax.nn.log_sigmoid(z.astype(F32)) / GLA_TAU).reshape(B, T, N_DIR, GLA_HEADS, GLA_DK)
    of, Sf = gla_scan(gq, gk, gv, log_a[:, :, 0], S0[:, 0])
    ob, Sb = gla_scan(flip(gq), flip(gk), flip(gv), flip(log_a[:, :, 1]), S0[:, 1])
    og = rms_norm((of + flip(ob)).astype(h.dtype), o_gain.reshape(GLA_HEADS, GLA_DV)).reshape(B, T, GLA_WIDTH) * jax.nn.silu(gg)
    out = jnp.concatenate([lru, og], axis=-1)
    ctx_state = (jnp.stack([hf_last, hb_last], axis=1), jnp.stack([Sf, Sb], axis=1))
    return out, ctx_state


def setup_inputs(seed: int = 0) -> dict:
    key = jax.random.key(seed)
    keys = iter(jax.random.split(key, 48))

    def nrm(shape, scale=1.0):
        return scale * jax.random.normal(next(keys), shape, F32)

    def gain(shape):
        return 1.0 + 0.05 * jax.random.normal(next(keys), shape, F32)

    lam_a = jax.random.uniform(next(keys), (N_ODD, N_DIR, LRU_WIDTH), F32, 0.9, 0.999)
    lam_s = lam_a ** (1.0 / LRU_C)
    f_bias = 3.0 + 3.0 * jax.random.uniform(next(keys), (N_EVEN, N_DIR, MLSTM_HEADS), F32)
    return {
        'x_prompt': nrm((BATCH, SEQ, D_MODEL)),
        'x_sample': nrm((DEC_BATCH, DEC_SEQ, D_MODEL)),
        'cache_attn_k': nrm((DEC_BATCH, N_EVEN, ATT_KV_HEADS, PAST_LEN, HEAD_DIM)),
        'cache_attn_v': nrm((DEC_BATCH, N_EVEN, ATT_KV_HEADS, PAST_LEN, HEAD_DIM)),
        'state_mlstm_C': nrm((DEC_BATCH, N_EVEN, N_DIR, MLSTM_HEADS, MLSTM_HD, MLSTM_HD), 0.5),
        'state_mlstm_n': nrm((DEC_BATCH, N_EVEN, N_DIR, MLSTM_HEADS, MLSTM_HD), 0.5),
        'state_mlstm_m': nrm((DEC_BATCH, N_EVEN, N_DIR, MLSTM_HEADS)),
        'state_lru_h': nrm((DEC_BATCH, N_ODD, N_DIR, LRU_WIDTH), 0.5),
        'state_gla_S': nrm((DEC_BATCH, N_ODD, N_DIR, GLA_HEADS, GLA_DK, GLA_DV), 0.5),
        'c': nrm((DEC_BATCH, D_MODEL)),
        'c_ctx': nrm((D_MODEL,)),
        'ada_w': nrm((DEPTH, D_MODEL, 6 * D_MODEL), D_MODEL ** -0.5),
        'ada_b': nrm((DEPTH, 6 * D_MODEL), 0.02),
        'norm_mix_pre': gain((DEPTH, D_MODEL)),
        'norm_mix_post': gain((DEPTH, D_MODEL)),
        'norm_ffn_pre': gain((DEPTH, D_MODEL)),
        'norm_ffn_post': gain((DEPTH, D_MODEL)),
        'w_out': nrm((DEPTH, MIX_WIDTH, D_MODEL), MIX_WIDTH ** -0.5),
        'ffn_w1': nrm((DEPTH, D_MODEL, D_FF), D_MODEL ** -0.5),
        'ffn_w2': nrm((DEPTH, D_FF, D_MODEL), D_FF ** -0.5),
        'w_in_even': nrm((N_EVEN, D_MODEL, IN_EVEN), D_MODEL ** -0.5),
        'attn_q_norm': gain((N_EVEN, HEAD_DIM)),
        'attn_k_norm': gain((N_EVEN, HEAD_DIM)),
        'mlstm_i_bias': nrm((N_EVEN, N_DIR, MLSTM_HEADS), 0.1),
        'mlstm_f_bias': f_bias,
        'mlstm_norm': gain((N_EVEN, MLSTM_WIDTH)),
        'w_in_odd': nrm((N_ODD, D_MODEL, IN_ODD), D_MODEL ** -0.5),
        'lru_conv_w': nrm((N_ODD, CONV_W, LRU_WIDTH), CONV_W ** -0.5),
        'lru_conv_b': nrm((N_ODD, LRU_WIDTH), 0.02),
        'lru_w_r': nrm((N_ODD, N_DIR, LRU_BLOCKS, LRU_BD, LRU_BD), LRU_BD ** -0.5),
        'lru_b_r': nrm((N_ODD, N_DIR, LRU_WIDTH), 0.1),
        'lru_w_i': nrm((N_ODD, N_DIR, LRU_BLOCKS, LRU_BD, LRU_BD), LRU_BD ** -0.5),
        'lru_b_i': nrm((N_ODD, N_DIR, LRU_WIDTH), 0.1),
        'lru_lambda': jnp.log(lam_s) - jnp.log1p(-lam_s),
        'gla_w_lr': nrm((N_ODD, N_DIR, GLA_RANK, GLA_KW), GLA_RANK ** -0.5),
        'gla_b_lr': nrm((N_ODD, N_DIR, GLA_KW), 0.1),
        'gla_norm': gain((N_ODD, GLA_WIDTH)),
    }


def reference(x_prompt, x_sample, cache_attn_k, cache_attn_v, state_mlstm_C, state_mlstm_n, state_mlstm_m, state_lru_h, state_gla_S, c, c_ctx, ada_w, ada_b, norm_mix_pre, norm_mix_post, norm_ffn_pre, norm_ffn_post, w_out, ffn_w1, ffn_w2, w_in_even, attn_q_norm, attn_k_norm, mlstm_i_bias, mlstm_f_bias, mlstm_norm, w_in_odd, lru_conv_w, lru_conv_b, lru_w_r, lru_b_r, lru_w_i, lru_b_i, lru_lambda, gla_w_lr, gla_b_lr, gla_norm):
    rope = axial_rope(x_sample.shape[1])
    silu_ctx = jax.nn.silu(c_ctx)[None]
    silu_c = jax.nn.silu(c)
    Bp = x_prompt.shape[0]
    zC = jnp.zeros((Bp, N_DIR, MLSTM_HEADS, MLSTM_HD, MLSTM_HD), F32)
    zn = jnp.zeros((Bp, N_DIR, MLSTM_HEADS, MLSTM_HD), F32)
    zm = jnp.full((Bp, N_DIR, MLSTM_HEADS), M_INIT, F32)
    zh = jnp.zeros((Bp, N_DIR, LRU_WIDTH), F32)
    zS = jnp.zeros((Bp, N_DIR, GLA_HEADS, GLA_DK, GLA_DV), F32)
    xp, xs = x_prompt, x_sample
    ks, vs, Cs, ns, ms, hs_, Ss = [], [], [], [], [], [], []
    for l in range(DEPTH):
        sh_ap, sc_ap, gt_ap, sh_mp, sc_mp, gt_mp = split_mod(silu_ctx @ ada_w[l] + ada_b[l])
        sh_as, sc_as, gt_as, sh_ms, sc_ms, gt_ms = split_mod(silu_c @ ada_w[l] + ada_b[l])
        hp = modulate(xp, norm_mix_pre[l], sh_ap, sc_ap)
        hs = modulate(xs, norm_mix_pre[l], sh_as, sc_as)
        if l % 2 == 0:
            e = l // 2
            yp, (k_c, v_c, C_c, n_c, m_c) = even_mixer(hp, w_in_even[e], attn_q_norm[e], attn_k_norm[e], mlstm_i_bias[e], mlstm_f_bias[e], mlstm_norm[e], None, None, None, zC, zn, zm)
            ys, _ = even_mixer(hs, w_in_even[e], attn_q_norm[e], attn_k_norm[e], mlstm_i_bias[e], mlstm_f_bias[e], mlstm_norm[e], rope, cache_attn_k[:, e], cache_attn_v[:, e], state_mlstm_C[:, e], state_mlstm_n[:, e], state_mlstm_m[:, e])
            ks.append(k_c)
            vs.append(v_c)
            Cs.append(C_c)
            ns.append(n_c)
            ms.append(m_c)
        else:
            o = l // 2
            yp, (h_c, S_c) = odd_mixer(hp, w_in_odd[o], lru_conv_w[o], lru_conv_b[o], lru_w_r[o], lru_b_r[o], lru_w_i[o], lru_b_i[o], lru_lambda[o], gla_w_lr[o], gla_b_lr[o], gla_norm[o], zh, zS)
            ys, _ = odd_mixer(hs, w_in_odd[o], lru_conv_w[o], lru_conv_b[o], lru_w_r[o], lru_b_r[o], lru_w_i[o], lru_b_i[o], lru_lambda[o], gla_w_lr[o], gla_b_lr[o], gla_norm[o], state_lru_h[:, o], state_gla_S[:, o])
            hs_.append(h_c)
            Ss.append(S_c)
        xp = gated_residual(xp, yp @ w_out[l], norm_mix_post[l], gt_ap)
        xs = gated_residual(xs, ys @ w_out[l], norm_mix_post[l], gt_as)
        xp = gated_residual(xp, sq_relu_ffn(modulate(xp, norm_ffn_pre[l], sh_mp, sc_mp), ffn_w1[l], ffn_w2[l]), norm_ffn_post[l], gt_mp)
        xs = gated_residual(xs, sq_relu_ffn(modulate(xs, norm_ffn_pre[l], sh_ms, sc_ms), ffn_w1[l], ffn_w2[l]), norm_ffn_post[l], gt_ms)
    dt = x_prompt.dtype
    y_prompt = xp
    y_sample = xs
    new_attn_k = jnp.stack(ks, axis=1).astype(dt)
    new_attn_v = jnp.stack(vs, axis=1).astype(dt)
    new_mlstm_C = jnp.stack(Cs, axis=1).astype(dt)
    new_mlstm_n = jnp.stack(ns, axis=1).astype(dt)
    new_mlstm_m = jnp.stack(ms, axis=1).astype(dt)
    new_lru_h = jnp.stack(hs_, axis=1).astype(dt)
    new_gla_S = jnp.stack(Ss, axis=1).astype(dt)
    return (y_prompt, y_sample, new_attn_k, new_attn_v, new_mlstm_C, new_mlstm_n, new_mlstm_m, new_lru_h, new_gla_S)
```

```python
import functools

import numpy as np
import jax
import jax.numpy as jnp
from jax import lax
from jax.experimental import pallas as pl
from jax.experimental.pallas import tpu as pltpu

F32 = jnp.float32
BF16 = jnp.bfloat16

D_MODEL = 1024
D_FF = 4 * D_MODEL
GRID_W = 64
EPS = 1e-6
N_DIR = 2

ATT_HEADS = 8
ATT_KV_HEADS = 2
HEAD_DIM = 64
ATT_WIDTH = ATT_HEADS * HEAD_DIM
ATT_KV_WIDTH = ATT_KV_HEADS * HEAD_DIM
ROPE_THETA = 10000.0
ROPE_PAIRS_PER_AXIS = HEAD_DIM // 4

MLSTM_HEADS = 4
MLSTM_HD = 128
MLSTM_WIDTH = MLSTM_HEADS * MLSTM_HD
MLSTM_CHUNK = 128
M_INIT = -1e30

LRU_WIDTH = 512
LRU_BLOCKS = 8
LRU_BD = LRU_WIDTH // LRU_BLOCKS
LRU_C = 8.0

GLA_HEADS = 4
GLA_DK = 64
GLA_DV = 128
GLA_KW = GLA_HEADS * GLA_DK
GLA_WIDTH = GLA_HEADS * GLA_DV
GLA_RANK = 16
GLA_TAU = 16.0
GLA_CHUNK = 64

LANES = 128
SUBLANES = 8
VMEM_LIMIT = 56 * 1024 * 1024

TOKEN_TILE = 512
ATTN_Q_TILE = 256
FF_CHUNK = 1024
LRU_GATE_TILE = 512


def _cparams(*sem):
    return pltpu.CompilerParams(dimension_semantics=sem, vmem_limit_bytes=VMEM_LIMIT)


def _const_spec(shape):
    n = len(shape)
    return pl.BlockSpec(shape, lambda *_: (0,) * n, pipeline_mode=pl.Buffered(1))


def _dot(a, b):
    return jnp.dot(a, b, preferred_element_type=F32)


def _dot_nt(a, b):
    return lax.dot_general(a, b, (((1,), (1,)), ((), ())), preferred_element_type=F32)


def _dot_tn(a, b):
    return lax.dot_general(a, b, (((0,), (0,)), ((), ())), preferred_element_type=F32)


def _split3(x):
    hi = x.astype(BF16)
    r = x - hi.astype(F32)
    mid = r.astype(BF16)
    lo = (r - mid.astype(F32)).astype(BF16)
    return hi, mid, lo


def _split2(x):
    hi = x.astype(BF16)
    lo = (x - hi.astype(F32)).astype(BF16)
    return hi, lo


def _rms(x, g):
    return x * lax.rsqrt(jnp.mean(x * x, axis=-1, keepdims=True) + EPS) * g


def _log_sigmoid(x):
    return jnp.minimum(x, 0.0) - jnp.log1p(jnp.exp(-jnp.abs(x)))


def _softplus(x):
    return jnp.maximum(x, 0.0) + jnp.log1p(jnp.exp(-jnp.abs(x)))


def _group_rms(y, gain, ones_bd, width):
    hi, lo = _split2(y * y)
    ss = _dot(hi, ones_bd) + _dot(lo, ones_bd)
    return y * lax.rsqrt(ss * (1.0 / width) + EPS) * gain


def _ada_kernel(c_ref, w_ref, b_ref, o_ref):
    s = jax.nn.silu(c_ref[...])
    o_ref[...] = _dot(s.astype(BF16), w_ref[...].astype(BF16)) + b_ref[...]


def _ada_call(cvec, ada_w, ada_b):
    depth = ada_w.shape[0]
    rows = cvec.shape[0]
    n_col = ada_w.shape[2] // D_MODEL
    return pl.pallas_call(
        _ada_kernel,
        grid=(depth, n_col),
        in_specs=[
            pl.BlockSpec((rows, D_MODEL), lambda l, j: (0, 0)),
            pl.BlockSpec((None, D_MODEL, D_MODEL), lambda l, j: (l, 0, j)),
            pl.BlockSpec((None, 1, D_MODEL), lambda l, j: (l, 0, j)),
        ],
        out_specs=pl.BlockSpec((None, rows, D_MODEL), lambda l, j: (l, 0, j)),
        out_shape=jax.ShapeDtypeStruct((depth, rows, ada_w.shape[2]), F32),
        compiler_params=_cparams("arbitrary", "arbitrary"),
        name="ada_mod",
    )(cvec, ada_w, ada_b.reshape(depth, 1, -1))


def _mod_spec(row0, row_stride):
    return pl.BlockSpec((None, 6, 1, D_MODEL), lambda b, i: (row0 + row_stride * b, 0, 0, 0))


def _tok_spec(tm, width):
    return pl.BlockSpec((None, tm, width), lambda b, i: (b, i, 0))


def _rope(y, cos, sin_signed):
    width = y.shape[1]
    reps = width // LANES
    cosw = jnp.concatenate([cos] * reps, axis=1) if reps > 1 else cos
    sinw = jnp.concatenate([sin_signed] * reps, axis=1) if reps > 1 else sin_signed
    lane = lax.broadcasted_iota(jnp.int32, y.shape, 1)
    first_half = (lane % HEAD_DIM) < (HEAD_DIM // 2)
    partner = jnp.where(first_half, pltpu.roll(y, width - HEAD_DIM // 2, axis=1), pltpu.roll(y, HEAD_DIM // 2, axis=1))
    return y * cosw + partner * sinw


def _kv_variants(a):
    lane = lax.broadcasted_iota(jnp.int32, a.shape, 1)
    low = lane < HEAD_DIM
    swapped = pltpu.roll(a, HEAD_DIM, axis=1)
    zero = jnp.zeros_like(a)
    return jnp.concatenate([
        jnp.where(low, a, zero),
        jnp.where(low, zero, swapped),
        jnp.where(low, swapped, zero),
        jnp.where(low, zero, a),
    ], axis=1)


def _inproj_even_kernel(*refs, rope):
    if rope:
        (x_ref, mod_ref, g_ref, wqkv_ref, wm_ref, wg_ref, qg_ref, kg_ref, ib_ref, fb_ref, ones_ref, cos_ref, sin_ref,
         q_out, kk_out, vv_out, k_out, v_out, mq_out, mk_out, mv_out, mo_out, gate_out) = refs
    else:
        (x_ref, mod_ref, g_ref, wqkv_ref, wm_ref, wg_ref, qg_ref, kg_ref, ib_ref, fb_ref, ones_ref,
         q_out, kk_out, vv_out, k_out, v_out, mq_out, mk_out, mv_out, mo_out, gate_out) = refs
    h = _rms(x_ref[...], g_ref[...]) * (1.0 + mod_ref[1]) + mod_ref[0]
    hb = h.astype(BF16)

    qkv = _dot(hb, wqkv_ref[...])
    q = _group_rms(qkv[:, :ATT_WIDTH], qg_ref[...], ones_ref[...], HEAD_DIM)
    k = _group_rms(qkv[:, ATT_WIDTH:ATT_WIDTH + ATT_KV_WIDTH], kg_ref[...], ones_ref[:ATT_KV_WIDTH, :ATT_KV_WIDTH], HEAD_DIM)
    v = qkv[:, ATT_WIDTH + ATT_KV_WIDTH:]
    if rope:
        q = _rope(q, cos_ref[...], sin_ref[...])
        k = _rope(k, cos_ref[...], sin_ref[...])
    q_out[...] = (q * HEAD_DIM ** -0.5).astype(BF16)
    k_out[...] = k
    v_out[...] = v
    kk_out[...] = _kv_variants(k).astype(BF16)
    vv_out[...] = _kv_variants(v).astype(BF16)

    mm = _dot(hb, wm_ref[...])
    w = MLSTM_WIDTH
    mq_out[...] = (mm[:, :w] * MLSTM_HD ** -0.5).astype(BF16)
    mk_out[...] = mm[:, w:2 * w].astype(BF16)
    mv_out[...] = mm[:, 2 * w:3 * w].astype(BF16)
    mo_out[...] = mm[:, 3 * w:]

    gates = _dot(hb, wg_ref[...])
    lane = lax.broadcasted_iota(jnp.int32, gates.shape, 1)
    gate_out[...] = jnp.where(lane < N_DIR * MLSTM_HEADS, gates + ib_ref[...], _log_sigmoid(gates + fb_ref[...]))


def _inproj_even_call(x, mod, mod_row0, mod_stride, prm, rope_tabs):
    B, T, _ = x.shape
    tm = min(TOKEN_TILE, T)
    rope = rope_tabs is not None
    ins = [x, mod, prm["g_mix_pre"], prm["w_qkv"], prm["w_m"], prm["w_g"], prm["q_gain"], prm["k_gain"],
           prm["i_bias"], prm["f_bias"], prm["ones64"]]
    in_specs = [_tok_spec(tm, D_MODEL), _mod_spec(mod_row0, mod_stride)] + [_const_spec(a.shape) for a in ins[2:]]
    if rope:
        ins += list(rope_tabs)
        in_specs += [pl.BlockSpec((tm, LANES), lambda b, i: (i, 0))] * 2
    widths = [(ATT_WIDTH, BF16), (4 * LANES, BF16), (4 * LANES, BF16), (ATT_KV_WIDTH, F32), (ATT_KV_WIDTH, F32),
              (MLSTM_WIDTH, BF16), (MLSTM_WIDTH, BF16), (MLSTM_WIDTH, BF16), (MLSTM_WIDTH, F32), (LANES, F32)]
    return pl.pallas_call(
        functools.partial(_inproj_even_kernel, rope=rope),
        grid=(B, T // tm),
        in_specs=in_specs,
        out_specs=[_tok_spec(tm, w) for w, _ in widths],
        out_shape=[jax.ShapeDtypeStruct((B, T, w), dt) for w, dt in widths],
        compiler_params=_cparams("parallel", "parallel"),
        name="inproj_even",
    )(*ins)


def _attn_kernel(*refs, cached):
    if cached:
        q_ref, kk_ref, vv_ref, kc_ref, vc_ref, o_ref = refs
    else:
        q_ref, kk_ref, vv_ref, o_ref = refs
    for pair in range(ATT_HEADS // 2):
        qp = q_ref[:, pair * LANES:(pair + 1) * LANES]
        acc = None
        for half in range(2):
            head = 2 * pair + half
            kv = head // (ATT_HEADS // ATT_KV_HEADS)
            col = (2 * kv + half) * LANES
            s = _dot_nt(qp, kk_ref[:, col:col + LANES])
            m = jnp.max(s, axis=-1, keepdims=True)
            if cached:
                sc = _dot_nt(qp, kc_ref[:, col:col + LANES])
                m = jnp.maximum(m, jnp.max(sc, axis=-1, keepdims=True))
            p = jnp.exp(s - m)
            den = jnp.sum(p, axis=-1, keepdims=True)
            o = _dot(p.astype(BF16), vv_ref[:, col:col + LANES])
            if cached:
                pc = jnp.exp(sc - m)
                den = den + jnp.sum(pc, axis=-1, keepdims=True)
                o = o + _dot(pc.astype(BF16), vc_ref[:, col:col + LANES])
            o = o / den
            acc = o if acc is None else acc + o
        o_ref[:, pair * LANES:(pair + 1) * LANES] = acc.astype(o_ref.dtype)


def _attn_call(q, kk, vv, cache=None):
    B, T, _ = q.shape
    tq = min(ATTN_Q_TILE, T)
    cached = cache is not None
    ins = [q, kk, vv]
    full = lambda n: pl.BlockSpec((None, n, 4 * LANES), lambda b, i: (b, 0, 0))
    in_specs = [_tok_spec(tq, ATT_WIDTH), full(T), full(T)]
    if cached:
        ins += list(cache)
        in_specs += [full(cache[0].shape[1])] * 2
    return pl.pallas_call(
        functools.partial(_attn_kernel, cached=cached),
        grid=(B, T // tq),
        in_specs=in_specs,
        out_specs=_tok_spec(tq, ATT_WIDTH),
        out_shape=jax.ShapeDtypeStruct((B, T, ATT_WIDTH), BF16),
        compiler_params=_cparams("parallel", "parallel"),
        name="attention",
    )(*ins)


def _mlstm_kernel(*refs, zero_init):
    if zero_init:
        (qf_ref, kf_ref, vf_ref, gf_ref, qb_ref, kb_ref, vb_ref, gb_ref, tri_ref,
         hf_ref, hb_ref, c_ref, n_ref, m_ref) = refs
    else:
        (qf_ref, kf_ref, vf_ref, gf_ref, qb_ref, kb_ref, vb_ref, gb_ref, tri_ref, c0_ref, n0_ref, m0_ref,
         hf_ref, hb_ref, c_ref, n_ref, m_ref) = refs
    L = MLSTM_CHUNK
    H = MLSTM_HEADS

    @pl.when(pl.program_id(1) == 0)
    def _init():
        if zero_init:
            c_ref[...] = jnp.zeros(c_ref.shape, F32)
            n_ref[...] = jnp.zeros(n_ref.shape, F32)
            m_ref[...] = jnp.full(m_ref.shape, M_INIT, F32)
        else:
            c_ref[...] = c0_ref[...]
            n_ref[...] = n0_ref[...]
            m_ref[...] = m0_ref[...]

    row = lax.broadcasted_iota(jnp.int32, (L, L), 0)
    colm = lax.broadcasted_iota(jnp.int32, (L, L), 1)
    dirs = ((qf_ref, kf_ref, vf_ref, gf_ref, hf_ref, colm <= row, L - 1),
            (qb_ref, kb_ref, vb_ref, gb_ref, hb_ref, colm >= row, 0))
    for d, (q_ref, k_ref, v_ref, g_ref, h_ref, mask, last) in enumerate(dirs):
        gates = g_ref[...]
        hi, mid, lo = _split3(gates)
        tri = tri_ref[d]
        bcum = _dot(tri, hi) + _dot(tri, mid) + _dot(tri, lo)
        gates_t = gates.T
        bcum_t = bcum.T
        for hd in range(H):
            j = d * H + hd
            jf = N_DIR * H + j
            sl = slice(hd * MLSTM_HD, (hd + 1) * MLSTM_HD)
            q = q_ref[:, sl]
            k = k_ref[:, sl]
            v = v_ref[:, sl]
            b_col = bcum[:, jf:jf + 1]
            b_row = bcum_t[jf:jf + 1, :]
            i_col = gates[:, j:j + 1]
            i_row = gates_t[j:j + 1, :]
            m_prev = m_ref[0:1, j:j + 1]
            c_prev = c_ref[d, hd]
            n_prev = n_ref[j:j + 1, :]

            g = b_col + m_prev
            dlog = jnp.where(mask, b_col - b_row + i_row, -jnp.inf)
            m_t = jnp.maximum(g, jnp.max(dlog, axis=1, keepdims=True))
            w = jnp.exp(dlog - m_t)
            w_inter = jnp.exp(g - m_t)
            qk = _dot_nt(q, k) * w
            num = _dot(qk.astype(BF16), v) + w_inter * _dot(q, c_prev.astype(BF16))
            den = jnp.sum(qk, axis=1, keepdims=True) + w_inter * jnp.sum(q.astype(F32) * n_prev, axis=1, keepdims=True)
            h_ref[:, sl] = num / jnp.maximum(jnp.abs(den), jnp.exp(-m_t))

            b_last = bcum[last:last + 1, jf:jf + 1]
            wlog = b_last - b_col + i_col
            m_new = jnp.maximum(b_last + m_prev, jnp.max(wlog, axis=0, keepdims=True))
            ws = jnp.exp(wlog - m_new)
            decay = jnp.exp(b_last + m_prev - m_new)
            kw = k.astype(F32) * ws
            c_ref[d, hd] = decay * c_prev + _dot_tn(kw.astype(BF16), v)
            n_ref[j:j + 1, :] = decay * n_prev + jnp.sum(kw, axis=0, keepdims=True)
            m_ref[0:1, j:j + 1] = m_new


def _mlstm_call(mq, mk, mv, gates, tri, state0):
    B, T, _ = mq.shape
    L = MLSTM_CHUNK
    nc = T // L
    zero_init = state0 is None
    fwd = lambda w: pl.BlockSpec((None, L, w), lambda b, c: (b, c, 0))
    bwd = lambda w: pl.BlockSpec((None, L, w), lambda b, c: (b, nc - 1 - c, 0))
    W = MLSTM_WIDTH
    c_spec = pl.BlockSpec((None, N_DIR, MLSTM_HEADS, MLSTM_HD, MLSTM_HD), lambda b, c: (b, 0, 0, 0, 0))
    n_spec = pl.BlockSpec((None, N_DIR * MLSTM_HEADS, MLSTM_HD), lambda b, c: (b, 0, 0))
    m_spec = pl.BlockSpec((None, 1, LANES), lambda b, c: (b, 0, 0))
    ins = [mq, mk, mv, gates, mq, mk, mv, gates, tri]
    in_specs = [fwd(W), fwd(W), fwd(W), fwd(LANES), bwd(W), bwd(W), bwd(W), bwd(LANES), _const_spec(tri.shape)]
    if not zero_init:
        ins += list(state0)
        in_specs += [c_spec, n_spec, m_spec]
    return pl.pallas_call(
        functools.partial(_mlstm_kernel, zero_init=zero_init),
        grid=(B, nc),
        in_specs=in_specs,
        out_specs=[fwd(W), bwd(W), c_spec, n_spec, m_spec],
        out_shape=[jax.ShapeDtypeStruct((B, T, W), F32), jax.ShapeDtypeStruct((B, T, W), F32),
                   jax.ShapeDtypeStruct((B, N_DIR, MLSTM_HEADS, MLSTM_HD, MLSTM_HD), F32),
                   jax.ShapeDtypeStruct((B, N_DIR * MLSTM_HEADS, MLSTM_HD), F32),
                   jax.ShapeDtypeStruct((B, 1, LANES), F32)],
        compiler_params=_cparams("parallel", "arbitrary"),
        name="mlstm",
    )(*ins)


def _residual_ffn(x, mix_a, mix_b, mod_ref, gpost_ref, gfpre_ref, gfpost_ref, wout_ref, w1_ref, w2_ref, o_ref):
    half = D_MODEL // 2
    y = _dot(mix_a.astype(BF16), wout_ref[:half, :]) + _dot(mix_b.astype(BF16), wout_ref[half:, :])
    x1 = x + mod_ref[2] * _rms(y, gpost_ref[...])
    h2 = (_rms(x1, gfpre_ref[...]) * (1.0 + mod_ref[4]) + mod_ref[3]).astype(BF16)
    f = None
    for c0 in range(0, D_FF, FF_CHUNK):
        a = jnp.square(jnp.maximum(_dot(h2, w1_ref[:, c0:c0 + FF_CHUNK]), 0.0)).astype(BF16)
        part = _dot(a, w2_ref[c0:c0 + FF_CHUNK, :])
        f = part if f is None else f + part
    o_ref[...] = x1 + mod_ref[5] * _rms(f, gfpost_ref[...])


def _head_rms(y, gain_ref, width):
    outs = []
    for h0 in range(0, y.shape[1], width):
        seg = y[:, h0:h0 + width]
        outs.append(seg * lax.rsqrt(jnp.mean(seg * seg, axis=-1, keepdims=True) + EPS) * gain_ref[:, h0:h0 + width])
    return jnp.concatenate(outs, axis=1)


def _post_even_kernel(x_ref, attn_ref, hf_ref, hb_ref, mo_ref, mod_ref, gm_ref, gpost_ref, gfpre_ref, gfpost_ref,
                      wout_ref, w1_ref, w2_ref, o_ref):
    hm = _head_rms(hf_ref[...] + hb_ref[...], gm_ref, MLSTM_HD) * jax.nn.sigmoid(mo_ref[...])
    _residual_ffn(x_ref[...], attn_ref[...], hm, mod_ref, gpost_ref, gfpre_ref, gfpost_ref, wout_ref, w1_ref, w2_ref, o_ref)


def _post_odd_kernel(x_ref, lru_ref, rg_ref, of_ref, ob_ref, gg_ref, mod_ref, gm_ref, gpost_ref, gfpre_ref, gfpost_ref,
                     wout_ref, w1_ref, w2_ref, o_ref):
    lru = jax.nn.gelu(rg_ref[...]) * lru_ref[...]
    og = _head_rms(of_ref[...] + ob_ref[...], gm_ref, GLA_DV) * jax.nn.silu(gg_ref[...])
    _residual_ffn(x_ref[...], lru, og, mod_ref, gpost_ref, gfpre_ref, gfpost_ref, wout_ref, w1_ref, w2_ref, o_ref)


def _post_call(kernel_fn, name, x, mix_ins, mod, mod_row0, mod_stride, consts):
    B, T, _ = x.shape
    tm = min(TOKEN_TILE, T)
    ins = [x] + list(mix_ins) + [mod] + list(consts)
    in_specs = ([_tok_spec(tm, D_MODEL)] + [_tok_spec(tm, a.shape[2]) for a in mix_ins] + [_mod_spec(mod_row0, mod_stride)]
                + [_const_spec(a.shape) for a in consts])
    return pl.pallas_call(
        kernel_fn,
        grid=(B, T // tm),
        in_specs=in_specs,
        out_specs=_tok_spec(tm, D_MODEL),
        out_shape=jax.ShapeDtypeStruct(x.shape, F32),
        compiler_params=_cparams("parallel", "parallel"),
        name=name,
    )(*ins)


def _inproj_odd_kernel(x_ref, mod_ref, g_ref, wmain_ref, wlr_in_ref, wlr_hi_ref, wlr_lo_ref, blr_ref,
                       rx_out, rg_out, gq_out, gk_out, gv_out, gg_out, la_out):
    h = _rms(x_ref[...], g_ref[...]) * (1.0 + mod_ref[1]) + mod_ref[0]
    hb = h.astype(BF16)
    mm = _dot(hb, wmain_ref[...])
    o = 0
    rx_out[...] = mm[:, o:o + LRU_WIDTH]; o += LRU_WIDTH
    rg_out[...] = mm[:, o:o + LRU_WIDTH]; o += LRU_WIDTH
    gq_out[...] = mm[:, o:o + GLA_KW] * GLA_DK ** -0.5; o += GLA_KW
    gk_out[...] = mm[:, o:o + GLA_KW]; o += GLA_KW
    gv_out[...] = mm[:, o:o + GLA_WIDTH].astype(BF16); o += GLA_WIDTH
    gg_out[...] = mm[:, o:o + GLA_WIDTH]
    glr_hi, glr_lo = _split2(_dot(hb, wlr_in_ref[...]))
    z = _dot(glr_hi, wlr_hi_ref[...]) + (_dot(glr_hi, wlr_lo_ref[...]) + _dot(glr_lo, wlr_hi_ref[...])) + blr_ref[...]
    la_out[...] = _log_sigmoid(z) * (1.0 / GLA_TAU)


def _inproj_odd_call(x, mod, mod_row0, mod_stride, prm):
    B, T, _ = x.shape
    tm = min(TOKEN_TILE, T)
    ins = [x, mod, prm["g_mix_pre"], prm["w_main"], prm["w_lr_in"], prm["w_lr_hi"], prm["w_lr_lo"], prm["b_lr"]]
    in_specs = [_tok_spec(tm, D_MODEL), _mod_spec(mod_row0, mod_stride)] + [_const_spec(a.shape) for a in ins[2:]]
    widths = [(LRU_WIDTH, F32), (LRU_WIDTH, F32), (GLA_KW, F32), (GLA_KW, F32), (GLA_WIDTH, BF16), (GLA_WIDTH, F32),
              (N_DIR * GLA_KW, F32)]
    return pl.pallas_call(
        _inproj_odd_kernel,
        grid=(B, T // tm),
        in_specs=in_specs,
        out_specs=[_tok_spec(tm, w) for w, _ in widths],
        out_shape=[jax.ShapeDtypeStruct((B, T, w), dt) for w, dt in widths],
        compiler_params=_cparams("parallel", "parallel"),
        name="inproj_odd",
    )(*ins)


def _lru_kernel(*refs, zero_init):
    if zero_init:
        x_ref, cw_ref, cb_ref, wg_ref, bg_ref, lam_ref, o_ref, last_ref, af_s, uf_s, ab_s, ub_s, hb_s = refs
    else:
        x_ref, cw_ref, cb_ref, wg_ref, bg_ref, lam_ref, h0_ref, o_ref, last_ref, af_s, uf_s, ab_s, ub_s, hb_s = refs
    T, W = x_ref.shape
    tg = min(LRU_GATE_TILE, T)
    sp = _softplus(-lam_ref[...])

    for t0 in range(0, T, tg):
        lo = max(t0 - SUBLANES, 0)
        hi = min(t0 + tg + SUBLANES, T)
        xs = x_ref[lo:hi, :]
        n = hi - lo
        rid = lax.broadcasted_iota(jnp.int32, (n, W), 0) + lo
        xc = cb_ref[...] + xs * cw_ref[2:3, :]
        xc = xc + jnp.where(rid >= 2, pltpu.roll(xs, 2, axis=0), 0.0) * cw_ref[0:1, :]
        xc = xc + jnp.where(rid >= 1, pltpu.roll(xs, 1, axis=0), 0.0) * cw_ref[1:2, :]
        xc = xc + jnp.where(rid <= T - 2, pltpu.roll(xs, n - 1, axis=0), 0.0) * cw_ref[3:4, :]
        xc = xc[t0 - lo:t0 - lo + tg, :]
        gates = jax.nn.sigmoid(_dot(xc.astype(BF16), wg_ref[...]) + bg_ref[...])
        for d, (a_s, u_s) in enumerate(((af_s, uf_s), (ab_s, ub_s))):
            r = gates[:, 2 * d * W:(2 * d + 1) * W]
            i = gates[:, (2 * d + 1) * W:(2 * d + 2) * W]
            log_a = (-LRU_C * r) * sp[d:d + 1, :]
            a = jnp.exp(log_a)
            a_s[t0:t0 + tg, :] = a
            u_s[t0:t0 + tg, :] = jnp.sqrt(-jnp.tanh(log_a) * (1.0 + a * a)) * (i * xc)

    nblk = T // SUBLANES
    rid8 = lax.broadcasted_iota(jnp.int32, (SUBLANES, W), 0)

    def block_scan(a, u, reverse):
        for dist in (1, 2, 4):
            if reverse:
                keep = rid8 < SUBLANES - dist
                shift = SUBLANES - dist
            else:
                keep = rid8 >= dist
                shift = dist
            a_n = jnp.where(keep, pltpu.roll(a, shift, axis=0), 1.0)
            u_n = jnp.where(keep, pltpu.roll(u, shift, axis=0), 0.0)
            u = a * u_n + u
            a = a * a_n
        return a, u

    def body(blk, carry):
        hf, hb = carry
        rf = pl.multiple_of(blk * SUBLANES, SUBLANES)
        a, u = block_scan(af_s[pl.ds(rf, SUBLANES), :], uf_s[pl.ds(rf, SUBLANES), :], False)
        out_f = u + a * hf
        o_ref[pl.ds(rf, SUBLANES), :] = out_f
        rb = pl.multiple_of((nblk - 1 - blk) * SUBLANES, SUBLANES)
        a, u = block_scan(ab_s[pl.ds(rb, SUBLANES), :], ub_s[pl.ds(rb, SUBLANES), :], True)
        out_b = u + a * hb
        hb_s[pl.ds(rb, SUBLANES), :] = out_b
        return out_f[SUBLANES - 1:SUBLANES, :], out_b[0:1, :]

    if zero_init:
        init = (jnp.zeros((1, W), F32), jnp.zeros((1, W), F32))
    else:
        init = (h0_ref[0:1, :], h0_ref[1:2, :])
    hf, hb = lax.fori_loop(0, nblk, body, init)
    last_ref[0:1, :] = hf
    last_ref[1:2, :] = hb
    o_ref[...] = o_ref[...] + hb_s[...]


def _lru_call(rx, prm, h0):
    B, T, W = rx.shape
    zero_init = h0 is None
    seq = pl.BlockSpec((None, T, W), lambda b: (b, 0, 0))
    st = pl.BlockSpec((None, N_DIR, W), lambda b: (b, 0, 0))
    ins = [rx, prm["conv_w"], prm["conv_b"], prm["w_gate"], prm["b_gate"], prm["lam"]]
    in_specs = [seq] + [_const_spec(a.shape) for a in ins[1:]]
    if not zero_init:
        ins.append(h0)
        in_specs.append(st)
    return pl.pallas_call(
        functools.partial(_lru_kernel, zero_init=zero_init),
        grid=(B,),
        in_specs=in_specs,
        out_specs=[seq, st],
        out_shape=[jax.ShapeDtypeStruct((B, T, W), F32), jax.ShapeDtypeStruct((B, N_DIR, W), F32)],
        scratch_shapes=[pltpu.VMEM((T, W), F32)] * 5,
        compiler_params=_cparams("parallel"),
        name="rglru",
    )(*ins)


GLA_LEVELS = 6


def _gla_constants():
    L = GLA_CHUNK
    idx = np.arange(L)
    sel = np.zeros((N_DIR, (GLA_LEVELS + 3) * L, L), np.float32)
    lvl_mask = np.zeros((N_DIR, GLA_LEVELS + 1, L, L), np.float32)
    for d in range(N_DIR):
        tri = (idx[None, :] <= idx[:, None]) if d == 0 else (idx[None, :] >= idx[:, None])
        tri = tri.astype(np.float32)
        last = L - 1 if d == 0 else 0
        sel[d, 0:L] = tri
        for lv in range(GLA_LEVELS):
            m = L >> (lv + 1)
            start = (idx // (2 * m)) * (2 * m)
            second = (idx - start) >= m
            boundary = start + (m - 1 if d == 0 else m)
            query_role = second if d == 0 else ~second
            diff = tri - tri[boundary]
            sel[d, (1 + lv) * L:(2 + lv) * L] = np.where(query_role[:, None], diff, -diff)
            same = start[:, None] == start[None, :]
            lvl_mask[d, lv] = same & query_role[:, None] & ~query_role[None, :]
        lvl_mask[d, GLA_LEVELS] = np.eye(L)
        sel[d, (GLA_LEVELS + 1) * L:(GLA_LEVELS + 2) * L] = tri[last][None, :] - tri
        sel[d, (GLA_LEVELS + 2) * L:] = tri[last][None, :]
    assert sel.min() >= 0.0 and sel.max() <= 1.0
    sel3 = np.concatenate([sel] * 3, axis=2)
    lvl_mask = np.tile(lvl_mask, (1, 1, GLA_HEADS, 1))
    return jnp.asarray(sel3, BF16), jnp.asarray(lvl_mask, F32)


def _gla_kernel(*refs, zero_init):
    if zero_init:
        (qf_ref, kf_ref, vf_ref, af_ref, qb_ref, kb_ref, vb_ref, ab_ref, sel_ref, msk_ref, of_ref, ob_ref, s_ref) = refs
    else:
        (qf_ref, kf_ref, vf_ref, af_ref, qb_ref, kb_ref, vb_ref, ab_ref, sel_ref, msk_ref, s0_ref,
         of_ref, ob_ref, s_ref) = refs
    L = GLA_CHUNK
    H = GLA_HEADS
    KW = GLA_KW
    VW = GLA_WIDTH
    lane_k = lax.broadcasted_iota(jnp.int32, (1, KW), 1) // GLA_DK
    head_k = [(lane_k == h).astype(F32) for h in range(H)]
    blk_r = lax.broadcasted_iota(jnp.int32, (KW, VW), 0) // GLA_DK
    blk_c = lax.broadcasted_iota(jnp.int32, (KW, VW), 1) // GLA_DV
    blk = blk_r == blk_c

    @pl.when(pl.program_id(1) == 0)
    def _init():
        for d in range(N_DIR):
            if zero_init:
                s_ref[d] = jnp.zeros((KW, VW), F32)
            else:
                rows = []
                for h in range(H):
                    pad_l = [jnp.zeros((GLA_DK, GLA_DV), F32)] * h
                    pad_r = [jnp.zeros((GLA_DK, GLA_DV), F32)] * (H - 1 - h)
                    rows.append(jnp.concatenate(pad_l + [s0_ref[d, h]] + pad_r, axis=1))
                s_ref[d] = jnp.concatenate(rows, axis=0)

    dirs = ((qf_ref, kf_ref, vf_ref, af_ref, of_ref), (qb_ref, kb_ref, vb_ref, ab_ref, ob_ref))
    for d, (q_ref, k_ref, v_ref, a_ref, o_ref) in enumerate(dirs):
        q = q_ref[...]
        k = k_ref[...]
        v = v_ref[...]
        hi, mid, lo = _split3(a_ref[...])
        sums = _dot(sel_ref[d], jnp.concatenate([hi, mid, lo], axis=0))

        def stack_heads(a):
            return jnp.concatenate([a * head_k[h] for h in range(H)], axis=0).astype(BF16)

        scores = _dot_nt(stack_heads(q), k.astype(BF16)) * msk_ref[d, GLA_LEVELS]
        for lv in range(GLA_LEVELS):
            wl = jnp.exp(sums[(1 + lv) * L:(2 + lv) * L])
            scores = scores + _dot_nt(stack_heads(q * wl), (k * wl).astype(BF16)) * msk_ref[d, lv]
        s_prev = s_ref[d]
        q_in = (q * jnp.exp(sums[0:L])).astype(BF16)
        out = _dot(q_in, s_prev.astype(BF16))
        intra = [_dot(scores[h * L:(h + 1) * L].astype(BF16), v[:, h * GLA_DV:(h + 1) * GLA_DV]) for h in range(H)]
        o_ref[...] = out + jnp.concatenate(intra, axis=1)

        k_out = k * jnp.exp(sums[(GLA_LEVELS + 1) * L:(GLA_LEVELS + 2) * L])
        dec = jnp.exp(sums[(GLA_LEVELS + 2) * L:])
        both_t = jnp.concatenate([k_out, dec], axis=0).T
        upd = _dot(both_t[:, :L].astype(BF16), v)
        s_ref[d] = both_t[:, L:L + 1] * s_prev + jnp.where(blk, upd, 0.0)


def _gla_call(gq, gk, gv, log_a, sel3, lvl_mask, s0):
    B, T, _ = gq.shape
    L = GLA_CHUNK
    nc = T // L
    zero_init = s0 is None
    fwd = lambda w: pl.BlockSpec((None, L, w), lambda b, c: (b, c, 0))
    bwd = lambda w: pl.BlockSpec((None, L, w), lambda b, c: (b, nc - 1 - c, 0))
    a_fwd = pl.BlockSpec((None, L, GLA_KW), lambda b, c: (b, c, 0))
    a_bwd = pl.BlockSpec((None, L, GLA_KW), lambda b, c: (b, nc - 1 - c, 1))
    s_spec = pl.BlockSpec((None, N_DIR, GLA_KW, GLA_WIDTH), lambda b, c: (b, 0, 0, 0))
    ins = [gq, gk, gv, log_a, gq, gk, gv, log_a, sel3, lvl_mask]
    in_specs = [fwd(GLA_KW), fwd(GLA_KW), fwd(GLA_WIDTH), a_fwd, bwd(GLA_KW), bwd(GLA_KW), bwd(GLA_WIDTH), a_bwd,
                _const_spec(sel3.shape), _const_spec(lvl_mask.shape)]
    if not zero_init:
        ins.append(s0)
        in_specs.append(pl.BlockSpec((None, N_DIR, GLA_HEADS, GLA_DK, GLA_DV), lambda b, c: (b, 0, 0, 0, 0)))
    return pl.pallas_call(
        functools.partial(_gla_kernel, zero_init=zero_init),
        grid=(B, nc),
        in_specs=in_specs,
        out_specs=[fwd(GLA_WIDTH), bwd(GLA_WIDTH), s_spec],
        out_shape=[jax.ShapeDtypeStruct((B, T, GLA_WIDTH), F32), jax.ShapeDtypeStruct((B, T, GLA_WIDTH), F32),
                   jax.ShapeDtypeStruct((B, N_DIR, GLA_KW, GLA_WIDTH), F32)],
        compiler_params=_cparams("parallel", "arbitrary"),
        name="gla",
    )(*ins)


def _block_diag(blocks):
    n, r, c = blocks.shape
    eye = jnp.eye(n, dtype=blocks.dtype)
    return (eye[:, None, :, None] * blocks[:, :, None, :]).reshape(n * r, n * c)


def _pad_cols(a, width):
    return jnp.pad(a, ((0, 0), (0, width - a.shape[1])))


def _rope_tables(n_tokens):
    rows = n_tokens // GRID_W
    row = jnp.repeat(jnp.arange(rows), GRID_W).astype(F32)
    col = jnp.tile(jnp.arange(GRID_W), rows).astype(F32)
    inv = jnp.power(ROPE_THETA, -jnp.arange(ROPE_PAIRS_PER_AXIS, dtype=F32) / ROPE_PAIRS_PER_AXIS)
    ang = jnp.concatenate([row[:, None] * inv, col[:, None] * inv], axis=-1)
    cos, sin = jnp.cos(ang), jnp.sin(ang)
    reps = LANES // HEAD_DIM
    return jnp.tile(jnp.concatenate([cos, cos], axis=-1), (1, reps)), jnp.tile(jnp.concatenate([-sin, sin], axis=-1), (1, reps))


def _cache_variants(cache):
    z = jnp.zeros_like(cache[:, 0])
    return jnp.concatenate([cache[:, 0], z, z, cache[:, 0], cache[:, 1], z, z, cache[:, 1]], axis=-1).astype(BF16)


def _mlstm_tri():
    idx = np.arange(MLSTM_CHUNK)
    lower = idx[None, :] <= idx[:, None]
    return jnp.asarray(np.stack([lower, lower.T]).astype(np.float32), BF16)


def kernel(x_prompt, x_sample, cache_attn_k, cache_attn_v, state_mlstm_C, state_mlstm_n, state_mlstm_m, state_lru_h, state_gla_S, c, c_ctx, ada_w, ada_b, norm_mix_pre, norm_mix_post, norm_ffn_pre, norm_ffn_post, w_out, ffn_w1, ffn_w2, w_in_even, attn_q_norm, attn_k_norm, mlstm_i_bias, mlstm_f_bias, mlstm_norm, w_in_odd, lru_conv_w, lru_conv_b, lru_w_r, lru_b_r, lru_w_i, lru_b_i, lru_lambda, gla_w_lr, gla_b_lr, gla_norm):
    depth = ada_w.shape[0]
    Bp = x_prompt.shape[0]
    Bs = x_sample.shape[0]
    row = lambda a: a.reshape(1, -1)

    n_rows = -(-(1 + Bs) // SUBLANES) * SUBLANES
    cvec = jnp.zeros((n_rows, D_MODEL), F32).at[0].set(c_ctx).at[1:1 + Bs].set(c)
    mod_all = _ada_call(cvec, ada_w, ada_b).reshape(depth, n_rows, 6, 1, D_MODEL)

    rope_tabs = _rope_tables(x_sample.shape[1])
    ones64 = _block_diag(jnp.ones((ATT_HEADS, HEAD_DIM, HEAD_DIM), BF16))
    tri = _mlstm_tri()
    sel3, lvl_mask = _gla_constants()

    xp, xs = x_prompt, x_sample
    outs = {}
    for l in range(depth):
        mod = mod_all[l]
        tail = [row(norm_mix_post[l]), row(norm_ffn_pre[l]), row(norm_ffn_post[l]),
                w_out[l].astype(BF16), ffn_w1[l].astype(BF16), ffn_w2[l].astype(BF16)]
        if l % 2 == 0:
            e = l // 2
            w_in = w_in_even[e]
            o1 = ATT_WIDTH + 2 * ATT_KV_WIDTH
            o2 = o1 + 4 * MLSTM_WIDTH
            gate_pad = LANES - 2 * N_DIR * MLSTM_HEADS
            prm = {
                "g_mix_pre": row(norm_mix_pre[l]),
                "w_qkv": w_in[:, :o1].astype(BF16),
                "w_m": w_in[:, o1:o2].astype(BF16),
                "w_g": _pad_cols(w_in[:, o2:], LANES).astype(BF16),
                "q_gain": row(jnp.tile(attn_q_norm[e], ATT_HEADS)),
                "k_gain": row(jnp.tile(attn_k_norm[e], ATT_KV_HEADS)),
                "i_bias": _pad_cols(row(mlstm_i_bias[e]), LANES),
                "f_bias": jnp.pad(row(mlstm_f_bias[e]), ((0, 0), (N_DIR * MLSTM_HEADS, gate_pad))),
                "ones64": ones64,
            }
            consts = [row(mlstm_norm[e])] + tail
            for path in ("prompt", "sample"):
                if path == "prompt":
                    x, r0, rs, tabs, cache, st0 = xp, 0, 0, None, None, None
                else:
                    x, r0, rs, tabs = xs, 1, 1, rope_tabs
                    cache = (_cache_variants(cache_attn_k[:, e]), _cache_variants(cache_attn_v[:, e]))
                    st0 = (state_mlstm_C[:, e], state_mlstm_n[:, e].reshape(Bs, N_DIR * MLSTM_HEADS, MLSTM_HD),
                           _pad_cols(state_mlstm_m[:, e].reshape(Bs, -1), LANES).reshape(Bs, 1, LANES))
                q, kk, vv, k_n, v_n, mq, mk, mv, mo, gates = _inproj_even_call(x, mod, r0, rs, prm, tabs)
                attn = _attn_call(q, kk, vv, cache)
                hf, hb, c_fin, n_fin, m_fin = _mlstm_call(mq, mk, mv, gates, tri, st0)
                x_new = _post_call(_post_even_kernel, "post_even", x, [attn, hf, hb, mo], mod, r0, rs, consts)
                if path == "prompt":
                    xp = x_new
                    T = x.shape[1]
                    outs.setdefault("k", []).append(jnp.swapaxes(k_n.reshape(Bp, T, ATT_KV_HEADS, HEAD_DIM), 1, 2))
                    outs.setdefault("v", []).append(jnp.swapaxes(v_n.reshape(Bp, T, ATT_KV_HEADS, HEAD_DIM), 1, 2))
                    outs.setdefault("C", []).append(c_fin)
                    outs.setdefault("n", []).append(n_fin.reshape(Bp, N_DIR, MLSTM_HEADS, MLSTM_HD))
                    outs.setdefault("m", []).append(m_fin[:, 0, :N_DIR * MLSTM_HEADS].reshape(Bp, N_DIR, MLSTM_HEADS))
                else:
                    xs = x_new
        else:
            o = l // 2
            w_in = w_in_odd[o]
            o1 = 2 * LRU_WIDTH + 2 * GLA_KW + 2 * GLA_WIDTH
            w_lr = _block_diag(gla_w_lr[o])
            w_lr = jnp.pad(w_lr, ((0, LANES - w_lr.shape[0]), (0, 0)))
            w_lr_hi = w_lr.astype(BF16)
            prm = {
                "g_mix_pre": row(norm_mix_pre[l]),
                "w_main": w_in[:, :o1].astype(BF16),
                "w_lr_in": _pad_cols(w_in[:, o1:], LANES).astype(BF16),
                "w_lr_hi": w_lr_hi,
                "w_lr_lo": (w_lr - w_lr_hi.astype(F32)).astype(BF16),
                "b_lr": row(gla_b_lr[o]),
            }
            lru_prm = {
                "conv_w": lru_conv_w[o],
                "conv_b": row(lru_conv_b[o]),
                "w_gate": jnp.concatenate([_block_diag(lru_w_r[o, 0]), _block_diag(lru_w_i[o, 0]),
                                           _block_diag(lru_w_r[o, 1]), _block_diag(lru_w_i[o, 1])], axis=1).astype(BF16),
                "b_gate": row(jnp.stack([lru_b_r[o, 0], lru_b_i[o, 0], lru_b_r[o, 1], lru_b_i[o, 1]])),
                "lam": lru_lambda[o],
            }
            consts = [row(gla_norm[o])] + tail
            for path in ("prompt", "sample"):
                if path == "prompt":
                    x, r0, rs, h0, s0 = xp, 0, 0, None, None
                else:
                    x, r0, rs, h0, s0 = xs, 1, 1, state_lru_h[:, o], state_gla_S[:, o]
                rx, rg, gq, gk, gv, gg, log_a = _inproj_odd_call(x, mod, r0, rs, prm)
                lru, h_last = _lru_call(rx, lru_prm, h0)
                of, ob, s_fin = _gla_call(gq, gk, gv, log_a, sel3, lvl_mask, s0)
                x_new = _post_call(_post_odd_kernel, "post_odd", x, [lru, rg, of, ob, gg], mod, r0, rs, consts)
                if path == "prompt":
                    xp = x_new
                    outs.setdefault("h", []).append(h_last)
                    s_heads = jnp.stack([s_fin[:, :, h * GLA_DK:(h + 1) * GLA_DK, h * GLA_DV:(h + 1) * GLA_DV]
                                         for h in range(GLA_HEADS)], axis=2)
                    outs.setdefault("S", []).append(s_heads)
                else:
                    xs = x_new

    stack = lambda name: jnp.stack(outs[name], axis=1)
    return (xp, xs, stack("k"), stack("v"), stack("C"), stack("n"), stack("m"), stack("h"), stack("S"))
```

```python
import functools

import numpy as np
import jax
import jax.numpy as jnp
from jax import lax
from jax.experimental import pallas as pl
from jax.experimental.pallas import tpu as pltpu

F32 = jnp.float32
BF16 = jnp.bfloat16

D_MODEL = 1024
D_FF = 4 * D_MODEL
GRID_W = 64
EPS = 1e-6
N_DIR = 2

ATT_HEADS = 8
ATT_KV_HEADS = 2
HEAD_DIM = 64
ATT_WIDTH = ATT_HEADS * HEAD_DIM
ATT_KV_WIDTH = ATT_KV_HEADS * HEAD_DIM
ROPE_THETA = 10000.0
ROPE_PAIRS_PER_AXIS = HEAD_DIM // 4

MLSTM_HEADS = 4
MLSTM_HD = 128
MLSTM_WIDTH = MLSTM_HEADS * MLSTM_HD
MLSTM_CHUNK = 256
MLSTM_STAGE_LAG = 2
M_INIT = -1e30

LRU_WIDTH = 512
LRU_BLOCKS = 8
LRU_BD = LRU_WIDTH // LRU_BLOCKS
LRU_C = 8.0
CONV_W = 4
CONV_LEFT = 2

GLA_HEADS = 4
GLA_DK = 64
GLA_DV = 128
GLA_KW = GLA_HEADS * GLA_DK
GLA_WIDTH = GLA_HEADS * GLA_DV
GLA_RANK = 16
GLA_TAU = 16.0
GLA_CHUNK = 64

LANES = 128
SUBLANES = 8
VMEM_LIMIT = 56 * 1024 * 1024

TOKEN_TILE = 512
ATTN_Q_TILE = 256
FF_CHUNK = 1024
LRU_GATE_TILE = 512


def _cparams(*sem):
    return pltpu.CompilerParams(dimension_semantics=sem, vmem_limit_bytes=VMEM_LIMIT)


def _const_spec(shape):
    n = len(shape)
    return pl.BlockSpec(shape, lambda *_: (0,) * n, pipeline_mode=pl.Buffered(1))


def _dot(a, b):
    return jnp.dot(a, b, preferred_element_type=F32)


def _dot_nt(a, b):
    return lax.dot_general(a, b, (((1,), (1,)), ((), ())), preferred_element_type=F32)


def _dot_tn(a, b):
    return lax.dot_general(a, b, (((0,), (0,)), ((), ())), preferred_element_type=F32)


def _split3(x):
    hi = x.astype(BF16)
    r = x - hi.astype(F32)
    mid = r.astype(BF16)
    lo = (r - mid.astype(F32)).astype(BF16)
    return hi, mid, lo


def _split2(x):
    hi = x.astype(BF16)
    lo = (x - hi.astype(F32)).astype(BF16)
    return hi, lo


def _rms(x, g):
    return x * lax.rsqrt(jnp.mean(x * x, axis=-1, keepdims=True) + EPS) * g


def _log_sigmoid(x):
    return jnp.minimum(x, 0.0) - jnp.log1p(jnp.exp(-jnp.abs(x)))


def _softplus(x):
    return jnp.maximum(x, 0.0) + jnp.log1p(jnp.exp(-jnp.abs(x)))


def _group_rms(y, gain, ones_bd, width):
    hi, lo = _split2(y * y)
    ss = _dot(hi, ones_bd) + _dot(lo, ones_bd)
    return y * lax.rsqrt(ss * (1.0 / width) + EPS) * gain


def _ada_kernel(c_ref, w_ref, b_ref, o_ref):
    s = jax.nn.silu(c_ref[...])
    o_ref[...] = _dot(s.astype(BF16), w_ref[...].astype(BF16)) + b_ref[...]


def _ada_call(cvec, ada_w, ada_b):
    depth = ada_w.shape[0]
    rows = cvec.shape[0]
    n_col = ada_w.shape[2] // D_MODEL
    return pl.pallas_call(
        _ada_kernel,
        grid=(depth, n_col),
        in_specs=[
            pl.BlockSpec((rows, D_MODEL), lambda l, j: (0, 0)),
            pl.BlockSpec((None, D_MODEL, D_MODEL), lambda l, j: (l, 0, j)),
            pl.BlockSpec((None, 1, D_MODEL), lambda l, j: (l, 0, j)),
        ],
        out_specs=pl.BlockSpec((None, rows, D_MODEL), lambda l, j: (l, 0, j)),
        out_shape=jax.ShapeDtypeStruct((depth, rows, ada_w.shape[2]), F32),
        compiler_params=_cparams("arbitrary", "arbitrary"),
        name="ada_mod",
    )(cvec, ada_w, ada_b.reshape(depth, 1, -1))


def _mod_spec(row0, row_stride):
    return pl.BlockSpec((None, 6, 1, D_MODEL), lambda b, i: (row0 + row_stride * b, 0, 0, 0))


def _tok_spec(tm, width):
    return pl.BlockSpec((None, tm, width), lambda b, i: (b, i, 0))


def _rope(y, cos, sin_signed):
    width = y.shape[1]
    reps = width // LANES
    cosw = jnp.concatenate([cos] * reps, axis=1) if reps > 1 else cos
    sinw = jnp.concatenate([sin_signed] * reps, axis=1) if reps > 1 else sin_signed
    lane = lax.broadcasted_iota(jnp.int32, y.shape, 1)
    first_half = (lane % HEAD_DIM) < (HEAD_DIM // 2)
    partner = jnp.where(first_half, pltpu.roll(y, width - HEAD_DIM // 2, axis=1), pltpu.roll(y, HEAD_DIM // 2, axis=1))
    return y * cosw + partner * sinw


def _kv_variants(a):
    lane = lax.broadcasted_iota(jnp.int32, a.shape, 1)
    low = lane < HEAD_DIM
    swapped = pltpu.roll(a, HEAD_DIM, axis=1)
    zero = jnp.zeros_like(a)
    return jnp.concatenate([
        jnp.where(low, a, zero),
        jnp.where(low, zero, swapped),
        jnp.where(low, swapped, zero),
        jnp.where(low, zero, a),
    ], axis=1)


def _inproj_even_kernel(*refs, rope):
    if rope:
        (x_ref, mod_ref, g_ref, wqkv_ref, wm_ref, wmt_ref, wgt_ref, qg_ref, kg_ref, gbt_ref, ones_ref, cos_ref, sin_ref,
         q_out, kk_out, vv_out, k_out, v_out, mqt_out, mk_out, mvt_out, mo_out, gatet_out) = refs
    else:
        (x_ref, mod_ref, g_ref, wqkv_ref, wm_ref, wmt_ref, wgt_ref, qg_ref, kg_ref, gbt_ref, ones_ref,
         q_out, kk_out, vv_out, k_out, v_out, mqt_out, mk_out, mvt_out, mo_out, gatet_out) = refs
    h = _rms(x_ref[...], g_ref[...]) * (1.0 + mod_ref[1]) + mod_ref[0]
    hb = h.astype(BF16)

    qkv = _dot(hb, wqkv_ref[...])
    q = _group_rms(qkv[:, :ATT_WIDTH], qg_ref[...], ones_ref[...], HEAD_DIM)
    k = _group_rms(qkv[:, ATT_WIDTH:ATT_WIDTH + ATT_KV_WIDTH], kg_ref[...], ones_ref[:ATT_KV_WIDTH, :ATT_KV_WIDTH], HEAD_DIM)
    v = qkv[:, ATT_WIDTH + ATT_KV_WIDTH:]
    if rope:
        q = _rope(q, cos_ref[...], sin_ref[...])
        k = _rope(k, cos_ref[...], sin_ref[...])
    q_out[...] = (q * HEAD_DIM ** -0.5).astype(BF16)
    k_out[...] = k
    v_out[...] = v
    kk_out[...] = _kv_variants(k).astype(BF16)
    vv_out[...] = _kv_variants(v).astype(BF16)

    w = MLSTM_WIDTH
    mm = _dot(hb, wm_ref[...])
    mk_out[...] = mm[:, :w].astype(BF16)
    mo_out[...] = mm[:, w:]
    mmt = _dot_nt(wmt_ref[...], hb)
    mqt_out[...] = (mmt[:w] * MLSTM_HD ** -0.5).astype(BF16)
    mvt_out[...] = mmt[w:].astype(BF16)

    gates_t = _dot_nt(wgt_ref[...], hb) + gbt_ref[...]
    sub = lax.broadcasted_iota(jnp.int32, gates_t.shape, 0)
    gatet_out[...] = jnp.where(sub < N_DIR * MLSTM_HEADS, gates_t, _log_sigmoid(gates_t))


def _inproj_even_call(x, mod, mod_row0, mod_stride, prm, rope_tabs):
    B, T, _ = x.shape
    tm = min(TOKEN_TILE, T)
    rope = rope_tabs is not None
    ins = [x, mod, prm["g_mix_pre"], prm["w_qkv"], prm["w_m"], prm["w_mt"], prm["w_gt"], prm["q_gain"],
           prm["k_gain"], prm["gate_bias_t"], prm["ones64"]]
    in_specs = [_tok_spec(tm, D_MODEL), _mod_spec(mod_row0, mod_stride)] + [_const_spec(a.shape) for a in ins[2:]]
    if rope:
        ins += list(rope_tabs)
        in_specs += [pl.BlockSpec((tm, LANES), lambda b, i: (i, 0))] * 2
    n_gate = 2 * N_DIR * MLSTM_HEADS
    outs = [(ATT_WIDTH, BF16, True), (4 * LANES, BF16, True), (4 * LANES, BF16, True), (ATT_KV_WIDTH, F32, True),
            (ATT_KV_WIDTH, F32, True), (MLSTM_WIDTH, BF16, False), (MLSTM_WIDTH, BF16, True), (MLSTM_WIDTH, BF16, False),
            (MLSTM_WIDTH, F32, True), (n_gate, F32, False)]
    feat_spec = lambda w: pl.BlockSpec((None, w, tm), lambda b, i: (b, 0, i))
    return pl.pallas_call(
        functools.partial(_inproj_even_kernel, rope=rope),
        grid=(B, T // tm),
        in_specs=in_specs,
        out_specs=[_tok_spec(tm, w) if tok else feat_spec(w) for w, _, tok in outs],
        out_shape=[jax.ShapeDtypeStruct((B, T, w) if tok else (B, w, T), dt) for w, dt, tok in outs],
        compiler_params=_cparams("parallel", "parallel"),
        name="inproj_even",
    )(*ins)


def _attn_kernel(*refs, cached):
    if cached:
        q_ref, kk_ref, vv_ref, kc_ref, vc_ref, o_ref = refs
    else:
        q_ref, kk_ref, vv_ref, o_ref = refs
    for pair in range(ATT_HEADS // 2):
        qp = q_ref[:, pair * LANES:(pair + 1) * LANES]
        acc = None
        for half in range(2):
            head = 2 * pair + half
            kv = head // (ATT_HEADS // ATT_KV_HEADS)
            col = (2 * kv + half) * LANES
            s = _dot_nt(qp, kk_ref[:, col:col + LANES])
            m = jnp.max(s, axis=-1, keepdims=True)
            if cached:
                sc = _dot_nt(qp, kc_ref[:, col:col + LANES])
                m = jnp.maximum(m, jnp.max(sc, axis=-1, keepdims=True))
            p = jnp.exp(s - m)
            den = jnp.sum(p, axis=-1, keepdims=True)
            o = _dot(p.astype(BF16), vv_ref[:, col:col + LANES])
            if cached:
                pc = jnp.exp(sc - m)
                den = den + jnp.sum(pc, axis=-1, keepdims=True)
                o = o + _dot(pc.astype(BF16), vc_ref[:, col:col + LANES])
            o = o / den
            acc = o if acc is None else acc + o
        o_ref[:, pair * LANES:(pair + 1) * LANES] = acc.astype(o_ref.dtype)


def _attn_call(q, kk, vv, cache=None):
    B, T, _ = q.shape
    tq = min(ATTN_Q_TILE, T)
    cached = cache is not None
    ins = [q, kk, vv]
    full = lambda n: pl.BlockSpec((None, n, 4 * LANES), lambda b, i: (b, 0, 0))
    in_specs = [_tok_spec(tq, ATT_WIDTH), full(T), full(T)]
    if cached:
        ins += list(cache)
        in_specs += [full(cache[0].shape[1])] * 2
    return pl.pallas_call(
        functools.partial(_attn_kernel, cached=cached),
        grid=(B, T // tq),
        in_specs=in_specs,
        out_specs=_tok_spec(tq, ATT_WIDTH),
        out_shape=jax.ShapeDtypeStruct((B, T, ATT_WIDTH), BF16),
        compiler_params=_cparams("parallel", "parallel"),
        name="attention",
    )(*ins)


def _mlstm_kernel(*refs, zero_init):
    if zero_init:
        (qf_ref, kf_ref, vf_ref, gtf_ref, qb_ref, kb_ref, vb_ref, gtb_ref, trit_ref,
         hf_ref, hb_ref, c_ref, n_ref, m_ref) = refs
    else:
        (qf_ref, kf_ref, vf_ref, gtf_ref, qb_ref, kb_ref, vb_ref, gtb_ref, trit_ref,
         c0_ref, n0_ref, m0_ref, hf_ref, hb_ref, c_ref, n_ref, m_ref) = refs
    L = MLSTM_CHUNK
    H = MLSTM_HEADS
    HD = MLSTM_HD
    PAD = 2 * SUBLANES
    chunk = pl.program_id(1)

    @pl.when(chunk == 0)
    def _init():
        if zero_init:
            c_ref[...] = jnp.zeros(c_ref.shape, F32)
            n_ref[...] = jnp.zeros(n_ref.shape, F32)
            m_ref[...] = jnp.full(m_ref.shape, M_INIT, F32)
        else:
            for d in range(N_DIR):
                for hd in range(H):
                    c_ref[d, hd] = c0_ref[d, hd].T
            n_ref[...] = n0_ref[...]
            m_ref[...] = m0_ref[...]

    m_all = m_ref[...]
    n_all = n_ref[...]
    m_row = m_all
    m_lane = lax.broadcasted_iota(jnp.int32, m_all.shape, 1)
    key = lax.broadcasted_iota(jnp.int32, (L, L), 0)
    qry = lax.broadcasted_iota(jnp.int32, (L, L), 1)
    dirs = ((qf_ref, kf_ref, vf_ref, gtf_ref, hf_ref, key <= qry, L - 1),
            (qb_ref, kb_ref, vb_ref, gtb_ref, hb_ref, key >= qry, 0))
    n_in = N_DIR * H
    sums = []
    for d, (_, _, _, gt_ref, _, _, _) in enumerate(dirs):
        gates_t = gt_ref[...]
        bcum_t = _dot(jnp.concatenate(_split3(gates_t), axis=1), trit_ref[d])
        sums.append((gates_t, bcum_t))
    c_rows = [sums[d][0][d * H:(d + 1) * H] - sums[d][1][n_in + d * H:n_in + (d + 1) * H] for d in range(N_DIR)]
    c_cols = jnp.concatenate(c_rows, axis=0).T

    def first_stage(d, hd):
        qt_ref, k_ref, vt_ref, _, _, mask, last = dirs[d]
        gates_t, bcum_t = sums[d]
        j = d * H + hd
        jf = n_in + j
        sl = slice(hd * HD, (hd + 1) * HD)
        k = k_ref[:, sl]
        q_t = qt_ref[sl, :]
        v_t = vt_ref[sl, :]
        b_row = bcum_t[jf:jf + 1, :]
        i_row = gates_t[j:j + 1, :]
        c_col = c_cols[:, j:j + 1]
        m_prev = m_all[0:1, j:j + 1]
        ct_prev = c_ref[d, hd]
        n_prev = n_all[j:j + 1, :]

        g = b_row + m_prev
        dlog = jnp.where(mask, c_col + b_row, -jnp.inf)
        m_t = jnp.maximum(g, jnp.max(dlog, axis=0, keepdims=True))
        w = jnp.exp(dlog - m_t)
        w_inter = jnp.exp(g - m_t)
        n_blk = jnp.broadcast_to(n_prev, (PAD, HD)).astype(BF16)
        r = _dot(jnp.concatenate([k, ct_prev.astype(BF16), n_blk], axis=0), q_t)

        b_last = bcum_t[jf:jf + 1, last:last + 1]
        wlog = b_last - b_row + i_row
        m_new = jnp.maximum(b_last + m_prev, jnp.max(wlog, axis=1, keepdims=True))
        ws = jnp.exp(wlog - m_new)
        decay = jnp.exp(b_last + m_prev - m_new)
        vw = (v_t.astype(F32) * ws).astype(BF16)
        ws_blk = jnp.broadcast_to(ws, (PAD, L)).astype(BF16)
        upd = _dot(jnp.concatenate([vw, ws_blk], axis=0), k)
        c_new = decay * ct_prev + upd[:HD]
        n_new = decay * n_prev + upd[HD:HD + 1]
        return dict(d=d, hd=hd, j=j, r=r, w=w, w_inter=w_inter, m_t=m_t, v_t=v_t, c_new=c_new, n_new=n_new, m_new=m_new)

    def second_stage(s):
        h_ref = dirs[s["d"]][4]
        sl = slice(s["hd"] * HD, (s["hd"] + 1) * HD)
        r = s["r"]
        qk = r[:L] * s["w"]
        num = _dot(s["v_t"], qk.astype(BF16)) + s["w_inter"] * r[L:L + HD]
        den = jnp.sum(qk, axis=0, keepdims=True) + s["w_inter"] * r[L + HD:L + HD + 1]
        h_ref[:, sl] = (num / jnp.maximum(jnp.abs(den), jnp.exp(-s["m_t"]))).T

    order = [(d, hd) for d in range(N_DIR) for hd in range(H)]
    staged = []
    for idx, (d, hd) in enumerate(order):
        staged.append(first_stage(d, hd))
        if idx >= MLSTM_STAGE_LAG:
            second_stage(staged[idx - MLSTM_STAGE_LAG])
    for s in staged[len(order) - MLSTM_STAGE_LAG:]:
        second_stage(s)
    for s in staged:
        c_ref[s["d"], s["hd"]] = s["c_new"]
        m_row = jnp.where(m_lane == s["j"], s["m_new"], m_row)
    n_ref[...] = jnp.concatenate([s["n_new"] for s in staged], axis=0)
    m_ref[...] = m_row

    @pl.when(chunk == pl.num_programs(1) - 1)
    def _finish():
        for d in range(N_DIR):
            for hd in range(H):
                c_ref[d, hd] = c_ref[d, hd].T


def _mlstm_call(mq_t, mk, mv_t, gates_t, tri_t, state0):
    B, T, _ = mk.shape
    L = MLSTM_CHUNK
    nc = T // L
    zero_init = state0 is None
    fwd = lambda w: pl.BlockSpec((None, L, w), lambda b, c: (b, c, 0))
    bwd = lambda w: pl.BlockSpec((None, L, w), lambda b, c: (b, nc - 1 - c, 0))
    fwd_t = lambda w: pl.BlockSpec((None, w, L), lambda b, c: (b, 0, c))
    bwd_t = lambda w: pl.BlockSpec((None, w, L), lambda b, c: (b, 0, nc - 1 - c))
    W = MLSTM_WIDTH
    NG = 2 * N_DIR * MLSTM_HEADS
    c_spec = pl.BlockSpec((None, N_DIR, MLSTM_HEADS, MLSTM_HD, MLSTM_HD), lambda b, c: (b, 0, 0, 0, 0))
    n_spec = pl.BlockSpec((None, N_DIR * MLSTM_HEADS, MLSTM_HD), lambda b, c: (b, 0, 0))
    m_spec = pl.BlockSpec((None, 1, LANES), lambda b, c: (b, 0, 0))
    ins = [mq_t, mk, mv_t, gates_t, mq_t, mk, mv_t, gates_t, tri_t]
    in_specs = [fwd_t(W), fwd(W), fwd_t(W), fwd_t(NG), bwd_t(W), bwd(W), bwd_t(W), bwd_t(NG), _const_spec(tri_t.shape)]
    if not zero_init:
        ins += list(state0)
        in_specs += [c_spec, n_spec, m_spec]
    return pl.pallas_call(
        functools.partial(_mlstm_kernel, zero_init=zero_init),
        grid=(B, nc),
        in_specs=in_specs,
        out_specs=[fwd(W), bwd(W), c_spec, n_spec, m_spec],
        out_shape=[jax.ShapeDtypeStruct((B, T, W), F32), jax.ShapeDtypeStruct((B, T, W), F32),
                   jax.ShapeDtypeStruct((B, N_DIR, MLSTM_HEADS, MLSTM_HD, MLSTM_HD), F32),
                   jax.ShapeDtypeStruct((B, N_DIR * MLSTM_HEADS, MLSTM_HD), F32),
                   jax.ShapeDtypeStruct((B, 1, LANES), F32)],
        compiler_params=_cparams("parallel", "arbitrary"),
        name="mlstm",
    )(*ins)


def _residual_ffn(x, mix_a, mix_b, mod_ref, gpost_ref, gfpre_ref, gfpost_ref, wout_ref, w1_ref, w2_ref, o_ref):
    half = D_MODEL // 2
    y = _dot(mix_a.astype(BF16), wout_ref[:half, :]) + _dot(mix_b.astype(BF16), wout_ref[half:, :])
    x1 = x + mod_ref[2] * _rms(y, gpost_ref[...])
    h2 = (_rms(x1, gfpre_ref[...]) * (1.0 + mod_ref[4]) + mod_ref[3]).astype(BF16)
    f = None
    for c0 in range(0, D_FF, FF_CHUNK):
        a = jnp.square(jnp.maximum(_dot(h2, w1_ref[:, c0:c0 + FF_CHUNK]), 0.0)).astype(BF16)
        part = _dot(a, w2_ref[c0:c0 + FF_CHUNK, :])
        f = part if f is None else f + part
    o_ref[...] = x1 + mod_ref[5] * _rms(f, gfpost_ref[...])


def _head_rms(y, gain_ref, width):
    outs = []
    for h0 in range(0, y.shape[1], width):
        seg = y[:, h0:h0 + width]
        outs.append(seg * lax.rsqrt(jnp.mean(seg * seg, axis=-1, keepdims=True) + EPS) * gain_ref[:, h0:h0 + width])
    return jnp.concatenate(outs, axis=1)


def _post_even_kernel(x_ref, attn_ref, hf_ref, hb_ref, mo_ref, mod_ref, gm_ref, gpost_ref, gfpre_ref, gfpost_ref,
                      wout_ref, w1_ref, w2_ref, o_ref):
    hm = _head_rms(hf_ref[...] + hb_ref[...], gm_ref, MLSTM_HD) * jax.nn.sigmoid(mo_ref[...])
    _residual_ffn(x_ref[...], attn_ref[...], hm, mod_ref, gpost_ref, gfpre_ref, gfpost_ref, wout_ref, w1_ref, w2_ref, o_ref)


def _post_odd_kernel(x_ref, lru_ref, rg_ref, of_ref, ob_ref, gg_ref, mod_ref, gm_ref, gpost_ref, gfpre_ref, gfpost_ref,
                     wout_ref, w1_ref, w2_ref, o_ref):
    lru = jax.nn.gelu(rg_ref[...]) * lru_ref[...]
    og = _head_rms(of_ref[...] + ob_ref[...], gm_ref, GLA_DV) * jax.nn.silu(gg_ref[...])
    _residual_ffn(x_ref[...], lru, og, mod_ref, gpost_ref, gfpre_ref, gfpost_ref, wout_ref, w1_ref, w2_ref, o_ref)


def _post_call(kernel_fn, name, x, mix_ins, mod, mod_row0, mod_stride, consts):
    B, T, _ = x.shape
    tm = min(TOKEN_TILE, T)
    ins = [x] + list(mix_ins) + [mod] + list(consts)
    in_specs = ([_tok_spec(tm, D_MODEL)] + [_tok_spec(tm, a.shape[2]) for a in mix_ins] + [_mod_spec(mod_row0, mod_stride)]
                + [_const_spec(a.shape) for a in consts])
    return pl.pallas_call(
        kernel_fn,
        grid=(B, T // tm),
        in_specs=in_specs,
        out_specs=_tok_spec(tm, D_MODEL),
        out_shape=jax.ShapeDtypeStruct(x.shape, F32),
        compiler_params=_cparams("parallel", "parallel"),
        name=name,
    )(*ins)


def _inproj_odd_kernel(x_ref, mod_ref, g_ref, wmain_ref, wlr_in_ref, wlr_hi_ref, wlr_lo_ref, blr_ref,
                       rx_out, rg_out, gq_out, gk_out, gv_out, gg_out, la_out):
    h = _rms(x_ref[...], g_ref[...]) * (1.0 + mod_ref[1]) + mod_ref[0]
    hb = h.astype(BF16)
    mm = _dot(hb, wmain_ref[...])
    o = 0
    rx_out[...] = mm[:, o:o + LRU_WIDTH]; o += LRU_WIDTH
    rg_out[...] = mm[:, o:o + LRU_WIDTH]; o += LRU_WIDTH
    gq_out[...] = mm[:, o:o + GLA_KW] * GLA_DK ** -0.5; o += GLA_KW
    gk_out[...] = mm[:, o:o + GLA_KW]; o += GLA_KW
    gv_out[...] = mm[:, o:o + GLA_WIDTH].astype(BF16); o += GLA_WIDTH
    gg_out[...] = mm[:, o:o + GLA_WIDTH]
    glr_hi, glr_lo = _split2(_dot(hb, wlr_in_ref[...]))
    z = _dot(glr_hi, wlr_hi_ref[...]) + (_dot(glr_hi, wlr_lo_ref[...]) + _dot(glr_lo, wlr_hi_ref[...])) + blr_ref[...]
    la_out[...] = _log_sigmoid(z) * (1.0 / GLA_TAU)


def _inproj_odd_call(x, mod, mod_row0, mod_stride, prm):
    B, T, _ = x.shape
    tm = min(TOKEN_TILE, T)
    ins = [x, mod, prm["g_mix_pre"], prm["w_main"], prm["w_lr_in"], prm["w_lr_hi"], prm["w_lr_lo"], prm["b_lr"]]
    in_specs = [_tok_spec(tm, D_MODEL), _mod_spec(mod_row0, mod_stride)] + [_const_spec(a.shape) for a in ins[2:]]
    widths = [(LRU_WIDTH, F32), (LRU_WIDTH, F32), (GLA_KW, F32), (GLA_KW, F32), (GLA_WIDTH, BF16), (GLA_WIDTH, F32),
              (N_DIR * GLA_KW, F32)]
    return pl.pallas_call(
        _inproj_odd_kernel,
        grid=(B, T // tm),
        in_specs=in_specs,
        out_specs=[_tok_spec(tm, w) for w, _ in widths],
        out_shape=[jax.ShapeDtypeStruct((B, T, w), dt) for w, dt in widths],
        compiler_params=_cparams("parallel", "parallel"),
        name="inproj_odd",
    )(*ins)


def _lru_kernel(*refs, zero_init):
    if zero_init:
        x_ref, cw_ref, cb_ref, wg_ref, bg_ref, lam_ref, o_ref, last_ref, af_s, uf_s, ab_s, ub_s, hb_s = refs
    else:
        x_ref, cw_ref, cb_ref, wg_ref, bg_ref, lam_ref, h0_ref, o_ref, last_ref, af_s, uf_s, ab_s, ub_s, hb_s = refs
    T, W = x_ref.shape
    tg = min(LRU_GATE_TILE, T)
    neg_half_c_sp = (-0.5 * LRU_C) * _softplus(-lam_ref[...])

    for t0 in range(0, T, tg):
        xs = x_ref[t0:t0 + tg, :]
        rid = lax.broadcasted_iota(jnp.int32, (tg, W), 0)
        xc = cb_ref[...] + xs * cw_ref[CONV_LEFT:CONV_LEFT + 1, :]
        for tap in range(CONV_W):
            off = tap - CONV_LEFT
            if off == 0:
                continue
            if 0 <= t0 + off and t0 + off + tg <= T:
                win = x_ref[t0 + off:t0 + off + tg, :]
            elif off < 0:
                win = jnp.where(rid >= -off, pltpu.roll(xs, -off, axis=0), 0.0)
            else:
                win = jnp.where(rid < tg - off, pltpu.roll(xs, tg - off, axis=0), 0.0)
            xc = xc + win * cw_ref[tap:tap + 1, :]
        th = jnp.tanh(_dot(xc.astype(BF16), wg_ref[...]) + bg_ref[...])
        half_xc = 0.5 * xc
        for d, (a_s, u_s) in enumerate(((af_s, uf_s), (ab_s, ub_s))):
            th_r = th[:, 2 * d * W:(2 * d + 1) * W]
            th_i = th[:, (2 * d + 1) * W:(2 * d + 2) * W]
            log_a = neg_half_c_sp[d:d + 1, :] * th_r + neg_half_c_sp[d:d + 1, :]
            a = jnp.exp(log_a)
            a_s[t0:t0 + tg, :] = a
            v = jnp.tanh(log_a) * (-1.0 - a * a)
            u_s[t0:t0 + tg, :] = jnp.where(v > 0.0, v * lax.rsqrt(v), 0.0) * (half_xc * th_i + half_xc)

    nblk = T // SUBLANES
    rid8 = lax.broadcasted_iota(jnp.int32, (SUBLANES, W), 0)

    def block_scan(a, u, reverse):
        for dist in (1, 2, 4):
            if reverse:
                keep = rid8 < SUBLANES - dist
                shift = SUBLANES - dist
            else:
                keep = rid8 >= dist
                shift = dist
            a_n = jnp.where(keep, pltpu.roll(a, shift, axis=0), 1.0)
            u_n = jnp.where(keep, pltpu.roll(u, shift, axis=0), 0.0)
            u = a * u_n + u
            a = a * a_n
        return a, u

    def body(blk, carry):
        hf, hb = carry
        rf = pl.multiple_of(blk * SUBLANES, SUBLANES)
        a, u = block_scan(af_s[pl.ds(rf, SUBLANES), :], uf_s[pl.ds(rf, SUBLANES), :], False)
        out_f = u + a * hf
        o_ref[pl.ds(rf, SUBLANES), :] = out_f
        rb = pl.multiple_of((nblk - 1 - blk) * SUBLANES, SUBLANES)
        a, u = block_scan(ab_s[pl.ds(rb, SUBLANES), :], ub_s[pl.ds(rb, SUBLANES), :], True)
        out_b = u + a * hb
        hb_s[pl.ds(rb, SUBLANES), :] = out_b
        return out_f[SUBLANES - 1:SUBLANES, :], out_b[0:1, :]

    if zero_init:
        init = (jnp.zeros((1, W), F32), jnp.zeros((1, W), F32))
    else:
        init = (h0_ref[0:1, :], h0_ref[1:2, :])
    hf, hb = lax.fori_loop(0, nblk, body, init)
    last_ref[0:1, :] = hf
    last_ref[1:2, :] = hb
    o_ref[...] = o_ref[...] + hb_s[...]


def _lru_call(rx, prm, h0):
    B, T, W = rx.shape
    zero_init = h0 is None
    seq = pl.BlockSpec((None, T, W), lambda b: (b, 0, 0))
    st = pl.BlockSpec((None, N_DIR, W), lambda b: (b, 0, 0))
    ins = [rx, prm["conv_w"], prm["conv_b"], prm["w_gate"], prm["b_gate"], prm["lam"]]
    in_specs = [seq] + [_const_spec(a.shape) for a in ins[1:]]
    if not zero_init:
        ins.append(h0)
        in_specs.append(st)
    return pl.pallas_call(
        functools.partial(_lru_kernel, zero_init=zero_init),
        grid=(B,),
        in_specs=in_specs,
        out_specs=[seq, st],
        out_shape=[jax.ShapeDtypeStruct((B, T, W), F32), jax.ShapeDtypeStruct((B, N_DIR, W), F32)],
        scratch_shapes=[pltpu.VMEM((T, W), F32)] * 5,
        compiler_params=_cparams("parallel"),
        name="rglru",
    )(*ins)


GLA_LEVELS = 6


def _gla_constants():
    L = GLA_CHUNK
    idx = np.arange(L)
    sel = np.zeros((N_DIR, (GLA_LEVELS + 3) * L, L), np.float32)
    lvl_mask = np.zeros((N_DIR, GLA_LEVELS + 1, L, L), np.float32)
    for d in range(N_DIR):
        tri = (idx[None, :] <= idx[:, None]) if d == 0 else (idx[None, :] >= idx[:, None])
        tri = tri.astype(np.float32)
        last = L - 1 if d == 0 else 0
        sel[d, 0:L] = tri
        for lv in range(GLA_LEVELS):
            m = L >> (lv + 1)
            start = (idx // (2 * m)) * (2 * m)
            second = (idx - start) >= m
            boundary = start + (m - 1 if d == 0 else m)
            query_role = second if d == 0 else ~second
            diff = tri - tri[boundary]
            sel[d, (1 + lv) * L:(2 + lv) * L] = np.where(query_role[:, None], diff, -diff)
            same = start[:, None] == start[None, :]
            lvl_mask[d, lv] = same & query_role[:, None] & ~query_role[None, :]
        lvl_mask[d, GLA_LEVELS] = np.eye(L)
        sel[d, (GLA_LEVELS + 1) * L:(GLA_LEVELS + 2) * L] = tri[last][None, :] - tri
        sel[d, (GLA_LEVELS + 2) * L:] = tri[last][None, :]
    assert sel.min() >= 0.0 and sel.max() <= 1.0
    sel3 = np.concatenate([sel] * 3, axis=2)
    lvl_mask = np.tile(lvl_mask, (1, 1, 1, GLA_HEADS))
    return jnp.asarray(sel3, BF16), jnp.asarray(lvl_mask, F32)


def _gla_kernel(*refs, zero_init):
    if zero_init:
        (qf_ref, kf_ref, vf_ref, af_ref, qb_ref, kb_ref, vb_ref, ab_ref, sel_ref, msk_ref, of_ref, ob_ref, s_ref) = refs
    else:
        (qf_ref, kf_ref, vf_ref, af_ref, qb_ref, kb_ref, vb_ref, ab_ref, sel_ref, msk_ref, s0_ref,
         of_ref, ob_ref, s_ref) = refs
    L = GLA_CHUNK
    H = GLA_HEADS
    KW = GLA_KW
    VW = GLA_WIDTH
    DK = GLA_DK
    DV = GLA_DV

    @pl.when(pl.program_id(1) == 0)
    def _init():
        if zero_init:
            s_ref[...] = jnp.zeros(s_ref.shape, F32)
        else:
            s_ref[...] = s0_ref[...]

    zero_k = jnp.zeros((L, KW), BF16)
    zero_v = jnp.zeros((L, VW), BF16)
    lane_k = lax.broadcasted_iota(jnp.int32, (L, KW), 1) // DK
    lane_v = lax.broadcasted_iota(jnp.int32, (L, VW), 1) // DV

    def stack_heads(a, lane_head, zero):
        return jnp.concatenate([jnp.where(lane_head == h, a, zero) for h in range(H)], axis=0)

    def tile_rows(a):
        return jnp.concatenate([a] * H, axis=0)

    dirs = ((qf_ref, kf_ref, vf_ref, af_ref, of_ref), (qb_ref, kb_ref, vb_ref, ab_ref, ob_ref))
    work = []
    for d, (q_ref, k_ref, v_ref, a_ref, _) in enumerate(dirs):
        sums = _dot(sel_ref[d], jnp.concatenate(_split3(a_ref[...]), axis=0))
        q_b = q_ref[...].astype(BF16)
        work.append(dict(sums=sums, q_b=q_b, q_st=stack_heads(q_b, lane_k, zero_k),
                         k_st=stack_heads(k_ref[...].astype(BF16), lane_k, zero_k)))

    for lv in range(GLA_LEVELS + 1):
        for d, wk in enumerate(work):
            if lv == GLA_LEVELS:
                part = _dot_nt(wk["q_b"], wk["k_st"])
            else:
                wl = jnp.exp(wk["sums"][(1 + lv) * L:(2 + lv) * L]).astype(BF16)
                part = _dot_nt(wk["q_b"] * wl, wk["k_st"] * tile_rows(wl))
            part = part * msk_ref[d, lv]
            wk["scores"] = part if lv == 0 else wk["scores"] + part

    for d, (q_ref, k_ref, v_ref, a_ref, o_ref) in enumerate(dirs):
        wk = work[d]
        sums = wk["sums"]
        v = v_ref[...]
        s_prev = s_ref[d]
        intra = _dot(wk["scores"].astype(BF16), stack_heads(v, lane_v, zero_v))
        q_in = wk["q_st"] * tile_rows(jnp.exp(sums[0:L]).astype(BF16))
        inter = _dot(q_in, s_prev.astype(BF16))
        o_ref[...] = intra + jnp.concatenate([inter[h * L:(h + 1) * L] for h in range(H)], axis=1)

        k_out = k_ref[...] * jnp.exp(sums[(GLA_LEVELS + 1) * L:(GLA_LEVELS + 2) * L])
        dec = jnp.exp(sums[(GLA_LEVELS + 2) * L:])
        both_t = jnp.concatenate([k_out, dec], axis=0).T
        k_out_t = both_t[:, :L].astype(BF16)
        upd = [_dot(k_out_t[h * DK:(h + 1) * DK], v[:, h * DV:(h + 1) * DV]) for h in range(H)]
        s_ref[d] = both_t[:, L:L + 1] * s_prev + jnp.concatenate(upd, axis=0)


def _gla_call(gq, gk, gv, log_a, sel3, lvl_mask, s0):
    B, T, _ = gq.shape
    L = GLA_CHUNK
    nc = T // L
    zero_init = s0 is None
    fwd = lambda w: pl.BlockSpec((None, L, w), lambda b, c: (b, c, 0))
    bwd = lambda w: pl.BlockSpec((None, L, w), lambda b, c: (b, nc - 1 - c, 0))
    a_fwd = pl.BlockSpec((None, L, GLA_KW), lambda b, c: (b, c, 0))
    a_bwd = pl.BlockSpec((None, L, GLA_KW), lambda b, c: (b, nc - 1 - c, 1))
    s_spec = pl.BlockSpec((None, N_DIR, GLA_KW, GLA_DV), lambda b, c: (b, 0, 0, 0))
    ins = [gq, gk, gv, log_a, gq, gk, gv, log_a, sel3, lvl_mask]
    in_specs = [fwd(GLA_KW), fwd(GLA_KW), fwd(GLA_WIDTH), a_fwd, bwd(GLA_KW), bwd(GLA_KW), bwd(GLA_WIDTH), a_bwd,
                _const_spec(sel3.shape), _const_spec(lvl_mask.shape)]
    if not zero_init:
        ins.append(s0)
        in_specs.append(s_spec)
    return pl.pallas_call(
        functools.partial(_gla_kernel, zero_init=zero_init),
        grid=(B, nc),
        in_specs=in_specs,
        out_specs=[fwd(GLA_WIDTH), bwd(GLA_WIDTH), s_spec],
        out_shape=[jax.ShapeDtypeStruct((B, T, GLA_WIDTH), F32), jax.ShapeDtypeStruct((B, T, GLA_WIDTH), F32),
                   jax.ShapeDtypeStruct((B, N_DIR, GLA_KW, GLA_DV), F32)],
        compiler_params=_cparams("parallel", "arbitrary"),
        name="gla",
    )(*ins)


def _block_diag(blocks):
    n, r, c = blocks.shape
    eye = jnp.eye(n, dtype=blocks.dtype)
    return (eye[:, None, :, None] * blocks[:, :, None, :]).reshape(n * r, n * c)


def _pad_cols(a, width):
    return jnp.pad(a, ((0, 0), (0, width - a.shape[1])))


def _rope_tables(n_tokens):
    rows = n_tokens // GRID_W
    row = jnp.repeat(jnp.arange(rows), GRID_W).astype(F32)
    col = jnp.tile(jnp.arange(GRID_W), rows).astype(F32)
    inv = jnp.power(ROPE_THETA, -jnp.arange(ROPE_PAIRS_PER_AXIS, dtype=F32) / ROPE_PAIRS_PER_AXIS)
    ang = jnp.concatenate([row[:, None] * inv, col[:, None] * inv], axis=-1)
    cos, sin = jnp.cos(ang), jnp.sin(ang)
    reps = LANES // HEAD_DIM
    return jnp.tile(jnp.concatenate([cos, cos], axis=-1), (1, reps)), jnp.tile(jnp.concatenate([-sin, sin], axis=-1), (1, reps))


def _cache_variants(cache):
    z = jnp.zeros_like(cache[:, 0])
    return jnp.concatenate([cache[:, 0], z, z, cache[:, 0], cache[:, 1], z, z, cache[:, 1]], axis=-1).astype(BF16)


def _mlstm_tri():
    idx = np.arange(MLSTM_CHUNK)
    upper = idx[:, None] <= idx[None, :]
    tri_t = np.stack([upper, upper.T]).astype(np.float32)
    return jnp.asarray(np.concatenate([tri_t] * 3, axis=1), BF16)


def kernel(x_prompt, x_sample, cache_attn_k, cache_attn_v, state_mlstm_C, state_mlstm_n, state_mlstm_m, state_lru_h, state_gla_S, c, c_ctx, ada_w, ada_b, norm_mix_pre, norm_mix_post, norm_ffn_pre, norm_ffn_post, w_out, ffn_w1, ffn_w2, w_in_even, attn_q_norm, attn_k_norm, mlstm_i_bias, mlstm_f_bias, mlstm_norm, w_in_odd, lru_conv_w, lru_conv_b, lru_w_r, lru_b_r, lru_w_i, lru_b_i, lru_lambda, gla_w_lr, gla_b_lr, gla_norm):
    depth = ada_w.shape[0]
    Bp = x_prompt.shape[0]
    Bs = x_sample.shape[0]
    row = lambda a: a.reshape(1, -1)

    n_rows = -(-(1 + Bs) // SUBLANES) * SUBLANES
    cvec = jnp.zeros((n_rows, D_MODEL), F32).at[0].set(c_ctx).at[1:1 + Bs].set(c)
    mod_all = _ada_call(cvec, ada_w, ada_b).reshape(depth, n_rows, 6, 1, D_MODEL)

    rope_tabs = _rope_tables(x_sample.shape[1])
    ones64 = _block_diag(jnp.ones((ATT_HEADS, HEAD_DIM, HEAD_DIM), BF16))
    tri_t = _mlstm_tri()
    sel3, lvl_mask = _gla_constants()

    xp, xs = x_prompt, x_sample
    outs = {}
    for l in range(depth):
        mod = mod_all[l]
        tail = [row(norm_mix_post[l]), row(norm_ffn_pre[l]), row(norm_ffn_post[l]),
                w_out[l].astype(BF16), ffn_w1[l].astype(BF16), ffn_w2[l].astype(BF16)]
        if l % 2 == 0:
            e = l // 2
            w_in = w_in_even[e]
            o1 = ATT_WIDTH + 2 * ATT_KV_WIDTH
            o2 = o1 + 4 * MLSTM_WIDTH
            prm = {
                "g_mix_pre": row(norm_mix_pre[l]),
                "w_qkv": w_in[:, :o1].astype(BF16),
                "w_m": jnp.concatenate([w_in[:, o1 + MLSTM_WIDTH:o1 + 2 * MLSTM_WIDTH], w_in[:, o1 + 3 * MLSTM_WIDTH:o2]],
                                       axis=1).astype(BF16),
                "w_mt": jnp.concatenate([w_in[:, o1:o1 + MLSTM_WIDTH], w_in[:, o1 + 2 * MLSTM_WIDTH:o1 + 3 * MLSTM_WIDTH]],
                                        axis=1).T.astype(BF16),
                "w_gt": w_in[:, o2:].T.astype(BF16),
                "gate_bias_t": jnp.concatenate([mlstm_i_bias[e].reshape(-1), mlstm_f_bias[e].reshape(-1)]).reshape(-1, 1),
                "q_gain": row(jnp.tile(attn_q_norm[e], ATT_HEADS)),
                "k_gain": row(jnp.tile(attn_k_norm[e], ATT_KV_HEADS)),
                "ones64": ones64,
            }
            consts = [row(mlstm_norm[e])] + tail
            for path in ("prompt", "sample"):
                if path == "prompt":
                    x, r0, rs, tabs, cache, st0 = xp, 0, 0, None, None, None
                else:
                    x, r0, rs, tabs = xs, 1, 1, rope_tabs
                    cache = (_cache_variants(cache_attn_k[:, e]), _cache_variants(cache_attn_v[:, e]))
                    st0 = (state_mlstm_C[:, e], state_mlstm_n[:, e].reshape(Bs, N_DIR * MLSTM_HEADS, MLSTM_HD),
                           _pad_cols(state_mlstm_m[:, e].reshape(Bs, -1), LANES).reshape(Bs, 1, LANES))
                q, kk, vv, k_n, v_n, mq_t, mk, mv_t, mo, gates_t = _inproj_even_call(x, mod, r0, rs, prm, tabs)
                attn = _attn_call(q, kk, vv, cache)
                hf, hb, c_fin, n_fin, m_fin = _mlstm_call(mq_t, mk, mv_t, gates_t, tri_t, st0)
                x_new = _post_call(_post_even_kernel, "post_even", x, [attn, hf, hb, mo], mod, r0, rs, consts)
                if path == "prompt":
                    xp = x_new
                    T = x.shape[1]
                    outs.setdefault("k", []).append(jnp.swapaxes(k_n.reshape(Bp, T, ATT_KV_HEADS, HEAD_DIM), 1, 2))
                    outs.setdefault("v", []).append(jnp.swapaxes(v_n.reshape(Bp, T, ATT_KV_HEADS, HEAD_DIM), 1, 2))
                    outs.setdefault("C", []).append(c_fin)
                    outs.setdefault("n", []).append(n_fin.reshape(Bp, N_DIR, MLSTM_HEADS, MLSTM_HD))
                    outs.setdefault("m", []).append(m_fin[:, 0, :N_DIR * MLSTM_HEADS].reshape(Bp, N_DIR, MLSTM_HEADS))
                else:
                    xs = x_new
        else:
            o = l // 2
            w_in = w_in_odd[o]
            o1 = 2 * LRU_WIDTH + 2 * GLA_KW + 2 * GLA_WIDTH
            w_lr = _block_diag(gla_w_lr[o])
            w_lr = jnp.pad(w_lr, ((0, LANES - w_lr.shape[0]), (0, 0)))
            w_lr_hi = w_lr.astype(BF16)
            prm = {
                "g_mix_pre": row(norm_mix_pre[l]),
                "w_main": w_in[:, :o1].astype(BF16),
                "w_lr_in": _pad_cols(w_in[:, o1:], LANES).astype(BF16),
                "w_lr_hi": w_lr_hi,
                "w_lr_lo": (w_lr - w_lr_hi.astype(F32)).astype(BF16),
                "b_lr": row(gla_b_lr[o]),
            }
            lru_prm = {
                "conv_w": lru_conv_w[o],
                "conv_b": row(lru_conv_b[o]),
                "w_gate": (0.5 * jnp.concatenate([_block_diag(lru_w_r[o, 0]), _block_diag(lru_w_i[o, 0]),
                                                  _block_diag(lru_w_r[o, 1]), _block_diag(lru_w_i[o, 1])], axis=1)).astype(BF16),
                "b_gate": 0.5 * row(jnp.stack([lru_b_r[o, 0], lru_b_i[o, 0], lru_b_r[o, 1], lru_b_i[o, 1]])),
                "lam": lru_lambda[o],
            }
            consts = [row(gla_norm[o])] + tail
            for path in ("prompt", "sample"):
                if path == "prompt":
                    x, r0, rs, h0, s0 = xp, 0, 0, None, None
                else:
                    x, r0, rs, h0 = xs, 1, 1, state_lru_h[:, o]
                    s0 = state_gla_S[:, o].reshape(Bs, N_DIR, GLA_KW, GLA_DV)
                rx, rg, gq, gk, gv, gg, log_a = _inproj_odd_call(x, mod, r0, rs, prm)
                lru, h_last = _lru_call(rx, lru_prm, h0)
                of, ob, s_fin = _gla_call(gq, gk, gv, log_a, sel3, lvl_mask, s0)
                x_new = _post_call(_post_odd_kernel, "post_odd", x, [lru, rg, of, ob, gg], mod, r0, rs, consts)
                if path == "prompt":
                    xp = x_new
                    outs.setdefault("h", []).append(h_last)
                    outs.setdefault("S", []).append(s_fin.reshape(Bp, N_DIR, GLA_HEADS, GLA_DK, GLA_DV))
                else:
                    xs = x_new

    stack = lambda name: jnp.stack(outs[name], axis=1)
    return (xp, xs, stack("k"), stack("v"), stack("C"), stack("n"), stack("m"), stack("h"), stack("S"))
```

```python
import functools

import numpy as np
import jax
import jax.numpy as jnp
from jax import lax
from jax.experimental import pallas as pl
from jax.experimental.pallas import tpu as pltpu

F32 = jnp.float32
BF16 = jnp.bfloat16

D_MODEL = 1024
D_FF = 4 * D_MODEL
GRID_W = 64
EPS = 1e-6
LOG2_E = 1.4426950408889634
N_DIR = 2

ATT_HEADS = 8
ATT_KV_HEADS = 2
HEAD_DIM = 64
ATT_WIDTH = ATT_HEADS * HEAD_DIM
ATT_KV_WIDTH = ATT_KV_HEADS * HEAD_DIM
ROPE_THETA = 10000.0
ROPE_PAIRS_PER_AXIS = HEAD_DIM // 4

MLSTM_HEADS = 4
MLSTM_HD = 128
MLSTM_WIDTH = MLSTM_HEADS * MLSTM_HD
MLSTM_CHUNK = 256
MLSTM_STAGE_LAG = 2
M_INIT = -1e30

LRU_WIDTH = 512
LRU_BLOCKS = 8
LRU_BD = LRU_WIDTH // LRU_BLOCKS
LRU_C = 8.0
CONV_W = 4
CONV_LEFT = 2

GLA_HEADS = 4
GLA_DK = 64
GLA_DV = 128
GLA_KW = GLA_HEADS * GLA_DK
GLA_WIDTH = GLA_HEADS * GLA_DV
GLA_RANK = 16
GLA_TAU = 16.0
GLA_CHUNK = 64
GLA_STEP_CHUNKS = 4

LANES = 128
SUBLANES = 8
VMEM_LIMIT = 56 * 1024 * 1024

TOKEN_TILE = 512
ATTN_Q_TILE = 256
FF_CHUNK = 1024
POST_SUBTILES = 2
LRU_GATE_TILE = 512


def _cparams(*sem):
    return pltpu.CompilerParams(dimension_semantics=sem, vmem_limit_bytes=VMEM_LIMIT)


def _const_spec(shape):
    n = len(shape)
    return pl.BlockSpec(shape, lambda *_: (0,) * n, pipeline_mode=pl.Buffered(1))


def _dot(a, b):
    return jnp.dot(a, b, preferred_element_type=F32)


def _dot_nt(a, b):
    return lax.dot_general(a, b, (((1,), (1,)), ((), ())), preferred_element_type=F32)


def _dot_tn(a, b):
    return lax.dot_general(a, b, (((0,), (0,)), ((), ())), preferred_element_type=F32)


def _split3(x):
    hi = x.astype(BF16)
    r = x - hi.astype(F32)
    mid = r.astype(BF16)
    lo = (r - mid.astype(F32)).astype(BF16)
    return hi, mid, lo


def _split2(x):
    hi = x.astype(BF16)
    lo = (x - hi.astype(F32)).astype(BF16)
    return hi, lo


def _rms(x, g):
    return x * lax.rsqrt(jnp.mean(x * x, axis=-1, keepdims=True) + EPS) * g


def _log_sigmoid(x):
    return jnp.minimum(x, 0.0) - jnp.log1p(jnp.exp(-jnp.abs(x)))


def _softplus(x):
    return jnp.maximum(x, 0.0) + jnp.log1p(jnp.exp(-jnp.abs(x)))


def _group_rms(y, gain, ones_bd, width):
    hi, lo = _split2(y * y)
    ss = _dot(hi, ones_bd) + _dot(lo, ones_bd)
    return y * lax.rsqrt(ss * (1.0 / width) + EPS) * gain


def _ada_kernel(c_ref, w_ref, b_ref, o_ref):
    s = jax.nn.silu(c_ref[...])
    o_ref[...] = _dot(s.astype(BF16), w_ref[...].astype(BF16)) + b_ref[...]


def _ada_call(cvec, ada_w, ada_b):
    depth = ada_w.shape[0]
    rows = cvec.shape[0]
    n_col = ada_w.shape[2] // D_MODEL
    return pl.pallas_call(
        _ada_kernel,
        grid=(depth, n_col),
        in_specs=[
            pl.BlockSpec((rows, D_MODEL), lambda l, j: (0, 0)),
            pl.BlockSpec((None, D_MODEL, D_MODEL), lambda l, j: (l, 0, j)),
            pl.BlockSpec((None, 1, D_MODEL), lambda l, j: (l, 0, j)),
        ],
        out_specs=pl.BlockSpec((None, rows, D_MODEL), lambda l, j: (l, 0, j)),
        out_shape=jax.ShapeDtypeStruct((depth, rows, ada_w.shape[2]), F32),
        compiler_params=_cparams("arbitrary", "arbitrary"),
        name="ada_mod",
    )(cvec, ada_w, ada_b.reshape(depth, 1, -1))


def _mod_spec(row0, row_stride):
    return pl.BlockSpec((None, 6, 1, D_MODEL), lambda b, i: (row0 + row_stride * b, 0, 0, 0))


def _tok_spec(tm, width):
    return pl.BlockSpec((None, tm, width), lambda b, i: (b, i, 0))


def _rope(y, cos, sin_signed):
    width = y.shape[1]
    reps = width // LANES
    cosw = jnp.concatenate([cos] * reps, axis=1) if reps > 1 else cos
    sinw = jnp.concatenate([sin_signed] * reps, axis=1) if reps > 1 else sin_signed
    lane = lax.broadcasted_iota(jnp.int32, y.shape, 1)
    first_half = (lane % HEAD_DIM) < (HEAD_DIM // 2)
    partner = jnp.where(first_half, pltpu.roll(y, width - HEAD_DIM // 2, axis=1), pltpu.roll(y, HEAD_DIM // 2, axis=1))
    return y * cosw + partner * sinw


def _kv_variants(a):
    lane = lax.broadcasted_iota(jnp.int32, a.shape, 1)
    low = lane < HEAD_DIM
    swapped = pltpu.roll(a, HEAD_DIM, axis=1)
    zero = jnp.zeros_like(a)
    return jnp.concatenate([
        jnp.where(low, a, zero),
        jnp.where(low, zero, swapped),
        jnp.where(low, swapped, zero),
        jnp.where(low, zero, a),
    ], axis=1)


def _inproj_even_kernel(*refs, rope):
    if rope:
        (x_ref, mod_ref, g_ref, wqkv_ref, wm_ref, wmt_ref, wgt_ref, qg_ref, kg_ref, gbt_ref, ones_ref, cos_ref, sin_ref,
         q_out, kk_out, vv_out, k_out, v_out, mqt_out, mk_out, mvt_out, mo_out, gatet_out) = refs
    else:
        (x_ref, mod_ref, g_ref, wqkv_ref, wm_ref, wmt_ref, wgt_ref, qg_ref, kg_ref, gbt_ref, ones_ref,
         q_out, kk_out, vv_out, k_out, v_out, mqt_out, mk_out, mvt_out, mo_out, gatet_out) = refs
    h = _rms(x_ref[...], g_ref[...]) * (1.0 + mod_ref[1]) + mod_ref[0]
    hb = h.astype(BF16)

    qkv = _dot(hb, wqkv_ref[...])
    q = _group_rms(qkv[:, :ATT_WIDTH], qg_ref[...], ones_ref[...], HEAD_DIM)
    k = _group_rms(qkv[:, ATT_WIDTH:ATT_WIDTH + ATT_KV_WIDTH], kg_ref[...], ones_ref[:ATT_KV_WIDTH, :ATT_KV_WIDTH], HEAD_DIM)
    v = qkv[:, ATT_WIDTH + ATT_KV_WIDTH:]
    if rope:
        q = _rope(q, cos_ref[...], sin_ref[...])
        k = _rope(k, cos_ref[...], sin_ref[...])
    q_out[...] = (q * (HEAD_DIM ** -0.5 * LOG2_E)).astype(BF16)
    k_out[...] = k
    v_out[...] = v
    kk_out[...] = _kv_variants(k).astype(BF16)
    vv_out[...] = _kv_variants(v).astype(BF16)

    w = MLSTM_WIDTH
    mm = _dot(hb, wm_ref[...])
    mk_out[...] = mm[:, :w].astype(BF16)
    mo_out[...] = mm[:, w:]
    mmt = _dot_nt(wmt_ref[...], hb)
    mqt_out[...] = (mmt[:w] * MLSTM_HD ** -0.5).astype(BF16)
    mvt_out[...] = mmt[w:].astype(BF16)

    gates_t = _dot_nt(wgt_ref[...], hb) + gbt_ref[...]
    sub = lax.broadcasted_iota(jnp.int32, gates_t.shape, 0)
    gatet_out[...] = jnp.where(sub < N_DIR * MLSTM_HEADS, gates_t, _log_sigmoid(gates_t))


def _inproj_even_call(x, mod, mod_row0, mod_stride, prm, rope_tabs):
    B, T, _ = x.shape
    tm = min(TOKEN_TILE, T)
    rope = rope_tabs is not None
    ins = [x, mod, prm["g_mix_pre"], prm["w_qkv"], prm["w_m"], prm["w_mt"], prm["w_gt"], prm["q_gain"],
           prm["k_gain"], prm["gate_bias_t"], prm["ones64"]]
    in_specs = [_tok_spec(tm, D_MODEL), _mod_spec(mod_row0, mod_stride)] + [_const_spec(a.shape) for a in ins[2:]]
    if rope:
        ins += list(rope_tabs)
        in_specs += [pl.BlockSpec((tm, LANES), lambda b, i: (i, 0))] * 2
    n_gate = 2 * N_DIR * MLSTM_HEADS
    outs = [(ATT_WIDTH, BF16, True), (4 * LANES, BF16, True), (4 * LANES, BF16, True), (ATT_KV_WIDTH, F32, True),
            (ATT_KV_WIDTH, F32, True), (MLSTM_WIDTH, BF16, False), (MLSTM_WIDTH, BF16, True), (MLSTM_WIDTH, BF16, False),
            (MLSTM_WIDTH, F32, True), (n_gate, F32, False)]
    feat_spec = lambda w: pl.BlockSpec((None, w, tm), lambda b, i: (b, 0, i))
    return pl.pallas_call(
        functools.partial(_inproj_even_kernel, rope=rope),
        grid=(B, T // tm),
        in_specs=in_specs,
        out_specs=[_tok_spec(tm, w) if tok else feat_spec(w) for w, _, tok in outs],
        out_shape=[jax.ShapeDtypeStruct((B, T, w) if tok else (B, w, T), dt) for w, dt, tok in outs],
        compiler_params=_cparams("parallel", "parallel"),
        name="inproj_even",
    )(*ins)


def _attn_kernel(*refs, cached):
    if cached:
        q_ref, kk_ref, vv_ref, kc_ref, vc_ref, o_ref = refs
    else:
        q_ref, kk_ref, vv_ref, o_ref = refs
    def value_col(head):
        kv = head // (ATT_HEADS // ATT_KV_HEADS)
        return (2 * kv + head % 2) * LANES

    def scores(head):
        qp = q_ref[:, (head // 2) * LANES:(head // 2 + 1) * LANES]
        col = value_col(head)
        s = _dot_nt(qp, kk_ref[:, col:col + LANES])
        sc = _dot_nt(qp, kc_ref[:, col:col + LANES]) if cached else None
        return s, sc

    def weighted_values(head, s, sc):
        col = value_col(head)
        m = jnp.max(s, axis=-1, keepdims=True)
        if cached:
            m = jnp.maximum(m, jnp.max(sc, axis=-1, keepdims=True))
        p = jnp.exp2(s - m)
        den = jnp.sum(p, axis=-1, keepdims=True)
        o = _dot(p.astype(BF16), vv_ref[:, col:col + LANES])
        if cached:
            pc = jnp.exp2(sc - m)
            den = den + jnp.sum(pc, axis=-1, keepdims=True)
            o = o + _dot(pc.astype(BF16), vc_ref[:, col:col + LANES])
        return o / den

    pending = scores(0)
    acc = None
    for head in range(ATT_HEADS):
        current = pending
        if head + 1 < ATT_HEADS:
            pending = scores(head + 1)
        o = weighted_values(head, *current)
        if head % 2 == 0:
            acc = o
        else:
            pair = head // 2
            o_ref[:, pair * LANES:(pair + 1) * LANES] = (acc + o).astype(o_ref.dtype)


def _attn_call(q, kk, vv, cache=None):
    B, T, _ = q.shape
    tq = min(ATTN_Q_TILE, T)
    cached = cache is not None
    ins = [q, kk, vv]
    full = lambda n: pl.BlockSpec((None, n, 4 * LANES), lambda b, i: (b, 0, 0))
    in_specs = [_tok_spec(tq, ATT_WIDTH), full(T), full(T)]
    if cached:
        ins += list(cache)
        in_specs += [full(cache[0].shape[1])] * 2
    return pl.pallas_call(
        functools.partial(_attn_kernel, cached=cached),
        grid=(B, T // tq),
        in_specs=in_specs,
        out_specs=_tok_spec(tq, ATT_WIDTH),
        out_shape=jax.ShapeDtypeStruct((B, T, ATT_WIDTH), BF16),
        compiler_params=_cparams("parallel", "parallel"),
        name="attention",
    )(*ins)


def _mlstm_kernel(*refs, zero_init):
    if zero_init:
        (qf_ref, kf_ref, vf_ref, gtf_ref, qb_ref, kb_ref, vb_ref, gtb_ref, trit_ref,
         hf_ref, hb_ref, c_ref, n_ref, m_ref) = refs
    else:
        (qf_ref, kf_ref, vf_ref, gtf_ref, qb_ref, kb_ref, vb_ref, gtb_ref, trit_ref,
         c0_ref, n0_ref, m0_ref, hf_ref, hb_ref, c_ref, n_ref, m_ref) = refs
    L = MLSTM_CHUNK
    H = MLSTM_HEADS
    HD = MLSTM_HD
    PAD = 2 * SUBLANES
    chunk = pl.program_id(1)

    @pl.when(chunk == 0)
    def _init():
        if zero_init:
            c_ref[...] = jnp.zeros(c_ref.shape, F32)
            n_ref[...] = jnp.zeros(n_ref.shape, F32)
            m_ref[...] = jnp.full(m_ref.shape, M_INIT, F32)
        else:
            for d in range(N_DIR):
                for hd in range(H):
                    c_ref[d, hd] = c0_ref[d, hd].T
            n_ref[...] = n0_ref[...]
            m_ref[...] = m0_ref[...]

    m_all = m_ref[...]
    n_all = n_ref[...]
    m_row = m_all
    m_lane = lax.broadcasted_iota(jnp.int32, m_all.shape, 1)
    key = lax.broadcasted_iota(jnp.int32, (L, L), 0)
    qry = lax.broadcasted_iota(jnp.int32, (L, L), 1)
    dirs = ((qf_ref, kf_ref, vf_ref, gtf_ref, hf_ref, key <= qry, L - 1),
            (qb_ref, kb_ref, vb_ref, gtb_ref, hb_ref, key >= qry, 0))
    n_in = N_DIR * H
    sums = []
    for d, (_, _, _, gt_ref, _, _, _) in enumerate(dirs):
        gates_t = gt_ref[...]
        bcum_t = _dot(jnp.concatenate(_split3(gates_t), axis=1), trit_ref[d])
        sums.append((gates_t, bcum_t))
    c_rows = [sums[d][0][d * H:(d + 1) * H] - sums[d][1][n_in + d * H:n_in + (d + 1) * H] for d in range(N_DIR)]
    c_cols = jnp.concatenate(c_rows, axis=0).T

    def first_stage(d, hd):
        qt_ref, k_ref, vt_ref, _, _, mask, last = dirs[d]
        gates_t, bcum_t = sums[d]
        j = d * H + hd
        jf = n_in + j
        sl = slice(hd * HD, (hd + 1) * HD)
        k = k_ref[:, sl]
        q_t = qt_ref[sl, :]
        v_t = vt_ref[sl, :]
        b_row = bcum_t[jf:jf + 1, :]
        i_row = gates_t[j:j + 1, :]
        c_col = c_cols[:, j:j + 1]
        m_prev = m_all[0:1, j:j + 1]
        ct_prev = c_ref[d, hd]
        n_prev = n_all[j:j + 1, :]

        g = b_row + m_prev
        dlog = jnp.where(mask, c_col + b_row, -jnp.inf)
        m_t = jnp.maximum(g, jnp.max(dlog, axis=0, keepdims=True))
        w = jnp.exp(dlog - m_t)
        w_inter = jnp.exp(g - m_t)
        n_blk = jnp.broadcast_to(n_prev, (PAD, HD)).astype(BF16)
        r = _dot(jnp.concatenate([k, ct_prev.astype(BF16), n_blk], axis=0), q_t)

        b_last = bcum_t[jf:jf + 1, last:last + 1]
        wlog = b_last - b_row + i_row
        m_new = jnp.maximum(b_last + m_prev, jnp.max(wlog, axis=1, keepdims=True))
        ws = jnp.exp(wlog - m_new)
        decay = jnp.exp(b_last + m_prev - m_new)
        vw = (v_t.astype(F32) * ws).astype(BF16)
        ws_blk = jnp.broadcast_to(ws, (PAD, L)).astype(BF16)
        upd = _dot(jnp.concatenate([vw, ws_blk], axis=0), k)
        c_new = decay * ct_prev + upd[:HD]
        n_new = decay * n_prev + upd[HD:HD + 1]
        return dict(d=d, hd=hd, j=j, r=r, w=w, w_inter=w_inter, m_t=m_t, v_t=v_t, c_new=c_new, n_new=n_new, m_new=m_new)

    def second_stage(s):
        h_ref = dirs[s["d"]][4]
        sl = slice(s["hd"] * HD, (s["hd"] + 1) * HD)
        r = s["r"]
        qk = r[:L] * s["w"]
        num = _dot(s["v_t"], qk.astype(BF16)) + s["w_inter"] * r[L:L + HD]
        den = jnp.sum(qk, axis=0, keepdims=True) + s["w_inter"] * r[L + HD:L + HD + 1]
        h_ref[:, sl] = (num / jnp.maximum(jnp.abs(den), jnp.exp(-s["m_t"]))).T

    order = [(d, hd) for d in range(N_DIR) for hd in range(H)]
    staged = []
    for idx, (d, hd) in enumerate(order):
        staged.append(first_stage(d, hd))
        if idx >= MLSTM_STAGE_LAG:
            second_stage(staged[idx - MLSTM_STAGE_LAG])
    for s in staged[len(order) - MLSTM_STAGE_LAG:]:
        second_stage(s)
    for s in staged:
        c_ref[s["d"], s["hd"]] = s["c_new"]
        m_row = jnp.where(m_lane == s["j"], s["m_new"], m_row)
    n_ref[...] = jnp.concatenate([s["n_new"] for s in staged], axis=0)
    m_ref[...] = m_row

    @pl.when(chunk == pl.num_programs(1) - 1)
    def _finish():
        for d in range(N_DIR):
            for hd in range(H):
                c_ref[d, hd] = c_ref[d, hd].T


def _mlstm_call(mq_t, mk, mv_t, gates_t, tri_t, state0):
    B, T, _ = mk.shape
    L = MLSTM_CHUNK
    nc = T // L
    zero_init = state0 is None
    fwd = lambda w: pl.BlockSpec((None, L, w), lambda b, c: (b, c, 0))
    bwd = lambda w: pl.BlockSpec((None, L, w), lambda b, c: (b, nc - 1 - c, 0))
    fwd_t = lambda w: pl.BlockSpec((None, w, L), lambda b, c: (b, 0, c))
    bwd_t = lambda w: pl.BlockSpec((None, w, L), lambda b, c: (b, 0, nc - 1 - c))
    W = MLSTM_WIDTH
    NG = 2 * N_DIR * MLSTM_HEADS
    c_spec = pl.BlockSpec((None, N_DIR, MLSTM_HEADS, MLSTM_HD, MLSTM_HD), lambda b, c: (b, 0, 0, 0, 0))
    n_spec = pl.BlockSpec((None, N_DIR * MLSTM_HEADS, MLSTM_HD), lambda b, c: (b, 0, 0))
    m_spec = pl.BlockSpec((None, 1, LANES), lambda b, c: (b, 0, 0))
    ins = [mq_t, mk, mv_t, gates_t, mq_t, mk, mv_t, gates_t, tri_t]
    in_specs = [fwd_t(W), fwd(W), fwd_t(W), fwd_t(NG), bwd_t(W), bwd(W), bwd_t(W), bwd_t(NG), _const_spec(tri_t.shape)]
    if not zero_init:
        ins += list(state0)
        in_specs += [c_spec, n_spec, m_spec]
    return pl.pallas_call(
        functools.partial(_mlstm_kernel, zero_init=zero_init),
        grid=(B, nc),
        in_specs=in_specs,
        out_specs=[fwd(W), bwd(W), c_spec, n_spec, m_spec],
        out_shape=[jax.ShapeDtypeStruct((B, T, W), F32), jax.ShapeDtypeStruct((B, T, W), F32),
                   jax.ShapeDtypeStruct((B, N_DIR, MLSTM_HEADS, MLSTM_HD, MLSTM_HD), F32),
                   jax.ShapeDtypeStruct((B, N_DIR * MLSTM_HEADS, MLSTM_HD), F32),
                   jax.ShapeDtypeStruct((B, 1, LANES), F32)],
        compiler_params=_cparams("parallel", "arbitrary"),
        name="mlstm",
    )(*ins)


def _residual_ffn(x_ref, mix_fn, mod_ref, gpost_ref, gfpre_ref, gfpost_ref, wout_ref, w1_ref, w2_ref, o_ref):
    half = D_MODEL // 2
    sub = x_ref.shape[0] // POST_SUBTILES
    rows = [slice(i * sub, (i + 1) * sub) for i in range(POST_SUBTILES)]
    ys = []
    for r in rows:
        mix_a, mix_b = mix_fn(r)
        ys.append(_dot(mix_a.astype(BF16), wout_ref[:half, :]) + _dot(mix_b.astype(BF16), wout_ref[half:, :]))
    for r, y in zip(rows, ys):
        x1 = x_ref[r, :] + mod_ref[2] * _rms(y, gpost_ref[...])
        h2 = (_rms(x1, gfpre_ref[...]) * (1.0 + mod_ref[4]) + mod_ref[3]).astype(BF16)
        chunks = list(range(0, D_FF, FF_CHUNK))
        up = _dot(h2, w1_ref[:, chunks[0]:chunks[0] + FF_CHUNK])
        f = None
        for i, c0 in enumerate(chunks):
            cur = up
            if i + 1 < len(chunks):
                up = _dot(h2, w1_ref[:, chunks[i + 1]:chunks[i + 1] + FF_CHUNK])
            part = _dot(jnp.square(jnp.maximum(cur, 0.0)).astype(BF16), w2_ref[c0:c0 + FF_CHUNK, :])
            f = part if f is None else f + part
        o_ref[r, :] = x1 + mod_ref[5] * _rms(f, gfpost_ref[...])


def _head_rms(y, gain_ref, width):
    outs = []
    for h0 in range(0, y.shape[1], width):
        seg = y[:, h0:h0 + width]
        outs.append(seg * lax.rsqrt(jnp.mean(seg * seg, axis=-1, keepdims=True) + EPS) * gain_ref[:, h0:h0 + width])
    return jnp.concatenate(outs, axis=1)


def _post_even_kernel(x_ref, attn_ref, hf_ref, hb_ref, mo_ref, mod_ref, gm_ref, gpost_ref, gfpre_ref, gfpost_ref,
                      wout_ref, w1_ref, w2_ref, o_ref):
    def mix(r):
        hm = _head_rms(hf_ref[r, :] + hb_ref[r, :], gm_ref, MLSTM_HD) * jax.nn.sigmoid(mo_ref[r, :])
        return attn_ref[r, :], hm

    _residual_ffn(x_ref, mix, mod_ref, gpost_ref, gfpre_ref, gfpost_ref, wout_ref, w1_ref, w2_ref, o_ref)


def _post_odd_kernel(x_ref, lru_ref, rg_ref, of_ref, ob_ref, gg_ref, mod_ref, gm_ref, gpost_ref, gfpre_ref, gfpost_ref,
                     wout_ref, w1_ref, w2_ref, o_ref):
    def mix(r):
        lru = jax.nn.gelu(rg_ref[r, :]) * lru_ref[r, :]
        og = _head_rms(of_ref[r, :] + ob_ref[r, :], gm_ref, GLA_DV) * jax.nn.silu(gg_ref[r, :])
        return lru, og

    _residual_ffn(x_ref, mix, mod_ref, gpost_ref, gfpre_ref, gfpost_ref, wout_ref, w1_ref, w2_ref, o_ref)


def _post_call(kernel_fn, name, x, mix_ins, mod, mod_row0, mod_stride, consts):
    B, T, _ = x.shape
    tm = min(TOKEN_TILE, T)
    ins = [x] + list(mix_ins) + [mod] + list(consts)
    in_specs = ([_tok_spec(tm, D_MODEL)] + [_tok_spec(tm, a.shape[2]) for a in mix_ins] + [_mod_spec(mod_row0, mod_stride)]
                + [_const_spec(a.shape) for a in consts])
    return pl.pallas_call(
        kernel_fn,
        grid=(B, T // tm),
        in_specs=in_specs,
        out_specs=_tok_spec(tm, D_MODEL),
        out_shape=jax.ShapeDtypeStruct(x.shape, F32),
        compiler_params=_cparams("parallel", "parallel"),
        name=name,
    )(*ins)


def _inproj_odd_kernel(x_ref, mod_ref, g_ref, wmain_ref, wlr_in_ref, wlr_hi_ref, wlr_lo_ref, blr_ref,
                       rx_out, rg_out, gq_out, gk_out, gv_out, gg_out, la_out):
    h = _rms(x_ref[...], g_ref[...]) * (1.0 + mod_ref[1]) + mod_ref[0]
    hb = h.astype(BF16)
    glr_hi, glr_lo = _split2(_dot(hb, wlr_in_ref[...]))
    z = _dot(glr_hi, wlr_hi_ref[...]) + (_dot(glr_hi, wlr_lo_ref[...]) + _dot(glr_lo, wlr_hi_ref[...])) + blr_ref[...]
    la_out[...] = _log_sigmoid(z) * (1.0 / GLA_TAU)
    mm = _dot(hb, wmain_ref[...])
    o = 0
    rx_out[...] = mm[:, o:o + LRU_WIDTH]; o += LRU_WIDTH
    rg_out[...] = mm[:, o:o + LRU_WIDTH]; o += LRU_WIDTH
    gq_out[...] = mm[:, o:o + GLA_KW] * GLA_DK ** -0.5; o += GLA_KW
    gk_out[...] = mm[:, o:o + GLA_KW]; o += GLA_KW
    gv_out[...] = mm[:, o:o + GLA_WIDTH].astype(BF16); o += GLA_WIDTH
    gg_out[...] = mm[:, o:o + GLA_WIDTH]


def _inproj_odd_call(x, mod, mod_row0, mod_stride, prm):
    B, T, _ = x.shape
    tm = min(TOKEN_TILE, T)
    ins = [x, mod, prm["g_mix_pre"], prm["w_main"], prm["w_lr_in"], prm["w_lr_hi"], prm["w_lr_lo"], prm["b_lr"]]
    in_specs = [_tok_spec(tm, D_MODEL), _mod_spec(mod_row0, mod_stride)] + [_const_spec(a.shape) for a in ins[2:]]
    widths = [(LRU_WIDTH, F32), (LRU_WIDTH, F32), (GLA_KW, F32), (GLA_KW, F32), (GLA_WIDTH, BF16), (GLA_WIDTH, F32),
              (N_DIR * GLA_KW, F32)]
    return pl.pallas_call(
        _inproj_odd_kernel,
        grid=(B, T // tm),
        in_specs=in_specs,
        out_specs=[_tok_spec(tm, w) for w, _ in widths],
        out_shape=[jax.ShapeDtypeStruct((B, T, w), dt) for w, dt in widths],
        compiler_params=_cparams("parallel", "parallel"),
        name="inproj_odd",
    )(*ins)


def _lru_kernel(*refs, zero_init):
    if zero_init:
        x_ref, cw_ref, cb_ref, wg_ref, bg_ref, lam_ref, o_ref, last_ref, af_s, uf_s, ab_s, ub_s, hb_s = refs
    else:
        x_ref, cw_ref, cb_ref, wg_ref, bg_ref, lam_ref, h0_ref, o_ref, last_ref, af_s, uf_s, ab_s, ub_s, hb_s = refs
    T, W = x_ref.shape
    tg = min(LRU_GATE_TILE, T)
    neg_half_c_sp = (-0.5 * LRU_C) * _softplus(-lam_ref[...])

    for t0 in range(0, T, tg):
        xs = x_ref[t0:t0 + tg, :]
        rid = lax.broadcasted_iota(jnp.int32, (tg, W), 0)
        xc = cb_ref[...] + xs * cw_ref[CONV_LEFT:CONV_LEFT + 1, :]
        for tap in range(CONV_W):
            off = tap - CONV_LEFT
            if off == 0:
                continue
            if 0 <= t0 + off and t0 + off + tg <= T:
                win = x_ref[t0 + off:t0 + off + tg, :]
            elif off < 0:
                win = jnp.where(rid >= -off, pltpu.roll(xs, -off, axis=0), 0.0)
            else:
                win = jnp.where(rid < tg - off, pltpu.roll(xs, tg - off, axis=0), 0.0)
            xc = xc + win * cw_ref[tap:tap + 1, :]
        th = jnp.tanh(_dot(xc.astype(BF16), wg_ref[...]) + bg_ref[...])
        half_xc = 0.5 * xc
        for d, (a_s, u_s) in enumerate(((af_s, uf_s), (ab_s, ub_s))):
            th_r = th[:, 2 * d * W:(2 * d + 1) * W]
            th_i = th[:, (2 * d + 1) * W:(2 * d + 2) * W]
            log_a = neg_half_c_sp[d:d + 1, :] * th_r + neg_half_c_sp[d:d + 1, :]
            a = jnp.exp(log_a)
            a_s[t0:t0 + tg, :] = a
            v = jnp.tanh(log_a) * (-1.0 - a * a)
            u_s[t0:t0 + tg, :] = jnp.where(v > 0.0, v * lax.rsqrt(v), 0.0) * (half_xc * th_i + half_xc)

    nblk = T // SUBLANES
    rid8 = lax.broadcasted_iota(jnp.int32, (SUBLANES, W), 0)

    def block_scan(a, u, reverse):
        for dist in (1, 2, 4):
            if reverse:
                keep = rid8 < SUBLANES - dist
                shift = SUBLANES - dist
            else:
                keep = rid8 >= dist
                shift = dist
            a_n = jnp.where(keep, pltpu.roll(a, shift, axis=0), 1.0)
            u_n = jnp.where(keep, pltpu.roll(u, shift, axis=0), 0.0)
            u = a * u_n + u
            a = a * a_n
        return a, u

    def body(blk, carry):
        hf, hb = carry
        rf = pl.multiple_of(blk * SUBLANES, SUBLANES)
        a, u = block_scan(af_s[pl.ds(rf, SUBLANES), :], uf_s[pl.ds(rf, SUBLANES), :], False)
        out_f = u + a * hf
        o_ref[pl.ds(rf, SUBLANES), :] = out_f
        rb = pl.multiple_of((nblk - 1 - blk) * SUBLANES, SUBLANES)
        a, u = block_scan(ab_s[pl.ds(rb, SUBLANES), :], ub_s[pl.ds(rb, SUBLANES), :], True)
        out_b = u + a * hb
        hb_s[pl.ds(rb, SUBLANES), :] = out_b
        return out_f[SUBLANES - 1:SUBLANES, :], out_b[0:1, :]

    if zero_init:
        init = (jnp.zeros((1, W), F32), jnp.zeros((1, W), F32))
    else:
        init = (h0_ref[0:1, :], h0_ref[1:2, :])
    hf, hb = lax.fori_loop(0, nblk, body, init)
    last_ref[0:1, :] = hf
    last_ref[1:2, :] = hb
    o_ref[...] = o_ref[...] + hb_s[...]


def _lru_call(rx, prm, h0):
    B, T, W = rx.shape
    zero_init = h0 is None
    seq = pl.BlockSpec((None, T, W), lambda b: (b, 0, 0))
    st = pl.BlockSpec((None, N_DIR, W), lambda b: (b, 0, 0))
    ins = [rx, prm["conv_w"], prm["conv_b"], prm["w_gate"], prm["b_gate"], prm["lam"]]
    in_specs = [seq] + [_const_spec(a.shape) for a in ins[1:]]
    if not zero_init:
        ins.append(h0)
        in_specs.append(st)
    return pl.pallas_call(
        functools.partial(_lru_kernel, zero_init=zero_init),
        grid=(B,),
        in_specs=in_specs,
        out_specs=[seq, st],
        out_shape=[jax.ShapeDtypeStruct((B, T, W), F32), jax.ShapeDtypeStruct((B, N_DIR, W), F32)],
        scratch_shapes=[pltpu.VMEM((T, W), F32)] * 5,
        compiler_params=_cparams("parallel"),
        name="rglru",
    )(*ins)


GLA_LEVELS = 6


def _gla_constants():
    L = GLA_CHUNK
    idx = np.arange(L)
    sel = np.zeros((N_DIR, (GLA_LEVELS + 3) * L, L), np.float32)
    lvl_mask = np.zeros((N_DIR, GLA_LEVELS + 1, L, L), np.float32)
    for d in range(N_DIR):
        tri = (idx[None, :] <= idx[:, None]) if d == 0 else (idx[None, :] >= idx[:, None])
        tri = tri.astype(np.float32)
        last = L - 1 if d == 0 else 0
        sel[d, 0:L] = tri
        for lv in range(GLA_LEVELS):
            m = L >> (lv + 1)
            start = (idx // (2 * m)) * (2 * m)
            second = (idx - start) >= m
            boundary = start + (m - 1 if d == 0 else m)
            query_role = second if d == 0 else ~second
            diff = tri - tri[boundary]
            sel[d, (1 + lv) * L:(2 + lv) * L] = np.where(query_role[:, None], diff, -diff)
            same = start[:, None] == start[None, :]
            lvl_mask[d, lv] = same & query_role[:, None] & ~query_role[None, :]
        lvl_mask[d, GLA_LEVELS] = np.eye(L)
        sel[d, (GLA_LEVELS + 1) * L:(GLA_LEVELS + 2) * L] = tri[last][None, :] - tri
        sel[d, (GLA_LEVELS + 2) * L:] = tri[last][None, :]
    assert sel.min() >= 0.0 and sel.max() <= 1.0
    sel3 = np.concatenate([sel] * 3, axis=2)
    lvl_mask = np.tile(lvl_mask, (1, 1, 1, GLA_HEADS))
    return jnp.asarray(sel3, BF16), jnp.asarray(lvl_mask, F32)


def _gla_kernel(*refs, zero_init):
    if zero_init:
        (qf_ref, kf_ref, vf_ref, af_ref, qb_ref, kb_ref, vb_ref, ab_ref, sel_ref, msk_ref, of_ref, ob_ref, s_ref) = refs
    else:
        (qf_ref, kf_ref, vf_ref, af_ref, qb_ref, kb_ref, vb_ref, ab_ref, sel_ref, msk_ref, s0_ref,
         of_ref, ob_ref, s_ref) = refs
    L = GLA_CHUNK
    H = GLA_HEADS
    KW = GLA_KW
    VW = GLA_WIDTH
    DK = GLA_DK
    DV = GLA_DV

    @pl.when(pl.program_id(1) == 0)
    def _init():
        if zero_init:
            s_ref[...] = jnp.zeros(s_ref.shape, F32)
        else:
            s_ref[...] = s0_ref[...]

    zero_k = jnp.zeros((L, KW), BF16)
    zero_v = jnp.zeros((L, VW), BF16)
    lane_k = lax.broadcasted_iota(jnp.int32, (L, KW), 1) // DK
    lane_v = lax.broadcasted_iota(jnp.int32, (L, VW), 1) // DV

    def stack_heads(a, lane_head, zero):
        return jnp.concatenate([jnp.where(lane_head == h, a, zero) for h in range(H)], axis=0)

    def tile_rows(a):
        return jnp.concatenate([a] * H, axis=0)

    dirs = ((qf_ref, kf_ref, vf_ref, af_ref, of_ref), (qb_ref, kb_ref, vb_ref, ab_ref, ob_ref))
    work = []
    for i in range(GLA_STEP_CHUNKS):
        for d, (q_ref, k_ref, v_ref, a_ref, o_ref) in enumerate(dirs):
            j = i if d == 0 else GLA_STEP_CHUNKS - 1 - i
            rows = slice(j * L, (j + 1) * L)
            sums = _dot(sel_ref[d], jnp.concatenate(_split3(a_ref[rows, :]), axis=0))
            q_b = q_ref[rows, :].astype(BF16)
            work.append(dict(d=d, rows=rows, sums=sums, q_b=q_b, q_st=stack_heads(q_b, lane_k, zero_k),
                             k_st=stack_heads(k_ref[rows, :].astype(BF16), lane_k, zero_k)))

    for lv in range(GLA_LEVELS + 1):
        for wk in work:
            if lv == GLA_LEVELS:
                part = _dot_nt(wk["q_b"], wk["k_st"])
            else:
                wl = jnp.exp(wk["sums"][(1 + lv) * L:(2 + lv) * L]).astype(BF16)
                part = _dot_nt(wk["q_b"] * wl, wk["k_st"] * tile_rows(wl))
            part = part * msk_ref[wk["d"], lv]
            wk["scores"] = part if lv == 0 else wk["scores"] + part

    state = [s_ref[d] for d in range(N_DIR)]
    for wk in work:
        d, rows, sums = wk["d"], wk["rows"], wk["sums"]
        _, k_ref, v_ref, _, o_ref = dirs[d]
        v = v_ref[rows, :]
        s_prev = state[d]
        intra = _dot(wk["scores"].astype(BF16), stack_heads(v, lane_v, zero_v))
        q_in = wk["q_st"] * tile_rows(jnp.exp(sums[0:L]).astype(BF16))
        inter = _dot(q_in, s_prev.astype(BF16))
        o_ref[rows, :] = intra + jnp.concatenate([inter[h * L:(h + 1) * L] for h in range(H)], axis=1)

        k_out = k_ref[rows, :] * jnp.exp(sums[(GLA_LEVELS + 1) * L:(GLA_LEVELS + 2) * L])
        dec = jnp.exp(sums[(GLA_LEVELS + 2) * L:])
        both_t = jnp.concatenate([k_out, dec], axis=0).T
        k_out_t = both_t[:, :L].astype(BF16)
        upd = [_dot(k_out_t[h * DK:(h + 1) * DK], v[:, h * DV:(h + 1) * DV]) for h in range(H)]
        state[d] = both_t[:, L:L + 1] * s_prev + jnp.concatenate(upd, axis=0)
    for d in range(N_DIR):
        s_ref[d] = state[d]


def _gla_call(gq, gk, gv, log_a, sel3, lvl_mask, s0):
    B, T, _ = gq.shape
    L = GLA_CHUNK * GLA_STEP_CHUNKS
    nc = T // L
    zero_init = s0 is None
    fwd = lambda w: pl.BlockSpec((None, L, w), lambda b, c: (b, c, 0))
    bwd = lambda w: pl.BlockSpec((None, L, w), lambda b, c: (b, nc - 1 - c, 0))
    a_fwd = pl.BlockSpec((None, L, GLA_KW), lambda b, c: (b, c, 0))
    a_bwd = pl.BlockSpec((None, L, GLA_KW), lambda b, c: (b, nc - 1 - c, 1))
    s_spec = pl.BlockSpec((None, N_DIR, GLA_KW, GLA_DV), lambda b, c: (b, 0, 0, 0))
    ins = [gq, gk, gv, log_a, gq, gk, gv, log_a, sel3, lvl_mask]
    in_specs = [fwd(GLA_KW), fwd(GLA_KW), fwd(GLA_WIDTH), a_fwd, bwd(GLA_KW), bwd(GLA_KW), bwd(GLA_WIDTH), a_bwd,
                _const_spec(sel3.shape), _const_spec(lvl_mask.shape)]
    if not zero_init:
        ins.append(s0)
        in_specs.append(s_spec)
    return pl.pallas_call(
        functools.partial(_gla_kernel, zero_init=zero_init),
        grid=(B, nc),
        in_specs=in_specs,
        out_specs=[fwd(GLA_WIDTH), bwd(GLA_WIDTH), s_spec],
        out_shape=[jax.ShapeDtypeStruct((B, T, GLA_WIDTH), F32), jax.ShapeDtypeStruct((B, T, GLA_WIDTH), F32),
                   jax.ShapeDtypeStruct((B, N_DIR, GLA_KW, GLA_DV), F32)],
        compiler_params=_cparams("parallel", "arbitrary"),
        name="gla",
    )(*ins)


def _block_diag(blocks):
    n, r, c = blocks.shape
    eye = jnp.eye(n, dtype=blocks.dtype)
    return (eye[:, None, :, None] * blocks[:, :, None, :]).reshape(n * r, n * c)


def _pad_cols(a, width):
    return jnp.pad(a, ((0, 0), (0, width - a.shape[1])))


def _rope_tables(n_tokens):
    rows = n_tokens // GRID_W
    row = jnp.repeat(jnp.arange(rows), GRID_W).astype(F32)
    col = jnp.tile(jnp.arange(GRID_W), rows).astype(F32)
    inv = jnp.power(ROPE_THETA, -jnp.arange(ROPE_PAIRS_PER_AXIS, dtype=F32) / ROPE_PAIRS_PER_AXIS)
    ang = jnp.concatenate([row[:, None] * inv, col[:, None] * inv], axis=-1)
    cos, sin = jnp.cos(ang), jnp.sin(ang)
    reps = LANES // HEAD_DIM
    return jnp.tile(jnp.concatenate([cos, cos], axis=-1), (1, reps)), jnp.tile(jnp.concatenate([-sin, sin], axis=-1), (1, reps))


def _cache_variants(cache):
    z = jnp.zeros_like(cache[:, 0])
    return jnp.concatenate([cache[:, 0], z, z, cache[:, 0], cache[:, 1], z, z, cache[:, 1]], axis=-1).astype(BF16)


def _mlstm_tri():
    idx = np.arange(MLSTM_CHUNK)
    upper = idx[:, None] <= idx[None, :]
    tri_t = np.stack([upper, upper.T]).astype(np.float32)
    return jnp.asarray(np.concatenate([tri_t] * 3, axis=1), BF16)


def kernel(x_prompt, x_sample, cache_attn_k, cache_attn_v, state_mlstm_C, state_mlstm_n, state_mlstm_m, state_lru_h, state_gla_S, c, c_ctx, ada_w, ada_b, norm_mix_pre, norm_mix_post, norm_ffn_pre, norm_ffn_post, w_out, ffn_w1, ffn_w2, w_in_even, attn_q_norm, attn_k_norm, mlstm_i_bias, mlstm_f_bias, mlstm_norm, w_in_odd, lru_conv_w, lru_conv_b, lru_w_r, lru_b_r, lru_w_i, lru_b_i, lru_lambda, gla_w_lr, gla_b_lr, gla_norm):
    depth = ada_w.shape[0]
    Bp = x_prompt.shape[0]
    Bs = x_sample.shape[0]
    row = lambda a: a.reshape(1, -1)

    n_rows = -(-(1 + Bs) // SUBLANES) * SUBLANES
    cvec = jnp.zeros((n_rows, D_MODEL), F32).at[0].set(c_ctx).at[1:1 + Bs].set(c)
    mod_all = _ada_call(cvec, ada_w, ada_b).reshape(depth, n_rows, 6, 1, D_MODEL)

    rope_tabs = _rope_tables(x_sample.shape[1])
    ones64 = _block_diag(jnp.ones((ATT_HEADS, HEAD_DIM, HEAD_DIM), BF16))
    tri_t = _mlstm_tri()
    sel3, lvl_mask = _gla_constants()

    xp, xs = x_prompt, x_sample
    outs = {}
    for l in range(depth):
        mod = mod_all[l]
        tail = [row(norm_mix_post[l]), row(norm_ffn_pre[l]), row(norm_ffn_post[l]),
                w_out[l].astype(BF16), ffn_w1[l].astype(BF16), ffn_w2[l].astype(BF16)]
        if l % 2 == 0:
            e = l // 2
            w_in = w_in_even[e]
            o1 = ATT_WIDTH + 2 * ATT_KV_WIDTH
            o2 = o1 + 4 * MLSTM_WIDTH
            prm = {
                "g_mix_pre": row(norm_mix_pre[l]),
                "w_qkv": w_in[:, :o1].astype(BF16),
                "w_m": jnp.concatenate([w_in[:, o1 + MLSTM_WIDTH:o1 + 2 * MLSTM_WIDTH], w_in[:, o1 + 3 * MLSTM_WIDTH:o2]],
                                       axis=1).astype(BF16),
                "w_mt": jnp.concatenate([w_in[:, o1:o1 + MLSTM_WIDTH], w_in[:, o1 + 2 * MLSTM_WIDTH:o1 + 3 * MLSTM_WIDTH]],
                                        axis=1).T.astype(BF16),
                "w_gt": w_in[:, o2:].T.astype(BF16),
                "gate_bias_t": jnp.concatenate([mlstm_i_bias[e].reshape(-1), mlstm_f_bias[e].reshape(-1)]).reshape(-1, 1),
                "q_gain": row(jnp.tile(attn_q_norm[e], ATT_HEADS)),
                "k_gain": row(jnp.tile(attn_k_norm[e], ATT_KV_HEADS)),
                "ones64": ones64,
            }
            consts = [row(mlstm_norm[e])] + tail
            for path in ("prompt", "sample"):
                if path == "prompt":
                    x, r0, rs, tabs, cache, st0 = xp, 0, 0, None, None, None
                else:
                    x, r0, rs, tabs = xs, 1, 1, rope_tabs
                    cache = (_cache_variants(cache_attn_k[:, e]), _cache_variants(cache_attn_v[:, e]))
                    st0 = (state_mlstm_C[:, e], state_mlstm_n[:, e].reshape(Bs, N_DIR * MLSTM_HEADS, MLSTM_HD),
                           _pad_cols(state_mlstm_m[:, e].reshape(Bs, -1), LANES).reshape(Bs, 1, LANES))
                q, kk, vv, k_n, v_n, mq_t, mk, mv_t, mo, gates_t = _inproj_even_call(x, mod, r0, rs, prm, tabs)
                attn = _attn_call(q, kk, vv, cache)
                hf, hb, c_fin, n_fin, m_fin = _mlstm_call(mq_t, mk, mv_t, gates_t, tri_t, st0)
                x_new = _post_call(_post_even_kernel, "post_even", x, [attn, hf, hb, mo], mod, r0, rs, consts)
                if path == "prompt":
                    xp = x_new
                    T = x.shape[1]
                    outs.setdefault("k", []).append(jnp.swapaxes(k_n.reshape(Bp, T, ATT_KV_HEADS, HEAD_DIM), 1, 2))
                    outs.setdefault("v", []).append(jnp.swapaxes(v_n.reshape(Bp, T, ATT_KV_HEADS, HEAD_DIM), 1, 2))
                    outs.setdefault("C", []).append(c_fin)
                    outs.setdefault("n", []).append(n_fin.reshape(Bp, N_DIR, MLSTM_HEADS, MLSTM_HD))
                    outs.setdefault("m", []).append(m_fin[:, 0, :N_DIR * MLSTM_HEADS].reshape(Bp, N_DIR, MLSTM_HEADS))
                else:
                    xs = x_new
        else:
            o = l // 2
            w_in = w_in_odd[o]
            o1 = 2 * LRU_WIDTH + 2 * GLA_KW + 2 * GLA_WIDTH
            w_lr = _block_diag(gla_w_lr[o])
            w_lr = jnp.pad(w_lr, ((0, LANES - w_lr.shape[0]), (0, 0)))
            w_lr_hi = w_lr.astype(BF16)
            prm = {
                "g_mix_pre": row(norm_mix_pre[l]),
                "w_main": w_in[:, :o1].astype(BF16),
                "w_lr_in": _pad_cols(w_in[:, o1:], LANES).astype(BF16),
                "w_lr_hi": w_lr_hi,
                "w_lr_lo": (w_lr - w_lr_hi.astype(F32)).astype(BF16),
                "b_lr": row(gla_b_lr[o]),
            }
            lru_prm = {
                "conv_w": lru_conv_w[o],
                "conv_b": row(lru_conv_b[o]),
                "w_gate": (0.5 * jnp.concatenate([_block_diag(lru_w_r[o, 0]), _block_diag(lru_w_i[o, 0]),
                                                  _block_diag(lru_w_r[o, 1]), _block_diag(lru_w_i[o, 1])], axis=1)).astype(BF16),
                "b_gate": 0.5 * row(jnp.stack([lru_b_r[o, 0], lru_b_i[o, 0], lru_b_r[o, 1], lru_b_i[o, 1]])),
                "lam": lru_lambda[o],
            }
            consts = [row(gla_norm[o])] + tail
            for path in ("prompt", "sample"):
                if path == "prompt":
                    x, r0, rs, h0, s0 = xp, 0, 0, None, None
                else:
                    x, r0, rs, h0 = xs, 1, 1, state_lru_h[:, o]
                    s0 = state_gla_S[:, o].reshape(Bs, N_DIR, GLA_KW, GLA_DV)
                rx, rg, gq, gk, gv, gg, log_a = _inproj_odd_call(x, mod, r0, rs, prm)
                lru, h_last = _lru_call(rx, lru_prm, h0)
                of, ob, s_fin = _gla_call(gq, gk, gv, log_a, sel3, lvl_mask, s0)
                x_new = _post_call(_post_odd_kernel, "post_odd", x, [lru, rg, of, ob, gg], mod, r0, rs, consts)
                if path == "prompt":
                    xp = x_new
                    outs.setdefault("h", []).append(h_last)
                    outs.setdefault("S", []).append(s_fin.reshape(Bp, N_DIR, GLA_HEADS, GLA_DK, GLA_DV))
                else:
                    xs = x_new

    stack = lambda name: jnp.stack(outs[name], axis=1)
    return (xp, xs, stack("k"), stack("v"), stack("C"), stack("n"), stack("m"), stack("h"), stack("S"))
```

```python
import functools

import numpy as np
import jax
import jax.numpy as jnp
from jax import lax
from jax.experimental import pallas as pl
from jax.experimental.pallas import tpu as pltpu

F32 = jnp.float32
BF16 = jnp.bfloat16

D_MODEL = 1024
D_FF = 4 * D_MODEL
GRID_W = 64
EPS = 1e-6
LOG2_E = 1.4426950408889634
N_DIR = 2

ATT_HEADS = 8
ATT_KV_HEADS = 2
HEAD_DIM = 64
ATT_WIDTH = ATT_HEADS * HEAD_DIM
ATT_KV_WIDTH = ATT_KV_HEADS * HEAD_DIM
ROPE_THETA = 10000.0
ROPE_PAIRS_PER_AXIS = HEAD_DIM // 4

MLSTM_HEADS = 4
MLSTM_HD = 128
MLSTM_WIDTH = MLSTM_HEADS * MLSTM_HD
MLSTM_CHUNK = 256
MLSTM_STAGE_LAG = 2
M_INIT = -1e30

LRU_WIDTH = 512
LRU_BLOCKS = 8
LRU_BD = LRU_WIDTH // LRU_BLOCKS
LRU_C = 8.0
CONV_W = 4
CONV_LEFT = 2

GLA_HEADS = 4
GLA_DK = 64
GLA_DV = 128
GLA_KW = GLA_HEADS * GLA_DK
GLA_WIDTH = GLA_HEADS * GLA_DV
GLA_RANK = 16
GLA_TAU = 16.0
GLA_CHUNK = 64
GLA_STEP_CHUNKS = 4

LANES = 128
SUBLANES = 8
VMEM_LIMIT = 56 * 1024 * 1024

TOKEN_TILE = 512
ATTN_Q_TILE = 256
FF_CHUNK = 1024
POST_SUBTILES = 2
ODD_ROW_GROUPS = 4
CONV_HALO = 2 * SUBLANES


def _cparams(*sem):
    return pltpu.CompilerParams(dimension_semantics=sem, vmem_limit_bytes=VMEM_LIMIT)


def _const_spec(shape):
    n = len(shape)
    return pl.BlockSpec(shape, lambda *_: (0,) * n, pipeline_mode=pl.Buffered(1))


def _dot(a, b):
    return jnp.dot(a, b, preferred_element_type=F32)


def _dot_nt(a, b):
    return lax.dot_general(a, b, (((1,), (1,)), ((), ())), preferred_element_type=F32)


def _dot_tn(a, b):
    return lax.dot_general(a, b, (((0,), (0,)), ((), ())), preferred_element_type=F32)


def _split3(x):
    hi = x.astype(BF16)
    r = x - hi.astype(F32)
    mid = r.astype(BF16)
    lo = (r - mid.astype(F32)).astype(BF16)
    return hi, mid, lo


def _split2(x):
    hi = x.astype(BF16)
    lo = (x - hi.astype(F32)).astype(BF16)
    return hi, lo


def _rms(x, g):
    return x * lax.rsqrt(jnp.mean(x * x, axis=-1, keepdims=True) + EPS) * g


def _log_sigmoid(x):
    return jnp.minimum(x, 0.0) - jnp.log1p(jnp.exp(-jnp.abs(x)))


def _softplus(x):
    return jnp.maximum(x, 0.0) + jnp.log1p(jnp.exp(-jnp.abs(x)))


def _group_rms(y, gain, ones_bd, width):
    hi, lo = _split2(y * y)
    ss = _dot(hi, ones_bd) + _dot(lo, ones_bd)
    return y * lax.rsqrt(ss * (1.0 / width) + EPS) * gain


def _ada_kernel(c_ref, w_ref, b_ref, o_ref):
    s = jax.nn.silu(c_ref[...])
    o_ref[...] = _dot(s.astype(BF16), w_ref[...].astype(BF16)) + b_ref[...]


def _ada_call(cvec, ada_w, ada_b):
    depth = ada_w.shape[0]
    rows = cvec.shape[0]
    n_col = ada_w.shape[2] // D_MODEL
    return pl.pallas_call(
        _ada_kernel,
        grid=(depth, n_col),
        in_specs=[
            pl.BlockSpec((rows, D_MODEL), lambda l, j: (0, 0)),
            pl.BlockSpec((None, D_MODEL, D_MODEL), lambda l, j: (l, 0, j)),
            pl.BlockSpec((None, 1, D_MODEL), lambda l, j: (l, 0, j)),
        ],
        out_specs=pl.BlockSpec((None, rows, D_MODEL), lambda l, j: (l, 0, j)),
        out_shape=jax.ShapeDtypeStruct((depth, rows, ada_w.shape[2]), F32),
        compiler_params=_cparams("arbitrary", "arbitrary"),
        name="ada_mod",
    )(cvec, ada_w, ada_b.reshape(depth, 1, -1))


def _mod_spec(row0, row_stride):
    return pl.BlockSpec((None, 6, 1, D_MODEL), lambda b, i: (row0 + row_stride * b, 0, 0, 0))


def _tok_spec(tm, width):
    return pl.BlockSpec((None, tm, width), lambda b, i: (b, i, 0))


def _rope(y, cos, sin_signed):
    width = y.shape[1]
    reps = width // LANES
    cosw = jnp.concatenate([cos] * reps, axis=1) if reps > 1 else cos
    sinw = jnp.concatenate([sin_signed] * reps, axis=1) if reps > 1 else sin_signed
    lane = lax.broadcasted_iota(jnp.int32, y.shape, 1)
    first_half = (lane % HEAD_DIM) < (HEAD_DIM // 2)
    partner = jnp.where(first_half, pltpu.roll(y, width - HEAD_DIM // 2, axis=1), pltpu.roll(y, HEAD_DIM // 2, axis=1))
    return y * cosw + partner * sinw


def _kv_variants(a):
    lane = lax.broadcasted_iota(jnp.int32, a.shape, 1)
    low = lane < HEAD_DIM
    swapped = pltpu.roll(a, HEAD_DIM, axis=1)
    zero = jnp.zeros_like(a)
    return jnp.concatenate([
        jnp.where(low, a, zero),
        jnp.where(low, zero, swapped),
        jnp.where(low, swapped, zero),
        jnp.where(low, zero, a),
    ], axis=1)


def _inproj_even_kernel(*refs, rope):
    if rope:
        (x_ref, mod_ref, g_ref, wqkv_ref, wm_ref, wmt_ref, wgt_ref, qg_ref, kg_ref, gbt_ref, ones_ref, cos_ref, sin_ref,
         q_out, kk_out, vv_out, k_out, v_out, mqt_out, mk_out, mvt_out, mo_out, gatet_out) = refs
    else:
        (x_ref, mod_ref, g_ref, wqkv_ref, wm_ref, wmt_ref, wgt_ref, qg_ref, kg_ref, gbt_ref, ones_ref,
         q_out, kk_out, vv_out, k_out, v_out, mqt_out, mk_out, mvt_out, mo_out, gatet_out) = refs
    h = _rms(x_ref[...], g_ref[...]) * (1.0 + mod_ref[1]) + mod_ref[0]
    hb = h.astype(BF16)

    qkv = _dot(hb, wqkv_ref[...])
    w = MLSTM_WIDTH
    mm = _dot(hb, wm_ref[...])
    mk_out[...] = mm[:, :w].astype(BF16)
    mo_out[...] = mm[:, w:]
    q = _group_rms(qkv[:, :ATT_WIDTH], qg_ref[...], ones_ref[...], HEAD_DIM)
    k = _group_rms(qkv[:, ATT_WIDTH:ATT_WIDTH + ATT_KV_WIDTH], kg_ref[...], ones_ref[:ATT_KV_WIDTH, :ATT_KV_WIDTH], HEAD_DIM)
    v = qkv[:, ATT_WIDTH + ATT_KV_WIDTH:]
    if rope:
        q = _rope(q, cos_ref[...], sin_ref[...])
        k = _rope(k, cos_ref[...], sin_ref[...])
    q_out[...] = (q * (HEAD_DIM ** -0.5 * LOG2_E)).astype(BF16)
    k_out[...] = k
    v_out[...] = v
    kk_out[...] = _kv_variants(k).astype(BF16)
    vv_out[...] = _kv_variants(v).astype(BF16)

    mmt = _dot_nt(wmt_ref[...], hb)
    mqt_out[...] = (mmt[:w] * MLSTM_HD ** -0.5).astype(BF16)
    mvt_out[...] = mmt[w:].astype(BF16)

    gates_t = _dot_nt(wgt_ref[...], hb) + gbt_ref[...]
    sub = lax.broadcasted_iota(jnp.int32, gates_t.shape, 0)
    gatet_out[...] = jnp.where(sub < N_DIR * MLSTM_HEADS, gates_t, _log_sigmoid(gates_t))


def _inproj_even_call(x, mod, mod_row0, mod_stride, prm, rope_tabs):
    B, T, _ = x.shape
    tm = min(TOKEN_TILE, T)
    rope = rope_tabs is not None
    ins = [x, mod, prm["g_mix_pre"], prm["w_qkv"], prm["w_m"], prm["w_mt"], prm["w_gt"], prm["q_gain"],
           prm["k_gain"], prm["gate_bias_t"], prm["ones64"]]
    in_specs = [_tok_spec(tm, D_MODEL), _mod_spec(mod_row0, mod_stride)] + [_const_spec(a.shape) for a in ins[2:]]
    if rope:
        ins += list(rope_tabs)
        in_specs += [pl.BlockSpec((tm, LANES), lambda b, i: (i, 0))] * 2
    n_gate = 2 * N_DIR * MLSTM_HEADS
    outs = [(ATT_WIDTH, BF16, True), (4 * LANES, BF16, True), (4 * LANES, BF16, True), (ATT_KV_WIDTH, F32, True),
            (ATT_KV_WIDTH, F32, True), (MLSTM_WIDTH, BF16, False), (MLSTM_WIDTH, BF16, True), (MLSTM_WIDTH, BF16, False),
            (MLSTM_WIDTH, F32, True), (n_gate, F32, False)]
    feat_spec = lambda w: pl.BlockSpec((None, w, tm), lambda b, i: (b, 0, i))
    return pl.pallas_call(
        functools.partial(_inproj_even_kernel, rope=rope),
        grid=(B, T // tm),
        in_specs=in_specs,
        out_specs=[_tok_spec(tm, w) if tok else feat_spec(w) for w, _, tok in outs],
        out_shape=[jax.ShapeDtypeStruct((B, T, w) if tok else (B, w, T), dt) for w, dt, tok in outs],
        compiler_params=_cparams("parallel", "parallel"),
        name="inproj_even",
    )(*ins)


def _attn_kernel(*refs, cached):
    if cached:
        q_ref, kk_ref, vv_ref, kc_ref, vc_ref, o_ref = refs
    else:
        q_ref, kk_ref, vv_ref, o_ref = refs
    def value_col(head):
        kv = head // (ATT_HEADS // ATT_KV_HEADS)
        return (2 * kv + head % 2) * LANES

    def scores(head):
        qp = q_ref[:, (head // 2) * LANES:(head // 2 + 1) * LANES]
        col = value_col(head)
        s = _dot_nt(qp, kk_ref[:, col:col + LANES])
        sc = _dot_nt(qp, kc_ref[:, col:col + LANES]) if cached else None
        return s, sc

    def weighted_values(head, s, sc):
        col = value_col(head)
        m = jnp.max(s, axis=-1, keepdims=True)
        if cached:
            m = jnp.maximum(m, jnp.max(sc, axis=-1, keepdims=True))
        p = jnp.exp2(s - m)
        den = jnp.sum(p, axis=-1, keepdims=True)
        o = _dot(p.astype(BF16), vv_ref[:, col:col + LANES])
        if cached:
            pc = jnp.exp2(sc - m)
            den = den + jnp.sum(pc, axis=-1, keepdims=True)
            o = o + _dot(pc.astype(BF16), vc_ref[:, col:col + LANES])
        return o / den

    pending = scores(0)
    acc = None
    for head in range(ATT_HEADS):
        current = pending
        if head + 1 < ATT_HEADS:
            pending = scores(head + 1)
        o = weighted_values(head, *current)
        if head % 2 == 0:
            acc = o
        else:
            pair = head // 2
            o_ref[:, pair * LANES:(pair + 1) * LANES] = (acc + o).astype(o_ref.dtype)


def _attn_call(q, kk, vv, cache=None):
    B, T, _ = q.shape
    tq = min(ATTN_Q_TILE, T)
    cached = cache is not None
    ins = [q, kk, vv]
    full = lambda n: pl.BlockSpec((None, n, 4 * LANES), lambda b, i: (b, 0, 0))
    in_specs = [_tok_spec(tq, ATT_WIDTH), full(T), full(T)]
    if cached:
        ins += list(cache)
        in_specs += [full(cache[0].shape[1])] * 2
    return pl.pallas_call(
        functools.partial(_attn_kernel, cached=cached),
        grid=(B, T // tq),
        in_specs=in_specs,
        out_specs=_tok_spec(tq, ATT_WIDTH),
        out_shape=jax.ShapeDtypeStruct((B, T, ATT_WIDTH), BF16),
        compiler_params=_cparams("parallel", "parallel"),
        name="attention",
    )(*ins)


def _mlstm_kernel(*refs, zero_init):
    if zero_init:
        (qf_ref, kf_ref, vf_ref, gtf_ref, qb_ref, kb_ref, vb_ref, gtb_ref, trit_ref,
         hf_ref, hb_ref, c_ref, n_ref, m_ref) = refs
    else:
        (qf_ref, kf_ref, vf_ref, gtf_ref, qb_ref, kb_ref, vb_ref, gtb_ref, trit_ref,
         c0_ref, n0_ref, m0_ref, hf_ref, hb_ref, c_ref, n_ref, m_ref) = refs
    L = MLSTM_CHUNK
    H = MLSTM_HEADS
    HD = MLSTM_HD
    PAD = 2 * SUBLANES
    chunk = pl.program_id(1)

    @pl.when(chunk == 0)
    def _init():
        if zero_init:
            c_ref[...] = jnp.zeros(c_ref.shape, F32)
            n_ref[...] = jnp.zeros(n_ref.shape, F32)
            m_ref[...] = jnp.full(m_ref.shape, M_INIT, F32)
        else:
            for d in range(N_DIR):
                for hd in range(H):
                    c_ref[d, hd] = c0_ref[d, hd].T
            n_ref[...] = n0_ref[...]
            m_ref[...] = m0_ref[...]

    m_all = m_ref[...]
    n_all = n_ref[...]
    m_row = m_all
    m_lane = lax.broadcasted_iota(jnp.int32, m_all.shape, 1)
    key = lax.broadcasted_iota(jnp.int32, (L, L), 0)
    qry = lax.broadcasted_iota(jnp.int32, (L, L), 1)
    dirs = ((qf_ref, kf_ref, vf_ref, gtf_ref, hf_ref, key <= qry, L - 1),
            (qb_ref, kb_ref, vb_ref, gtb_ref, hb_ref, key >= qry, 0))
    n_in = N_DIR * H
    order = [(d, hd) for d in range(N_DIR) for hd in range(H)]

    def query_stage(d, hd):
        qt_ref, k_ref = dirs[d][0], dirs[d][1]
        j = d * H + hd
        sl = slice(hd * HD, (hd + 1) * HD)
        n_blk = jnp.broadcast_to(n_all[j:j + 1, :], (PAD, HD)).astype(BF16)
        return _dot(jnp.concatenate([k_ref[:, sl], c_ref[d, hd].astype(BF16), n_blk], axis=0), qt_ref[sl, :])

    sums = []
    for d, (_, _, _, gt_ref, _, _, _) in enumerate(dirs):
        gates_t = gt_ref[...]
        bcum_t = _dot(jnp.concatenate(_split3(gates_t), axis=1), trit_ref[d])
        sums.append((gates_t, bcum_t))
    c_rows = [sums[d][0][d * H:(d + 1) * H] - sums[d][1][n_in + d * H:n_in + (d + 1) * H] for d in range(N_DIR)]
    c_cols = jnp.concatenate(c_rows, axis=0).T

    def first_stage(d, hd, r):
        _, k_ref, vt_ref, _, _, mask, last = dirs[d]
        gates_t, bcum_t = sums[d]
        j = d * H + hd
        jf = n_in + j
        sl = slice(hd * HD, (hd + 1) * HD)
        k = k_ref[:, sl]
        v_t = vt_ref[sl, :]
        b_row = bcum_t[jf:jf + 1, :]
        i_row = gates_t[j:j + 1, :]
        c_col = c_cols[:, j:j + 1]
        m_prev = m_all[0:1, j:j + 1]
        ct_prev = c_ref[d, hd]
        n_prev = n_all[j:j + 1, :]

        g = b_row + m_prev
        dlog = jnp.where(mask, c_col + b_row, -jnp.inf)
        m_t = jnp.maximum(g, jnp.max(dlog, axis=0, keepdims=True))
        w = jnp.exp(dlog - m_t)
        w_inter = jnp.exp(g - m_t)

        b_last = bcum_t[jf:jf + 1, last:last + 1]
        wlog = b_last - b_row + i_row
        m_new = jnp.maximum(b_last + m_prev, jnp.max(wlog, axis=1, keepdims=True))
        ws = jnp.exp(wlog - m_new)
        decay = jnp.exp(b_last + m_prev - m_new)
        vw = (v_t.astype(F32) * ws).astype(BF16)
        ws_blk = jnp.broadcast_to(ws, (PAD, L)).astype(BF16)
        upd = _dot(jnp.concatenate([vw, ws_blk], axis=0), k)
        c_new = decay * ct_prev + upd[:HD]
        n_new = decay * n_prev + upd[HD:HD + 1]
        return dict(d=d, hd=hd, j=j, r=r, w=w, w_inter=w_inter, m_t=m_t, v_t=v_t, c_new=c_new, n_new=n_new, m_new=m_new)

    def second_stage(s):
        h_ref = dirs[s["d"]][4]
        sl = slice(s["hd"] * HD, (s["hd"] + 1) * HD)
        r = s["r"]
        qk = r[:L] * s["w"]
        num = _dot(s["v_t"], qk.astype(BF16)) + s["w_inter"] * r[L:L + HD]
        den = jnp.sum(qk, axis=0, keepdims=True) + s["w_inter"] * r[L + HD:L + HD + 1]
        h_ref[:, sl] = (num / jnp.maximum(jnp.abs(den), jnp.exp(-s["m_t"]))).T

    staged = []
    for idx, (d, hd) in enumerate(order):
        staged.append(first_stage(d, hd, query_stage(d, hd)))
        if idx >= MLSTM_STAGE_LAG:
            second_stage(staged[idx - MLSTM_STAGE_LAG])
    for s in staged[len(order) - MLSTM_STAGE_LAG:]:
        second_stage(s)
    for s in staged:
        c_ref[s["d"], s["hd"]] = s["c_new"]
        m_row = jnp.where(m_lane == s["j"], s["m_new"], m_row)
    n_ref[...] = jnp.concatenate([s["n_new"] for s in staged], axis=0)
    m_ref[...] = m_row

    @pl.when(chunk == pl.num_programs(1) - 1)
    def _finish():
        for d in range(N_DIR):
            for hd in range(H):
                c_ref[d, hd] = c_ref[d, hd].T


def _mlstm_call(mq_t, mk, mv_t, gates_t, tri_t, state0):
    B, T, _ = mk.shape
    L = MLSTM_CHUNK
    nc = T // L
    zero_init = state0 is None
    fwd = lambda w: pl.BlockSpec((None, L, w), lambda b, c: (b, c, 0))
    bwd = lambda w: pl.BlockSpec((None, L, w), lambda b, c: (b, nc - 1 - c, 0))
    fwd_t = lambda w: pl.BlockSpec((None, w, L), lambda b, c: (b, 0, c))
    bwd_t = lambda w: pl.BlockSpec((None, w, L), lambda b, c: (b, 0, nc - 1 - c))
    W = MLSTM_WIDTH
    NG = 2 * N_DIR * MLSTM_HEADS
    c_spec = pl.BlockSpec((None, N_DIR, MLSTM_HEADS, MLSTM_HD, MLSTM_HD), lambda b, c: (b, 0, 0, 0, 0))
    n_spec = pl.BlockSpec((None, N_DIR * MLSTM_HEADS, MLSTM_HD), lambda b, c: (b, 0, 0))
    m_spec = pl.BlockSpec((None, 1, LANES), lambda b, c: (b, 0, 0))
    ins = [mq_t, mk, mv_t, gates_t, mq_t, mk, mv_t, gates_t, tri_t]
    in_specs = [fwd_t(W), fwd(W), fwd_t(W), fwd_t(NG), bwd_t(W), bwd(W), bwd_t(W), bwd_t(NG), _const_spec(tri_t.shape)]
    if not zero_init:
        ins += list(state0)
        in_specs += [c_spec, n_spec, m_spec]
    return pl.pallas_call(
        functools.partial(_mlstm_kernel, zero_init=zero_init),
        grid=(B, nc),
        in_specs=in_specs,
        out_specs=[fwd(W), bwd(W), c_spec, n_spec, m_spec],
        out_shape=[jax.ShapeDtypeStruct((B, T, W), F32), jax.ShapeDtypeStruct((B, T, W), F32),
                   jax.ShapeDtypeStruct((B, N_DIR, MLSTM_HEADS, MLSTM_HD, MLSTM_HD), F32),
                   jax.ShapeDtypeStruct((B, N_DIR * MLSTM_HEADS, MLSTM_HD), F32),
                   jax.ShapeDtypeStruct((B, 1, LANES), F32)],
        compiler_params=_cparams("parallel", "arbitrary"),
        name="mlstm",
    )(*ins)


def _residual_ffn(x_ref, mix_fn, mod_ref, gpost_ref, gfpre_ref, gfpost_ref, wout_ref, w1_ref, w2_ref, o_ref):
    half = D_MODEL // 2
    sub = x_ref.shape[0] // POST_SUBTILES
    rows = [slice(i * sub, (i + 1) * sub) for i in range(POST_SUBTILES)]
    ys = []
    for r in rows:
        mix_a, mix_b = mix_fn(r)
        ys.append(_dot(mix_a.astype(BF16), wout_ref[:half, :]) + _dot(mix_b.astype(BF16), wout_ref[half:, :]))
    for r, y in zip(rows, ys):
        x1 = x_ref[r, :] + mod_ref[2] * _rms(y, gpost_ref[...])
        h2 = (_rms(x1, gfpre_ref[...]) * (1.0 + mod_ref[4]) + mod_ref[3]).astype(BF16)
        chunks = list(range(0, D_FF, FF_CHUNK))
        up = _dot(h2, w1_ref[:, chunks[0]:chunks[0] + FF_CHUNK])
        f = None
        for i, c0 in enumerate(chunks):
            cur = up
            if i + 1 < len(chunks):
                up = _dot(h2, w1_ref[:, chunks[i + 1]:chunks[i + 1] + FF_CHUNK])
            part = _dot(jnp.square(jnp.maximum(cur, 0.0)).astype(BF16), w2_ref[c0:c0 + FF_CHUNK, :])
            f = part if f is None else f + part
        o_ref[r, :] = x1 + mod_ref[5] * _rms(f, gfpost_ref[...])


def _head_rms(y, gain_ref, width):
    outs = []
    for h0 in range(0, y.shape[1], width):
        seg = y[:, h0:h0 + width]
        outs.append(seg * lax.rsqrt(jnp.mean(seg * seg, axis=-1, keepdims=True) + EPS) * gain_ref[:, h0:h0 + width])
    return jnp.concatenate(outs, axis=1)


def _post_even_kernel(x_ref, attn_ref, hf_ref, hb_ref, mo_ref, mod_ref, gm_ref, gpost_ref, gfpre_ref, gfpost_ref,
                      wout_ref, w1_ref, w2_ref, o_ref):
    def mix(r):
        hm = _head_rms(hf_ref[r, :] + hb_ref[r, :], gm_ref, MLSTM_HD) * jax.nn.sigmoid(mo_ref[r, :])
        return attn_ref[r, :], hm

    _residual_ffn(x_ref, mix, mod_ref, gpost_ref, gfpre_ref, gfpost_ref, wout_ref, w1_ref, w2_ref, o_ref)


def _post_odd_kernel(x_ref, lru_ref, rg_ref, of_ref, ob_ref, gg_ref, mod_ref, gm_ref, gpost_ref, gfpre_ref, gfpost_ref,
                     wout_ref, w1_ref, w2_ref, o_ref):
    def mix(r):
        lru = jax.nn.gelu(rg_ref[r, :]) * lru_ref[r, :]
        og = _head_rms(of_ref[r, :] + ob_ref[r, :], gm_ref, GLA_DV) * jax.nn.silu(gg_ref[r, :])
        return lru, og

    _residual_ffn(x_ref, mix, mod_ref, gpost_ref, gfpre_ref, gfpost_ref, wout_ref, w1_ref, w2_ref, o_ref)


def _post_call(kernel_fn, name, x, mix_ins, mod, mod_row0, mod_stride, consts):
    B, T, _ = x.shape
    tm = min(TOKEN_TILE, T)
    ins = [x] + list(mix_ins) + [mod] + list(consts)
    in_specs = ([_tok_spec(tm, D_MODEL)] + [_tok_spec(tm, a.shape[2]) for a in mix_ins] + [_mod_spec(mod_row0, mod_stride)]
                + [_const_spec(a.shape) for a in consts])
    return pl.pallas_call(
        kernel_fn,
        grid=(B, T // tm),
        in_specs=in_specs,
        out_specs=_tok_spec(tm, D_MODEL),
        out_shape=jax.ShapeDtypeStruct(x.shape, F32),
        compiler_params=_cparams("parallel", "parallel"),
        name=name,
    )(*ins)


def _inproj_odd_kernel(x_ref, xp_ref, xn_ref, mod_ref, g_ref, wmain_ref, wlr_in_ref, wlr_hi_ref, wlr_lo_ref, blr_ref,
                       cw_ref, cb_ref, wg_ref, bg_ref, lam_ref,
                       rg_out, gq_out, gk_out, gv_out, gg_out, la_out, af_out, uf_out, ab_out, ub_out):
    tm = x_ref.shape[0]
    W = LRU_WIDTH

    def modulated(xv):
        return (_rms(xv, g_ref[...]) * (1.0 + mod_ref[1]) + mod_ref[0]).astype(BF16)

    hb = modulated(x_ref[...])
    hb_ext = jnp.concatenate([modulated(xp_ref[...]), hb, modulated(xn_ref[...])], axis=0)
    glr_hi, glr_lo = _split2(_dot(hb, wlr_in_ref[...]))
    z = _dot(glr_hi, wlr_hi_ref[...]) + (_dot(glr_hi, wlr_lo_ref[...]) + _dot(glr_lo, wlr_hi_ref[...])) + blr_ref[...]
    la_out[...] = _log_sigmoid(z) * (1.0 / GLA_TAU)

    n = tm + 2 * CONV_HALO
    rx = _dot(hb_ext, wmain_ref[:, :W])
    rid = lax.broadcasted_iota(jnp.int32, (n, W), 0)
    first = pl.program_id(1) == 0
    last = pl.program_id(1) == pl.num_programs(1) - 1
    outside = (first & (rid < CONV_HALO)) | (last & (rid >= tm + CONV_HALO))
    rx = jnp.where(outside, 0.0, rx)

    def main_piece(p):
        mm = _dot(hb, wmain_ref[:, (p + 1) * W:(p + 2) * W])
        if p == 0:
            rg_out[...] = mm
        elif p == 1:
            gq_out[...] = mm[:, :GLA_KW] * GLA_DK ** -0.5
            gk_out[...] = mm[:, GLA_KW:]
        elif p == 2:
            gv_out[...] = mm.astype(BF16)
        else:
            gg_out[...] = mm

    neg_half_c_sp = (-0.5 * LRU_C) * _softplus(-lam_ref[...])

    def coefficients(q):
        sub = tm // ODD_ROW_GROUPS
        m = sub + 2 * CONV_HALO
        win = rx[q * sub:q * sub + m, :]
        xc = cb_ref[...] + win * cw_ref[CONV_LEFT:CONV_LEFT + 1, :]
        for tap in range(CONV_W):
            off = tap - CONV_LEFT
            if off != 0:
                xc = xc + pltpu.roll(win, (-off) % m, axis=0) * cw_ref[tap:tap + 1, :]
        xc = xc[CONV_HALO:CONV_HALO + sub, :]
        th = jnp.tanh(_dot(xc.astype(BF16), wg_ref[...]) + bg_ref[...])
        half_xc = 0.5 * xc
        rows = slice(q * sub, (q + 1) * sub)
        for d, (a_out, u_out) in enumerate(((af_out, uf_out), (ab_out, ub_out))):
            th_r = th[:, 2 * d * W:(2 * d + 1) * W]
            th_i = th[:, (2 * d + 1) * W:(2 * d + 2) * W]
            log_a = neg_half_c_sp[d:d + 1, :] * th_r + neg_half_c_sp[d:d + 1, :]
            a = jnp.exp(log_a)
            a_out[rows, :] = a
            v = jnp.tanh(log_a) * (-1.0 - a * a)
            u_out[rows, :] = jnp.where(v > 0.0, v * lax.rsqrt(v), 0.0) * (half_xc * th_i + half_xc)

    n_pieces = (wmain_ref.shape[1] - W) // W
    for step in range(max(n_pieces, ODD_ROW_GROUPS)):
        if step < n_pieces:
            main_piece(step)
        if step < ODD_ROW_GROUPS:
            coefficients(step)


def _inproj_odd_call(x, mod, mod_row0, mod_stride, prm, lru_prm):
    B, T, _ = x.shape
    tm = min(TOKEN_TILE, T)
    halo_per_tile = tm // CONV_HALO
    n_halo = T // CONV_HALO
    prev_spec = pl.BlockSpec((None, CONV_HALO, D_MODEL), lambda b, i: (b, jnp.maximum(i * halo_per_tile - 1, 0), 0))
    next_spec = pl.BlockSpec((None, CONV_HALO, D_MODEL),
                             lambda b, i: (b, jnp.minimum((i + 1) * halo_per_tile, n_halo - 1), 0))
    consts = [prm["g_mix_pre"], prm["w_main"], prm["w_lr_in"], prm["w_lr_hi"], prm["w_lr_lo"], prm["b_lr"],
              lru_prm["conv_w"], lru_prm["conv_b"], lru_prm["w_gate"], lru_prm["b_gate"], lru_prm["lam"]]
    ins = [x, x, x, mod] + consts
    in_specs = ([_tok_spec(tm, D_MODEL), prev_spec, next_spec, _mod_spec(mod_row0, mod_stride)]
                + [_const_spec(a.shape) for a in consts])
    widths = [(LRU_WIDTH, F32), (GLA_KW, F32), (GLA_KW, F32), (GLA_WIDTH, BF16), (GLA_WIDTH, F32),
              (N_DIR * GLA_KW, F32)] + [(LRU_WIDTH, F32)] * 4
    return pl.pallas_call(
        _inproj_odd_kernel,
        grid=(B, T // tm),
        in_specs=in_specs,
        out_specs=[_tok_spec(tm, w) for w, _ in widths],
        out_shape=[jax.ShapeDtypeStruct((B, T, w), dt) for w, dt in widths],
        compiler_params=_cparams("parallel", "parallel"),
        name="inproj_odd",
    )(*ins)


def _lru_kernel(*refs, zero_init):
    if zero_init:
        af_s, uf_s, ab_s, ub_s, o_ref, last_ref, hb_s = refs
    else:
        af_s, uf_s, ab_s, ub_s, h0_ref, o_ref, last_ref, hb_s = refs
    T, W = af_s.shape
    nblk = T // SUBLANES
    rid8 = lax.broadcasted_iota(jnp.int32, (SUBLANES, W), 0)

    def block_scan(a, u, reverse):
        for dist in (1, 2, 4):
            if reverse:
                keep = rid8 < SUBLANES - dist
                shift = SUBLANES - dist
            else:
                keep = rid8 >= dist
                shift = dist
            a_n = jnp.where(keep, pltpu.roll(a, shift, axis=0), 1.0)
            u_n = jnp.where(keep, pltpu.roll(u, shift, axis=0), 0.0)
            u = a * u_n + u
            a = a * a_n
        return a, u

    def body(blk, carry):
        hf, hb = carry
        rf = pl.multiple_of(blk * SUBLANES, SUBLANES)
        a, u = block_scan(af_s[pl.ds(rf, SUBLANES), :], uf_s[pl.ds(rf, SUBLANES), :], False)
        out_f = u + a * hf
        o_ref[pl.ds(rf, SUBLANES), :] = out_f
        rb = pl.multiple_of((nblk - 1 - blk) * SUBLANES, SUBLANES)
        a, u = block_scan(ab_s[pl.ds(rb, SUBLANES), :], ub_s[pl.ds(rb, SUBLANES), :], True)
        out_b = u + a * hb
        hb_s[pl.ds(rb, SUBLANES), :] = out_b
        return out_f[SUBLANES - 1:SUBLANES, :], out_b[0:1, :]

    if zero_init:
        init = (jnp.zeros((1, W), F32), jnp.zeros((1, W), F32))
    else:
        init = (h0_ref[0:1, :], h0_ref[1:2, :])
    hf, hb = lax.fori_loop(0, nblk, body, init)
    last_ref[0:1, :] = hf
    last_ref[1:2, :] = hb
    o_ref[...] = o_ref[...] + hb_s[...]


def _lru_call(coeffs, h0):
    B, T, W = coeffs[0].shape
    zero_init = h0 is None
    seq = pl.BlockSpec((None, T, W), lambda b: (b, 0, 0))
    st = pl.BlockSpec((None, N_DIR, W), lambda b: (b, 0, 0))
    ins = list(coeffs)
    in_specs = [seq] * len(ins)
    if not zero_init:
        ins.append(h0)
        in_specs.append(st)
    return pl.pallas_call(
        functools.partial(_lru_kernel, zero_init=zero_init),
        grid=(B,),
        in_specs=in_specs,
        out_specs=[seq, st],
        out_shape=[jax.ShapeDtypeStruct((B, T, W), F32), jax.ShapeDtypeStruct((B, N_DIR, W), F32)],
        scratch_shapes=[pltpu.VMEM((T, W), F32)],
        compiler_params=_cparams("parallel"),
        name="rglru",
    )(*ins)


GLA_LEVELS = 6


def _gla_constants():
    L = GLA_CHUNK
    idx = np.arange(L)
    sel = np.zeros((N_DIR, (GLA_LEVELS + 3) * L, L), np.float32)
    lvl_mask = np.zeros((N_DIR, GLA_LEVELS + 1, L, L), np.float32)
    for d in range(N_DIR):
        tri = (idx[None, :] <= idx[:, None]) if d == 0 else (idx[None, :] >= idx[:, None])
        tri = tri.astype(np.float32)
        last = L - 1 if d == 0 else 0
        sel[d, 0:L] = tri
        for lv in range(GLA_LEVELS):
            m = L >> (lv + 1)
            start = (idx // (2 * m)) * (2 * m)
            second = (idx - start) >= m
            boundary = start + (m - 1 if d == 0 else m)
            query_role = second if d == 0 else ~second
            diff = tri - tri[boundary]
            sel[d, (1 + lv) * L:(2 + lv) * L] = np.where(query_role[:, None], diff, -diff)
            same = start[:, None] == start[None, :]
            lvl_mask[d, lv] = same & query_role[:, None] & ~query_role[None, :]
        lvl_mask[d, GLA_LEVELS] = np.eye(L)
        sel[d, (GLA_LEVELS + 1) * L:(GLA_LEVELS + 2) * L] = tri[last][None, :] - tri
        sel[d, (GLA_LEVELS + 2) * L:] = tri[last][None, :]
    assert sel.min() >= 0.0 and sel.max() <= 1.0
    sel3 = np.concatenate([sel] * 3, axis=2)
    lvl_mask = np.tile(lvl_mask, (1, 1, 1, GLA_HEADS))
    return jnp.asarray(sel3, BF16), jnp.asarray(lvl_mask, F32)


def _gla_kernel(*refs, zero_init):
    if zero_init:
        (qf_ref, kf_ref, vf_ref, af_ref, qb_ref, kb_ref, vb_ref, ab_ref, sel_ref, msk_ref, of_ref, ob_ref, s_ref) = refs
    else:
        (qf_ref, kf_ref, vf_ref, af_ref, qb_ref, kb_ref, vb_ref, ab_ref, sel_ref, msk_ref, s0_ref,
         of_ref, ob_ref, s_ref) = refs
    L = GLA_CHUNK
    H = GLA_HEADS
    KW = GLA_KW
    VW = GLA_WIDTH
    DK = GLA_DK
    DV = GLA_DV

    @pl.when(pl.program_id(1) == 0)
    def _init():
        if zero_init:
            s_ref[...] = jnp.zeros(s_ref.shape, F32)
        else:
            s_ref[...] = s0_ref[...]

    zero_k = jnp.zeros((L, KW), BF16)
    zero_v = jnp.zeros((L, VW), BF16)
    lane_k = lax.broadcasted_iota(jnp.int32, (L, KW), 1) // DK
    lane_v = lax.broadcasted_iota(jnp.int32, (L, VW), 1) // DV

    def stack_heads(a, lane_head, zero):
        return jnp.concatenate([jnp.where(lane_head == h, a, zero) for h in range(H)], axis=0)

    def tile_rows(a):
        return jnp.concatenate([a] * H, axis=0)

    dirs = ((qf_ref, kf_ref, vf_ref, af_ref, of_ref), (qb_ref, kb_ref, vb_ref, ab_ref, ob_ref))
    work = []
    for i in range(GLA_STEP_CHUNKS):
        for d, (q_ref, k_ref, v_ref, a_ref, o_ref) in enumerate(dirs):
            j = i if d == 0 else GLA_STEP_CHUNKS - 1 - i
            rows = slice(j * L, (j + 1) * L)
            sums = _dot(sel_ref[d], jnp.concatenate(_split3(a_ref[rows, :]), axis=0))
            q_b = q_ref[rows, :].astype(BF16)
            work.append(dict(d=d, rows=rows, sums=sums, q_b=q_b, q_st=stack_heads(q_b, lane_k, zero_k),
                             k_st=stack_heads(k_ref[rows, :].astype(BF16), lane_k, zero_k)))

    for lv in range(GLA_LEVELS + 1):
        for wk in work:
            if lv == GLA_LEVELS:
                part = _dot_nt(wk["q_b"], wk["k_st"])
            else:
                wl = jnp.exp(wk["sums"][(1 + lv) * L:(2 + lv) * L]).astype(BF16)
                part = _dot_nt(wk["q_b"] * wl, wk["k_st"] * tile_rows(wl))
            part = part * msk_ref[wk["d"], lv]
            wk["scores"] = part if lv == 0 else wk["scores"] + part

    state = [s_ref[d] for d in range(N_DIR)]
    for wk in work:
        d, rows, sums = wk["d"], wk["rows"], wk["sums"]
        _, k_ref, v_ref, _, o_ref = dirs[d]
        v = v_ref[rows, :]
        s_prev = state[d]
        intra = _dot(wk["scores"].astype(BF16), stack_heads(v, lane_v, zero_v))
        q_in = wk["q_st"] * tile_rows(jnp.exp(sums[0:L]).astype(BF16))
        inter = _dot(q_in, s_prev.astype(BF16))
        o_ref[rows, :] = intra + jnp.concatenate([inter[h * L:(h + 1) * L] for h in range(H)], axis=1)

        k_out = k_ref[rows, :] * jnp.exp(sums[(GLA_LEVELS + 1) * L:(GLA_LEVELS + 2) * L])
        dec = jnp.exp(sums[(GLA_LEVELS + 2) * L:])
        both_t = jnp.concatenate([k_out, dec], axis=0).T
        k_out_t = both_t[:, :L].astype(BF16)
        upd = [_dot(k_out_t[h * DK:(h + 1) * DK], v[:, h * DV:(h + 1) * DV]) for h in range(H)]
        state[d] = both_t[:, L:L + 1] * s_prev + jnp.concatenate(upd, axis=0)
    for d in range(N_DIR):
        s_ref[d] = state[d]


def _gla_call(gq, gk, gv, log_a, sel3, lvl_mask, s0):
    B, T, _ = gq.shape
    L = GLA_CHUNK * GLA_STEP_CHUNKS
    nc = T // L
    zero_init = s0 is None
    fwd = lambda w: pl.BlockSpec((None, L, w), lambda b, c: (b, c, 0))
    bwd = lambda w: pl.BlockSpec((None, L, w), lambda b, c: (b, nc - 1 - c, 0))
    a_fwd = pl.BlockSpec((None, L, GLA_KW), lambda b, c: (b, c, 0))
    a_bwd = pl.BlockSpec((None, L, GLA_KW), lambda b, c: (b, nc - 1 - c, 1))
    s_spec = pl.BlockSpec((None, N_DIR, GLA_KW, GLA_DV), lambda b, c: (b, 0, 0, 0))
    ins = [gq, gk, gv, log_a, gq, gk, gv, log_a, sel3, lvl_mask]
    in_specs = [fwd(GLA_KW), fwd(GLA_KW), fwd(GLA_WIDTH), a_fwd, bwd(GLA_KW), bwd(GLA_KW), bwd(GLA_WIDTH), a_bwd,
                _const_spec(sel3.shape), _const_spec(lvl_mask.shape)]
    if not zero_init:
        ins.append(s0)
        in_specs.append(s_spec)
    return pl.pallas_call(
        functools.partial(_gla_kernel, zero_init=zero_init),
        grid=(B, nc),
        in_specs=in_specs,
        out_specs=[fwd(GLA_WIDTH), bwd(GLA_WIDTH), s_spec],
        out_shape=[jax.ShapeDtypeStruct((B, T, GLA_WIDTH), F32), jax.ShapeDtypeStruct((B, T, GLA_WIDTH), F32),
                   jax.ShapeDtypeStruct((B, N_DIR, GLA_KW, GLA_DV), F32)],
        compiler_params=_cparams("parallel", "arbitrary"),
        name="gla",
    )(*ins)


def _block_diag(blocks):
    n, r, c = blocks.shape
    eye = jnp.eye(n, dtype=blocks.dtype)
    return (eye[:, None, :, None] * blocks[:, :, None, :]).reshape(n * r, n * c)


def _pad_cols(a, width):
    return jnp.pad(a, ((0, 0), (0, width - a.shape[1])))


def _rope_tables(n_tokens):
    rows = n_tokens // GRID_W
    row = jnp.repeat(jnp.arange(rows), GRID_W).astype(F32)
    col = jnp.tile(jnp.arange(GRID_W), rows).astype(F32)
    inv = jnp.power(ROPE_THETA, -jnp.arange(ROPE_PAIRS_PER_AXIS, dtype=F32) / ROPE_PAIRS_PER_AXIS)
    ang = jnp.concatenate([row[:, None] * inv, col[:, None] * inv], axis=-1)
    cos, sin = jnp.cos(ang), jnp.sin(ang)
    reps = LANES // HEAD_DIM
    return jnp.tile(jnp.concatenate([cos, cos], axis=-1), (1, reps)), jnp.tile(jnp.concatenate([-sin, sin], axis=-1), (1, reps))


def _cache_variants(cache):
    z = jnp.zeros_like(cache[:, 0])
    return jnp.concatenate([cache[:, 0], z, z, cache[:, 0], cache[:, 1], z, z, cache[:, 1]], axis=-1).astype(BF16)


def _mlstm_tri():
    idx = np.arange(MLSTM_CHUNK)
    upper = idx[:, None] <= idx[None, :]
    tri_t = np.stack([upper, upper.T]).astype(np.float32)
    return jnp.asarray(np.concatenate([tri_t] * 3, axis=1), BF16)


def kernel(x_prompt, x_sample, cache_attn_k, cache_attn_v, state_mlstm_C, state_mlstm_n, state_mlstm_m, state_lru_h, state_gla_S, c, c_ctx, ada_w, ada_b, norm_mix_pre, norm_mix_post, norm_ffn_pre, norm_ffn_post, w_out, ffn_w1, ffn_w2, w_in_even, attn_q_norm, attn_k_norm, mlstm_i_bias, mlstm_f_bias, mlstm_norm, w_in_odd, lru_conv_w, lru_conv_b, lru_w_r, lru_b_r, lru_w_i, lru_b_i, lru_lambda, gla_w_lr, gla_b_lr, gla_norm):
    depth = ada_w.shape[0]
    Bp = x_prompt.shape[0]
    Bs = x_sample.shape[0]
    row = lambda a: a.reshape(1, -1)

    n_rows = -(-(1 + Bs) // SUBLANES) * SUBLANES
    cvec = jnp.zeros((n_rows, D_MODEL), F32).at[0].set(c_ctx).at[1:1 + Bs].set(c)
    mod_all = _ada_call(cvec, ada_w, ada_b).reshape(depth, n_rows, 6, 1, D_MODEL)

    rope_tabs = _rope_tables(x_sample.shape[1])
    ones64 = _block_diag(jnp.ones((ATT_HEADS, HEAD_DIM, HEAD_DIM), BF16))
    tri_t = _mlstm_tri()
    sel3, lvl_mask = _gla_constants()

    xp, xs = x_prompt, x_sample
    outs = {}
    for l in range(depth):
        mod = mod_all[l]
        tail = [row(norm_mix_post[l]), row(norm_ffn_pre[l]), row(norm_ffn_post[l]),
                w_out[l].astype(BF16), ffn_w1[l].astype(BF16), ffn_w2[l].astype(BF16)]
        if l % 2 == 0:
            e = l // 2
            w_in = w_in_even[e]
            o1 = ATT_WIDTH + 2 * ATT_KV_WIDTH
            o2 = o1 + 4 * MLSTM_WIDTH
            prm = {
                "g_mix_pre": row(norm_mix_pre[l]),
                "w_qkv": w_in[:, :o1].astype(BF16),
                "w_m": jnp.concatenate([w_in[:, o1 + MLSTM_WIDTH:o1 + 2 * MLSTM_WIDTH], w_in[:, o1 + 3 * MLSTM_WIDTH:o2]],
                                       axis=1).astype(BF16),
                "w_mt": jnp.concatenate([w_in[:, o1:o1 + MLSTM_WIDTH], w_in[:, o1 + 2 * MLSTM_WIDTH:o1 + 3 * MLSTM_WIDTH]],
                                        axis=1).T.astype(BF16),
                "w_gt": w_in[:, o2:].T.astype(BF16),
                "gate_bias_t": jnp.concatenate([mlstm_i_bias[e].reshape(-1), mlstm_f_bias[e].reshape(-1)]).reshape(-1, 1),
                "q_gain": row(jnp.tile(attn_q_norm[e], ATT_HEADS)),
                "k_gain": row(jnp.tile(attn_k_norm[e], ATT_KV_HEADS)),
                "ones64": ones64,
            }
            consts = [row(mlstm_norm[e])] + tail
            for path in ("prompt", "sample"):
                if path == "prompt":
                    x, r0, rs, tabs, cache, st0 = xp, 0, 0, None, None, None
                else:
                    x, r0, rs, tabs = xs, 1, 1, rope_tabs
                    cache = (_cache_variants(cache_attn_k[:, e]), _cache_variants(cache_attn_v[:, e]))
                    st0 = (state_mlstm_C[:, e], state_mlstm_n[:, e].reshape(Bs, N_DIR * MLSTM_HEADS, MLSTM_HD),
                           _pad_cols(state_mlstm_m[:, e].reshape(Bs, -1), LANES).reshape(Bs, 1, LANES))
                q, kk, vv, k_n, v_n, mq_t, mk, mv_t, mo, gates_t = _inproj_even_call(x, mod, r0, rs, prm, tabs)
                attn = _attn_call(q, kk, vv, cache)
                hf, hb, c_fin, n_fin, m_fin = _mlstm_call(mq_t, mk, mv_t, gates_t, tri_t, st0)
                x_new = _post_call(_post_even_kernel, "post_even", x, [attn, hf, hb, mo], mod, r0, rs, consts)
                if path == "prompt":
                    xp = x_new
                    T = x.shape[1]
                    outs.setdefault("k", []).append(jnp.swapaxes(k_n.reshape(Bp, T, ATT_KV_HEADS, HEAD_DIM), 1, 2))
                    outs.setdefault("v", []).append(jnp.swapaxes(v_n.reshape(Bp, T, ATT_KV_HEADS, HEAD_DIM), 1, 2))
                    outs.setdefault("C", []).append(c_fin)
                    outs.setdefault("n", []).append(n_fin.reshape(Bp, N_DIR, MLSTM_HEADS, MLSTM_HD))
                    outs.setdefault("m", []).append(m_fin[:, 0, :N_DIR * MLSTM_HEADS].reshape(Bp, N_DIR, MLSTM_HEADS))
                else:
                    xs = x_new
        else:
            o = l // 2
            w_in = w_in_odd[o]
            o1 = 2 * LRU_WIDTH + 2 * GLA_KW + 2 * GLA_WIDTH
            w_lr = _block_diag(gla_w_lr[o])
            w_lr = jnp.pad(w_lr, ((0, LANES - w_lr.shape[0]), (0, 0)))
            w_lr_hi = w_lr.astype(BF16)
            prm = {
                "g_mix_pre": row(norm_mix_pre[l]),
                "w_main": w_in[:, :o1].astype(BF16),
                "w_lr_in": _pad_cols(w_in[:, o1:], LANES).astype(BF16),
                "w_lr_hi": w_lr_hi,
                "w_lr_lo": (w_lr - w_lr_hi.astype(F32)).astype(BF16),
                "b_lr": row(gla_b_lr[o]),
            }
            lru_prm = {
                "conv_w": lru_conv_w[o],
                "conv_b": row(lru_conv_b[o]),
                "w_gate": (0.5 * jnp.concatenate([_block_diag(lru_w_r[o, 0]), _block_diag(lru_w_i[o, 0]),
                                                  _block_diag(lru_w_r[o, 1]), _block_diag(lru_w_i[o, 1])], axis=1)).astype(BF16),
                "b_gate": 0.5 * row(jnp.stack([lru_b_r[o, 0], lru_b_i[o, 0], lru_b_r[o, 1], lru_b_i[o, 1]])),
                "lam": lru_lambda[o],
            }
            consts = [row(gla_norm[o])] + tail
            for path in ("prompt", "sample"):
                if path == "prompt":
                    x, r0, rs, h0, s0 = xp, 0, 0, None, None
                else:
                    x, r0, rs, h0 = xs, 1, 1, state_lru_h[:, o]
                    s0 = state_gla_S[:, o].reshape(Bs, N_DIR, GLA_KW, GLA_DV)
                rg, gq, gk, gv, gg, log_a, *lru_coeffs = _inproj_odd_call(x, mod, r0, rs, prm, lru_prm)
                lru, h_last = _lru_call(lru_coeffs, h0)
                of, ob, s_fin = _gla_call(gq, gk, gv, log_a, sel3, lvl_mask, s0)
                x_new = _post_call(_post_odd_kernel, "post_odd", x, [lru, rg, of, ob, gg], mod, r0, rs, consts)
                if path == "prompt":
                    xp = x_new
                    outs.setdefault("h", []).append(h_last)
                    outs.setdefault("S", []).append(s_fin.reshape(Bp, N_DIR, GLA_HEADS, GLA_DK, GLA_DV))
                else:
                    xs = x_new

    stack = lambda name: jnp.stack(outs[name], axis=1)
    return (xp, xs, stack("k"), stack("v"), stack("C"), stack("n"), stack("m"), stack("h"), stack("S"))
```

```python
import functools

import numpy as np
import jax
import jax.numpy as jnp
from jax import lax
from jax.experimental import pallas as pl
from jax.experimental.pallas import tpu as pltpu

F32 = jnp.float32
BF16 = jnp.bfloat16

D_MODEL = 1024
D_FF = 4 * D_MODEL
GRID_W = 64
EPS = 1e-6
LOG2_E = 1.4426950408889634
N_DIR = 2

ATT_HEADS = 8
ATT_KV_HEADS = 2
HEAD_DIM = 64
ATT_WIDTH = ATT_HEADS * HEAD_DIM
ATT_KV_WIDTH = ATT_KV_HEADS * HEAD_DIM
ROPE_THETA = 10000.0
ROPE_PAIRS_PER_AXIS = HEAD_DIM // 4

MLSTM_HEADS = 4
MLSTM_HD = 128
MLSTM_WIDTH = MLSTM_HEADS * MLSTM_HD
MLSTM_CHUNK = 256
MLSTM_STAGE_LAG = 2
M_INIT = -1e30

LRU_WIDTH = 512
LRU_BLOCKS = 8
LRU_BD = LRU_WIDTH // LRU_BLOCKS
LRU_C = 8.0
CONV_W = 4
CONV_LEFT = 2

GLA_HEADS = 4
GLA_DK = 64
GLA_DV = 128
GLA_KW = GLA_HEADS * GLA_DK
GLA_WIDTH = GLA_HEADS * GLA_DV
GLA_RANK = 16
GLA_TAU = 16.0
GLA_CHUNK = 64
GLA_STEP_CHUNKS = 4

LANES = 128
SUBLANES = 8
VMEM_LIMIT = 56 * 1024 * 1024

TOKEN_TILE = 512
ATTN_Q_TILE = 256
ATTN_SCORES_AHEAD = 1
FF_CHUNK = 1024
POST_TILE = 512
POST_SUBTILE = 256
ODD_GROUP_ROWS = 256
CONV_HALO = 2 * SUBLANES


def _cparams(*sem):
    return pltpu.CompilerParams(dimension_semantics=sem, vmem_limit_bytes=VMEM_LIMIT)


def _const_spec(shape):
    n = len(shape)
    return pl.BlockSpec(shape, lambda *_: (0,) * n, pipeline_mode=pl.Buffered(1))


def _dot(a, b):
    return jnp.dot(a, b, preferred_element_type=F32)


def _dot_nt(a, b):
    return lax.dot_general(a, b, (((1,), (1,)), ((), ())), preferred_element_type=F32)


def _dot_tn(a, b):
    return lax.dot_general(a, b, (((0,), (0,)), ((), ())), preferred_element_type=F32)


def _split3(x):
    hi = x.astype(BF16)
    r = x - hi.astype(F32)
    mid = r.astype(BF16)
    lo = (r - mid.astype(F32)).astype(BF16)
    return hi, mid, lo


def _split2(x):
    hi = x.astype(BF16)
    lo = (x - hi.astype(F32)).astype(BF16)
    return hi, lo


def _rms(x, g):
    return x * lax.rsqrt(jnp.mean(x * x, axis=-1, keepdims=True) + EPS) * g


def _log_sigmoid(x):
    return jnp.minimum(x, 0.0) - jnp.log1p(jnp.exp(-jnp.abs(x)))


def _softplus(x):
    return jnp.maximum(x, 0.0) + jnp.log1p(jnp.exp(-jnp.abs(x)))


def _group_rms(y, gain, ones_bd, width):
    hi, lo = _split2(y * y)
    ss = _dot(hi, ones_bd) + _dot(lo, ones_bd)
    return y * lax.rsqrt(ss * (1.0 / width) + EPS) * gain


def _ada_kernel(c_ref, w_ref, b_ref, o_ref):
    s = jax.nn.silu(c_ref[...])
    o_ref[...] = _dot(s.astype(BF16), w_ref[...].astype(BF16)) + b_ref[...]


def _ada_call(cvec, ada_w, ada_b):
    depth = ada_w.shape[0]
    rows = cvec.shape[0]
    n_col = ada_w.shape[2] // D_MODEL
    return pl.pallas_call(
        _ada_kernel,
        grid=(depth, n_col),
        in_specs=[
            pl.BlockSpec((rows, D_MODEL), lambda l, j: (0, 0)),
            pl.BlockSpec((None, D_MODEL, D_MODEL), lambda l, j: (l, 0, j)),
            pl.BlockSpec((None, 1, D_MODEL), lambda l, j: (l, 0, j)),
        ],
        out_specs=pl.BlockSpec((None, rows, D_MODEL), lambda l, j: (l, 0, j)),
        out_shape=jax.ShapeDtypeStruct((depth, rows, ada_w.shape[2]), F32),
        compiler_params=_cparams("arbitrary", "arbitrary"),
        name="ada_mod",
    )(cvec, ada_w, ada_b.reshape(depth, 1, -1))


def _mod_spec(row0, row_stride):
    return pl.BlockSpec((None, 6, 1, D_MODEL), lambda b, i: (row0 + row_stride * b, 0, 0, 0))


def _tok_spec(tm, width):
    return pl.BlockSpec((None, tm, width), lambda b, i: (b, i, 0))


def _rope(y, cos, sin_signed):
    width = y.shape[1]
    reps = width // LANES
    cosw = jnp.concatenate([cos] * reps, axis=1) if reps > 1 else cos
    sinw = jnp.concatenate([sin_signed] * reps, axis=1) if reps > 1 else sin_signed
    lane = lax.broadcasted_iota(jnp.int32, y.shape, 1)
    first_half = (lane % HEAD_DIM) < (HEAD_DIM // 2)
    partner = jnp.where(first_half, pltpu.roll(y, width - HEAD_DIM // 2, axis=1), pltpu.roll(y, HEAD_DIM // 2, axis=1))
    return y * cosw + partner * sinw


def _kv_variants(a):
    lane = lax.broadcasted_iota(jnp.int32, a.shape, 1)
    low = lane < HEAD_DIM
    swapped = pltpu.roll(a, HEAD_DIM, axis=1)
    zero = jnp.zeros_like(a)
    return jnp.concatenate([
        jnp.where(low, a, zero),
        jnp.where(low, zero, swapped),
        jnp.where(low, swapped, zero),
        jnp.where(low, zero, a),
    ], axis=1)


def _inproj_even_kernel(*refs, rope):
    if rope:
        (x_ref, mod_ref, g_ref, wqkv_ref, wm_ref, wmt_ref, wgt_ref, qg_ref, kg_ref, gbt_ref, ones_ref, cos_ref, sin_ref,
         q_out, kk_out, vv_out, k_out, v_out, mqt_out, mk_out, mvt_out, mo_out, gatet_out) = refs
    else:
        (x_ref, mod_ref, g_ref, wqkv_ref, wm_ref, wmt_ref, wgt_ref, qg_ref, kg_ref, gbt_ref, ones_ref,
         q_out, kk_out, vv_out, k_out, v_out, mqt_out, mk_out, mvt_out, mo_out, gatet_out) = refs
    h = _rms(x_ref[...], g_ref[...]) * (1.0 + mod_ref[1]) + mod_ref[0]
    hb = h.astype(BF16)

    qkv = _dot(hb, wqkv_ref[...])
    w = MLSTM_WIDTH
    mm = _dot(hb, wm_ref[...])
    mk_out[...] = mm[:, :w].astype(BF16)
    mo_out[...] = mm[:, w:]
    q = _group_rms(qkv[:, :ATT_WIDTH], qg_ref[...], ones_ref[...], HEAD_DIM)
    k = _group_rms(qkv[:, ATT_WIDTH:ATT_WIDTH + ATT_KV_WIDTH], kg_ref[...], ones_ref[:ATT_KV_WIDTH, :ATT_KV_WIDTH], HEAD_DIM)
    v = qkv[:, ATT_WIDTH + ATT_KV_WIDTH:]
    if rope:
        q = _rope(q, cos_ref[...], sin_ref[...])
        k = _rope(k, cos_ref[...], sin_ref[...])
    q_out[...] = (q * (HEAD_DIM ** -0.5 * LOG2_E)).astype(BF16)
    k_out[...] = k
    v_out[...] = v
    kk_out[...] = _kv_variants(k).astype(BF16)
    vv_out[...] = _kv_variants(v).astype(BF16)

    mmt = _dot_nt(wmt_ref[...], hb)
    mqt_out[...] = (mmt[:w] * MLSTM_HD ** -0.5).astype(BF16)
    mvt_out[...] = mmt[w:].astype(BF16)

    gates_t = _dot_nt(wgt_ref[...], hb) + gbt_ref[...]
    sub = lax.broadcasted_iota(jnp.int32, gates_t.shape, 0)
    gatet_out[...] = jnp.where(sub < N_DIR * MLSTM_HEADS, gates_t, _log_sigmoid(gates_t))


def _inproj_even_call(x, mod, mod_row0, mod_stride, prm, rope_tabs):
    B, T, _ = x.shape
    tm = min(TOKEN_TILE, T)
    rope = rope_tabs is not None
    ins = [x, mod, prm["g_mix_pre"], prm["w_qkv"], prm["w_m"], prm["w_mt"], prm["w_gt"], prm["q_gain"],
           prm["k_gain"], prm["gate_bias_t"], prm["ones64"]]
    in_specs = [_tok_spec(tm, D_MODEL), _mod_spec(mod_row0, mod_stride)] + [_const_spec(a.shape) for a in ins[2:]]
    if rope:
        ins += list(rope_tabs)
        in_specs += [pl.BlockSpec((tm, LANES), lambda b, i: (i, 0))] * 2
    n_gate = 2 * N_DIR * MLSTM_HEADS
    outs = [(ATT_WIDTH, BF16, True), (4 * LANES, BF16, True), (4 * LANES, BF16, True), (ATT_KV_WIDTH, F32, True),
            (ATT_KV_WIDTH, F32, True), (MLSTM_WIDTH, BF16, False), (MLSTM_WIDTH, BF16, True), (MLSTM_WIDTH, BF16, False),
            (MLSTM_WIDTH, F32, True), (n_gate, F32, False)]
    feat_spec = lambda w: pl.BlockSpec((None, w, tm), lambda b, i: (b, 0, i))
    return pl.pallas_call(
        functools.partial(_inproj_even_kernel, rope=rope),
        grid=(B, T // tm),
        in_specs=in_specs,
        out_specs=[_tok_spec(tm, w) if tok else feat_spec(w) for w, _, tok in outs],
        out_shape=[jax.ShapeDtypeStruct((B, T, w) if tok else (B, w, T), dt) for w, dt, tok in outs],
        compiler_params=_cparams("parallel", "parallel"),
        name="inproj_even",
    )(*ins)


def _attn_kernel(*refs, cached):
    if cached:
        q_ref, kk_ref, vv_ref, kc_ref, vc_ref, o_ref = refs
    else:
        q_ref, kk_ref, vv_ref, o_ref = refs
    def value_col(head):
        kv = head // (ATT_HEADS // ATT_KV_HEADS)
        return (2 * kv + head % 2) * LANES

    def scores(head):
        qp = q_ref[:, (head // 2) * LANES:(head // 2 + 1) * LANES]
        col = value_col(head)
        s = _dot_nt(qp, kk_ref[:, col:col + LANES])
        sc = _dot_nt(qp, kc_ref[:, col:col + LANES]) if cached else None
        return s, sc

    def weighted_values(head, s, sc):
        col = value_col(head)
        m = jnp.max(s, axis=-1, keepdims=True)
        if cached:
            m = jnp.maximum(m, jnp.max(sc, axis=-1, keepdims=True))
        p = jnp.exp2(s - m)
        den = jnp.sum(p, axis=-1, keepdims=True)
        o = _dot(p.astype(BF16), vv_ref[:, col:col + LANES])
        if cached:
            pc = jnp.exp2(sc - m)
            den = den + jnp.sum(pc, axis=-1, keepdims=True)
            o = o + _dot(pc.astype(BF16), vc_ref[:, col:col + LANES])
        return o / den

    pending = [scores(h) for h in range(ATTN_SCORES_AHEAD)]
    acc = None
    for head in range(ATT_HEADS):
        if head + ATTN_SCORES_AHEAD < ATT_HEADS:
            pending.append(scores(head + ATTN_SCORES_AHEAD))
        o = weighted_values(head, *pending[head])
        if head % 2 == 0:
            acc = o
        else:
            pair = head // 2
            o_ref[:, pair * LANES:(pair + 1) * LANES] = (acc + o).astype(o_ref.dtype)


def _attn_call(q, kk, vv, cache=None):
    B, T, _ = q.shape
    tq = min(ATTN_Q_TILE, T)
    cached = cache is not None
    ins = [q, kk, vv]
    full = lambda n: pl.BlockSpec((None, n, 4 * LANES), lambda b, i: (b, 0, 0))
    in_specs = [_tok_spec(tq, ATT_WIDTH), full(T), full(T)]
    if cached:
        ins += list(cache)
        in_specs += [full(cache[0].shape[1])] * 2
    return pl.pallas_call(
        functools.partial(_attn_kernel, cached=cached),
        grid=(B, T // tq),
        in_specs=in_specs,
        out_specs=_tok_spec(tq, ATT_WIDTH),
        out_shape=jax.ShapeDtypeStruct((B, T, ATT_WIDTH), BF16),
        compiler_params=_cparams("parallel", "parallel"),
        name="attention",
    )(*ins)


def _mlstm_kernel(*refs, zero_init):
    if zero_init:
        (qf_ref, kf_ref, vf_ref, gtf_ref, qb_ref, kb_ref, vb_ref, gtb_ref, trit_ref,
         hf_ref, hb_ref, c_ref, n_ref, m_ref) = refs
    else:
        (qf_ref, kf_ref, vf_ref, gtf_ref, qb_ref, kb_ref, vb_ref, gtb_ref, trit_ref,
         c0_ref, n0_ref, m0_ref, hf_ref, hb_ref, c_ref, n_ref, m_ref) = refs
    L = MLSTM_CHUNK
    H = MLSTM_HEADS
    HD = MLSTM_HD
    PAD = 2 * SUBLANES
    chunk = pl.program_id(1)

    @pl.when(chunk == 0)
    def _init():
        if zero_init:
            c_ref[...] = jnp.zeros(c_ref.shape, F32)
            n_ref[...] = jnp.zeros(n_ref.shape, F32)
            m_ref[...] = jnp.full(m_ref.shape, M_INIT, F32)
        else:
            for d in range(N_DIR):
                for hd in range(H):
                    c_ref[d, hd] = c0_ref[d, hd].T
            n_ref[...] = n0_ref[...]
            m_ref[...] = m0_ref[...]

    m_all = m_ref[...]
    n_all = n_ref[...]
    m_row = m_all
    m_lane = lax.broadcasted_iota(jnp.int32, m_all.shape, 1)
    key = lax.broadcasted_iota(jnp.int32, (L, L), 0)
    qry = lax.broadcasted_iota(jnp.int32, (L, L), 1)
    dirs = ((qf_ref, kf_ref, vf_ref, gtf_ref, hf_ref, key <= qry, L - 1),
            (qb_ref, kb_ref, vb_ref, gtb_ref, hb_ref, key >= qry, 0))
    n_in = N_DIR * H
    order = [(d, hd) for d in range(N_DIR) for hd in range(H)]

    def query_stage(d, hd):
        qt_ref, k_ref = dirs[d][0], dirs[d][1]
        j = d * H + hd
        sl = slice(hd * HD, (hd + 1) * HD)
        n_blk = jnp.broadcast_to(n_all[j:j + 1, :], (PAD, HD)).astype(BF16)
        return _dot(jnp.concatenate([k_ref[:, sl], c_ref[d, hd].astype(BF16), n_blk], axis=0), qt_ref[sl, :])

    sums = []
    for d, (_, _, _, gt_ref, _, _, _) in enumerate(dirs):
        gates_t = gt_ref[...]
        bcum_t = _dot(jnp.concatenate(_split3(gates_t), axis=1), trit_ref[d])
        sums.append((gates_t, bcum_t))
    c_rows = [sums[d][0][d * H:(d + 1) * H] - sums[d][1][n_in + d * H:n_in + (d + 1) * H] for d in range(N_DIR)]
    c_cols = jnp.concatenate(c_rows, axis=0).T

    def first_stage(d, hd, r):
        _, k_ref, vt_ref, _, _, mask, last = dirs[d]
        gates_t, bcum_t = sums[d]
        j = d * H + hd
        jf = n_in + j
        sl = slice(hd * HD, (hd + 1) * HD)
        k = k_ref[:, sl]
        v_t = vt_ref[sl, :]
        b_row = bcum_t[jf:jf + 1, :]
        i_row = gates_t[j:j + 1, :]
        c_col = c_cols[:, j:j + 1]
        m_prev = m_all[0:1, j:j + 1]
        ct_prev = c_ref[d, hd]
        n_prev = n_all[j:j + 1, :]

        g = b_row + m_prev
        dlog = jnp.where(mask, c_col + b_row, -jnp.inf)
        m_t = jnp.maximum(g, jnp.max(dlog, axis=0, keepdims=True))
        w = jnp.exp(dlog - m_t)
        w_inter = jnp.exp(g - m_t)

        b_last = bcum_t[jf:jf + 1, last:last + 1]
        wlog = b_last - b_row + i_row
        m_new = jnp.maximum(b_last + m_prev, jnp.max(wlog, axis=1, keepdims=True))
        ws = jnp.exp(wlog - m_new)
        decay = jnp.exp(b_last + m_prev - m_new)
        vw = (v_t.astype(F32) * ws).astype(BF16)
        ws_blk = jnp.broadcast_to(ws, (PAD, L)).astype(BF16)
        upd = _dot(jnp.concatenate([vw, ws_blk], axis=0), k)
        c_new = decay * ct_prev + upd[:HD]
        n_new = decay * n_prev + upd[HD:HD + 1]
        return dict(d=d, hd=hd, j=j, r=r, w=w, w_inter=w_inter, m_t=m_t, v_t=v_t, c_new=c_new, n_new=n_new, m_new=m_new)

    def second_stage(s):
        h_ref = dirs[s["d"]][4]
        sl = slice(s["hd"] * HD, (s["hd"] + 1) * HD)
        r = s["r"]
        qk = r[:L] * s["w"]
        num = _dot(s["v_t"], qk.astype(BF16)) + s["w_inter"] * r[L:L + HD]
        den = jnp.sum(qk, axis=0, keepdims=True) + s["w_inter"] * r[L + HD:L + HD + 1]
        h_ref[:, sl] = (num / jnp.maximum(jnp.abs(den), jnp.exp(-s["m_t"]))).T

    staged = []
    for idx, (d, hd) in enumerate(order):
        staged.append(first_stage(d, hd, query_stage(d, hd)))
        if idx >= MLSTM_STAGE_LAG:
            second_stage(staged[idx - MLSTM_STAGE_LAG])
    for s in staged[len(order) - MLSTM_STAGE_LAG:]:
        second_stage(s)
    for s in staged:
        c_ref[s["d"], s["hd"]] = s["c_new"]
        m_row = jnp.where(m_lane == s["j"], s["m_new"], m_row)
    n_ref[...] = jnp.concatenate([s["n_new"] for s in staged], axis=0)
    m_ref[...] = m_row

    @pl.when(chunk == pl.num_programs(1) - 1)
    def _finish():
        for d in range(N_DIR):
            for hd in range(H):
                c_ref[d, hd] = c_ref[d, hd].T


def _mlstm_call(mq_t, mk, mv_t, gates_t, tri_t, state0):
    B, T, _ = mk.shape
    L = MLSTM_CHUNK
    nc = T // L
    zero_init = state0 is None
    fwd = lambda w: pl.BlockSpec((None, L, w), lambda b, c: (b, c, 0))
    bwd = lambda w: pl.BlockSpec((None, L, w), lambda b, c: (b, nc - 1 - c, 0))
    fwd_t = lambda w: pl.BlockSpec((None, w, L), lambda b, c: (b, 0, c))
    bwd_t = lambda w: pl.BlockSpec((None, w, L), lambda b, c: (b, 0, nc - 1 - c))
    W = MLSTM_WIDTH
    NG = 2 * N_DIR * MLSTM_HEADS
    c_spec = pl.BlockSpec((None, N_DIR, MLSTM_HEADS, MLSTM_HD, MLSTM_HD), lambda b, c: (b, 0, 0, 0, 0))
    n_spec = pl.BlockSpec((None, N_DIR * MLSTM_HEADS, MLSTM_HD), lambda b, c: (b, 0, 0))
    m_spec = pl.BlockSpec((None, 1, LANES), lambda b, c: (b, 0, 0))
    ins = [mq_t, mk, mv_t, gates_t, mq_t, mk, mv_t, gates_t, tri_t]
    in_specs = [fwd_t(W), fwd(W), fwd_t(W), fwd_t(NG), bwd_t(W), bwd(W), bwd_t(W), bwd_t(NG), _const_spec(tri_t.shape)]
    if not zero_init:
        ins += list(state0)
        in_specs += [c_spec, n_spec, m_spec]
    return pl.pallas_call(
        functools.partial(_mlstm_kernel, zero_init=zero_init),
        grid=(B, nc),
        in_specs=in_specs,
        out_specs=[fwd(W), bwd(W), c_spec, n_spec, m_spec],
        out_shape=[jax.ShapeDtypeStruct((B, T, W), F32), jax.ShapeDtypeStruct((B, T, W), F32),
                   jax.ShapeDtypeStruct((B, N_DIR, MLSTM_HEADS, MLSTM_HD, MLSTM_HD), F32),
                   jax.ShapeDtypeStruct((B, N_DIR * MLSTM_HEADS, MLSTM_HD), F32),
                   jax.ShapeDtypeStruct((B, 1, LANES), F32)],
        compiler_params=_cparams("parallel", "arbitrary"),
        name="mlstm",
    )(*ins)


def _residual_ffn(x_ref, mix_fn, mod_ref, gpost_ref, gfpre_ref, gfpost_ref, wout_ref, w1_ref, w2_ref, o_ref):
    half = D_MODEL // 2
    sub = min(POST_SUBTILE, x_ref.shape[0])
    rows = [slice(r0, r0 + sub) for r0 in range(0, x_ref.shape[0], sub)]
    ys = []
    for r in rows:
        mix_a, mix_b = mix_fn(r)
        ys.append(_dot(mix_a.astype(BF16), wout_ref[:half, :]) + _dot(mix_b.astype(BF16), wout_ref[half:, :]))
    for r, y in zip(rows, ys):
        x1 = x_ref[r, :] + mod_ref[2] * _rms(y, gpost_ref[...])
        h2 = (_rms(x1, gfpre_ref[...]) * (1.0 + mod_ref[4]) + mod_ref[3]).astype(BF16)
        chunks = list(range(0, D_FF, FF_CHUNK))
        up = _dot(h2, w1_ref[:, chunks[0]:chunks[0] + FF_CHUNK])
        f = None
        for i, c0 in enumerate(chunks):
            cur = up
            if i + 1 < len(chunks):
                up = _dot(h2, w1_ref[:, chunks[i + 1]:chunks[i + 1] + FF_CHUNK])
            part = _dot(jnp.square(jnp.maximum(cur, 0.0)).astype(BF16), w2_ref[c0:c0 + FF_CHUNK, :])
            f = part if f is None else f + part
        o_ref[r, :] = x1 + mod_ref[5] * _rms(f, gfpost_ref[...])


def _head_rms(y, gain_ref, width):
    outs = []
    for h0 in range(0, y.shape[1], width):
        seg = y[:, h0:h0 + width]
        outs.append(seg * lax.rsqrt(jnp.mean(seg * seg, axis=-1, keepdims=True) + EPS) * gain_ref[:, h0:h0 + width])
    return jnp.concatenate(outs, axis=1)


def _post_even_kernel(x_ref, attn_ref, hf_ref, hb_ref, mo_ref, mod_ref, gm_ref, gpost_ref, gfpre_ref, gfpost_ref,
                      wout_ref, w1_ref, w2_ref, o_ref):
    def mix(r):
        hm = _head_rms(hf_ref[r, :] + hb_ref[r, :], gm_ref, MLSTM_HD) * jax.nn.sigmoid(mo_ref[r, :])
        return attn_ref[r, :], hm

    _residual_ffn(x_ref, mix, mod_ref, gpost_ref, gfpre_ref, gfpost_ref, wout_ref, w1_ref, w2_ref, o_ref)


def _post_odd_kernel(x_ref, lru_ref, rg_ref, of_ref, ob_ref, gg_ref, mod_ref, gm_ref, gpost_ref, gfpre_ref, gfpost_ref,
                     wout_ref, w1_ref, w2_ref, o_ref):
    def mix(r):
        lru = jax.nn.gelu(rg_ref[r, :]) * lru_ref[r, :]
        og = _head_rms(of_ref[r, :] + ob_ref[r, :], gm_ref, GLA_DV) * jax.nn.silu(gg_ref[r, :])
        return lru, og

    _residual_ffn(x_ref, mix, mod_ref, gpost_ref, gfpre_ref, gfpost_ref, wout_ref, w1_ref, w2_ref, o_ref)


def _post_call(kernel_fn, name, x, mix_ins, mod, mod_row0, mod_stride, consts):
    shape = x.shape
    if mod_stride == 0:
        x = x.reshape(1, -1, D_MODEL)
        mix_ins = [a.reshape(1, -1, a.shape[2]) for a in mix_ins]
    B, T, _ = x.shape
    tm = min(POST_TILE, T)
    ins = [x] + list(mix_ins) + [mod] + list(consts)
    in_specs = ([_tok_spec(tm, D_MODEL)] + [_tok_spec(tm, a.shape[2]) for a in mix_ins] + [_mod_spec(mod_row0, mod_stride)]
                + [_const_spec(a.shape) for a in consts])
    return pl.pallas_call(
        kernel_fn,
        grid=(B, T // tm),
        in_specs=in_specs,
        out_specs=_tok_spec(tm, D_MODEL),
        out_shape=jax.ShapeDtypeStruct(x.shape, F32),
        compiler_params=_cparams("parallel", "parallel"),
        name=name,
    )(*ins).reshape(shape)


def _inproj_odd_kernel(x_ref, xp_ref, xn_ref, mod_ref, g_ref, wmain_ref, wlr_in_ref, wlr_hi_ref, wlr_lo_ref, blr_ref,
                       cw_ref, cb_ref, wg_ref, bg_ref, lam_ref,
                       rg_out, gq_out, gk_out, gv_out, gg_out, la_out, af_out, uf_out, ab_out, ub_out):
    tm = x_ref.shape[0]
    W = LRU_WIDTH
    sub = min(ODD_GROUP_ROWS, tm)
    n_groups = tm // sub
    m = sub + 2 * CONV_HALO
    first = pl.program_id(1) == 0
    last = pl.program_id(1) == pl.num_programs(1) - 1
    rid = lax.broadcasted_iota(jnp.int32, (m, W), 0)
    neg_half_c_sp = (-0.5 * LRU_C) * _softplus(-lam_ref[...])

    def modulated(xv):
        return (_rms(xv, g_ref[...]) * (1.0 + mod_ref[1]) + mod_ref[0]).astype(BF16)

    for q in range(n_groups):
        rows = slice(q * sub, (q + 1) * sub)
        prev = xp_ref[...] if q == 0 else x_ref[q * sub - CONV_HALO:q * sub, :]
        nxt = xn_ref[...] if q == n_groups - 1 else x_ref[(q + 1) * sub:(q + 1) * sub + CONV_HALO, :]
        hb_ext = modulated(jnp.concatenate([prev, x_ref[rows, :], nxt], axis=0))
        hb = hb_ext[CONV_HALO:CONV_HALO + sub]

        rx = _dot(hb_ext, wmain_ref[:, :W])
        outside = None
        if q == 0:
            outside = first & (rid < CONV_HALO)
        if q == n_groups - 1:
            tail = last & (rid >= sub + CONV_HALO)
            outside = tail if outside is None else outside | tail
        if outside is not None:
            rx = jnp.where(outside, 0.0, rx)

        glr_hi, glr_lo = _split2(_dot(hb, wlr_in_ref[...]))
        z = _dot(glr_hi, wlr_hi_ref[...]) + (_dot(glr_hi, wlr_lo_ref[...]) + _dot(glr_lo, wlr_hi_ref[...])) + blr_ref[...]
        la_out[rows, :] = _log_sigmoid(z) * (1.0 / GLA_TAU)

        mm = _dot(hb, wmain_ref[:, W:])
        o = 0
        rg_out[rows, :] = mm[:, o:o + LRU_WIDTH]; o += LRU_WIDTH
        gq_out[rows, :] = mm[:, o:o + GLA_KW] * GLA_DK ** -0.5; o += GLA_KW
        gk_out[rows, :] = mm[:, o:o + GLA_KW]; o += GLA_KW
        gv_out[rows, :] = mm[:, o:o + GLA_WIDTH].astype(BF16); o += GLA_WIDTH
        gg_out[rows, :] = mm[:, o:o + GLA_WIDTH]

        xc = cb_ref[...] + rx * cw_ref[CONV_LEFT:CONV_LEFT + 1, :]
        for tap in range(CONV_W):
            off = tap - CONV_LEFT
            if off != 0:
                xc = xc + pltpu.roll(rx, (-off) % m, axis=0) * cw_ref[tap:tap + 1, :]
        xc = xc[CONV_HALO:CONV_HALO + sub, :]
        th = jnp.tanh(_dot(xc.astype(BF16), wg_ref[...]) + bg_ref[...])
        half_xc = 0.5 * xc
        for d, (a_out, u_out) in enumerate(((af_out, uf_out), (ab_out, ub_out))):
            th_r = th[:, 2 * d * W:(2 * d + 1) * W]
            th_i = th[:, (2 * d + 1) * W:(2 * d + 2) * W]
            log_a = neg_half_c_sp[d:d + 1, :] * th_r + neg_half_c_sp[d:d + 1, :]
            a = jnp.exp(log_a)
            a_out[rows, :] = a
            v = jnp.tanh(log_a) * (-1.0 - a * a)
            u_out[rows, :] = jnp.where(v > 0.0, v * lax.rsqrt(v), 0.0) * (half_xc * th_i + half_xc)


def _inproj_odd_call(x, mod, mod_row0, mod_stride, prm, lru_prm):
    B, T, _ = x.shape
    tm = min(TOKEN_TILE, T)
    halo_per_tile = tm // CONV_HALO
    n_halo = T // CONV_HALO
    prev_spec = pl.BlockSpec((None, CONV_HALO, D_MODEL), lambda b, i: (b, jnp.maximum(i * halo_per_tile - 1, 0), 0))
    next_spec = pl.BlockSpec((None, CONV_HALO, D_MODEL),
                             lambda b, i: (b, jnp.minimum((i + 1) * halo_per_tile, n_halo - 1), 0))
    consts = [prm["g_mix_pre"], prm["w_main"], prm["w_lr_in"], prm["w_lr_hi"], prm["w_lr_lo"], prm["b_lr"],
              lru_prm["conv_w"], lru_prm["conv_b"], lru_prm["w_gate"], lru_prm["b_gate"], lru_prm["lam"]]
    ins = [x, x, x, mod] + consts
    in_specs = ([_tok_spec(tm, D_MODEL), prev_spec, next_spec, _mod_spec(mod_row0, mod_stride)]
                + [_const_spec(a.shape) for a in consts])
    widths = [(LRU_WIDTH, F32), (GLA_KW, F32), (GLA_KW, F32), (GLA_WIDTH, BF16), (GLA_WIDTH, F32),
              (N_DIR * GLA_KW, F32)] + [(LRU_WIDTH, F32)] * 4
    return pl.pallas_call(
        _inproj_odd_kernel,
        grid=(B, T // tm),
        in_specs=in_specs,
        out_specs=[_tok_spec(tm, w) for w, _ in widths],
        out_shape=[jax.ShapeDtypeStruct((B, T, w), dt) for w, dt in widths],
        compiler_params=_cparams("parallel", "parallel"),
        name="inproj_odd",
    )(*ins)


def _lru_kernel(*refs, zero_init):
    if zero_init:
        af_s, uf_s, ab_s, ub_s, o_ref, last_ref, hb_s = refs
    else:
        af_s, uf_s, ab_s, ub_s, h0_ref, o_ref, last_ref, hb_s = refs
    T, W = af_s.shape
    nblk = T // SUBLANES
    rid8 = lax.broadcasted_iota(jnp.int32, (SUBLANES, W), 0)

    def block_scan(a, u, reverse):
        for dist in (1, 2, 4):
            if reverse:
                keep = rid8 < SUBLANES - dist
                shift = SUBLANES - dist
            else:
                keep = rid8 >= dist
                shift = dist
            a_n = jnp.where(keep, pltpu.roll(a, shift, axis=0), 1.0)
            u_n = jnp.where(keep, pltpu.roll(u, shift, axis=0), 0.0)
            u = a * u_n + u
            a = a * a_n
        return a, u

    def body(blk, carry):
        hf, hb = carry
        rf = pl.multiple_of(blk * SUBLANES, SUBLANES)
        a, u = block_scan(af_s[pl.ds(rf, SUBLANES), :], uf_s[pl.ds(rf, SUBLANES), :], False)
        out_f = u + a * hf
        o_ref[pl.ds(rf, SUBLANES), :] = out_f
        rb = pl.multiple_of((nblk - 1 - blk) * SUBLANES, SUBLANES)
        a, u = block_scan(ab_s[pl.ds(rb, SUBLANES), :], ub_s[pl.ds(rb, SUBLANES), :], True)
        out_b = u + a * hb
        hb_s[pl.ds(rb, SUBLANES), :] = out_b
        return out_f[SUBLANES - 1:SUBLANES, :], out_b[0:1, :]

    if zero_init:
        init = (jnp.zeros((1, W), F32), jnp.zeros((1, W), F32))
    else:
        init = (h0_ref[0:1, :], h0_ref[1:2, :])
    hf, hb = lax.fori_loop(0, nblk, body, init)
    last_ref[0:1, :] = hf
    last_ref[1:2, :] = hb
    o_ref[...] = o_ref[...] + hb_s[...]


def _lru_call(coeffs, h0):
    B, T, W = coeffs[0].shape
    zero_init = h0 is None
    seq = pl.BlockSpec((None, T, W), lambda b: (b, 0, 0))
    st = pl.BlockSpec((None, N_DIR, W), lambda b: (b, 0, 0))
    ins = list(coeffs)
    in_specs = [seq] * len(ins)
    if not zero_init:
        ins.append(h0)
        in_specs.append(st)
    return pl.pallas_call(
        functools.partial(_lru_kernel, zero_init=zero_init),
        grid=(B,),
        in_specs=in_specs,
        out_specs=[seq, st],
        out_shape=[jax.ShapeDtypeStruct((B, T, W), F32), jax.ShapeDtypeStruct((B, N_DIR, W), F32)],
        scratch_shapes=[pltpu.VMEM((T, W), F32)],
        compiler_params=_cparams("parallel"),
        name="rglru",
    )(*ins)


GLA_LEVELS = 6


def _gla_constants():
    L = GLA_CHUNK
    idx = np.arange(L)
    sel = np.zeros((N_DIR, (GLA_LEVELS + 3) * L, L), np.float32)
    lvl_mask = np.zeros((N_DIR, GLA_LEVELS + 1, L, L), np.float32)
    for d in range(N_DIR):
        tri = (idx[None, :] <= idx[:, None]) if d == 0 else (idx[None, :] >= idx[:, None])
        tri = tri.astype(np.float32)
        last = L - 1 if d == 0 else 0
        sel[d, 0:L] = tri
        for lv in range(GLA_LEVELS):
            m = L >> (lv + 1)
            start = (idx // (2 * m)) * (2 * m)
            second = (idx - start) >= m
            boundary = start + (m - 1 if d == 0 else m)
            query_role = second if d == 0 else ~second
            diff = tri - tri[boundary]
            sel[d, (1 + lv) * L:(2 + lv) * L] = np.where(query_role[:, None], diff, -diff)
            same = start[:, None] == start[None, :]
            lvl_mask[d, lv] = same & query_role[:, None] & ~query_role[None, :]
        lvl_mask[d, GLA_LEVELS] = np.eye(L)
        sel[d, (GLA_LEVELS + 1) * L:(GLA_LEVELS + 2) * L] = tri[last][None, :] - tri
        sel[d, (GLA_LEVELS + 2) * L:] = tri[last][None, :]
    assert sel.min() >= 0.0 and sel.max() <= 1.0
    sel3 = np.concatenate([sel] * 3, axis=2)
    lvl_mask = np.tile(lvl_mask, (1, 1, 1, GLA_HEADS))
    return jnp.asarray(sel3, BF16), jnp.asarray(lvl_mask, F32)


def _gla_kernel(*refs, zero_init):
    if zero_init:
        (qf_ref, kf_ref, vf_ref, af_ref, qb_ref, kb_ref, vb_ref, ab_ref, sel_ref, msk_ref, of_ref, ob_ref, s_ref) = refs
    else:
        (qf_ref, kf_ref, vf_ref, af_ref, qb_ref, kb_ref, vb_ref, ab_ref, sel_ref, msk_ref, s0_ref,
         of_ref, ob_ref, s_ref) = refs
    L = GLA_CHUNK
    H = GLA_HEADS
    KW = GLA_KW
    VW = GLA_WIDTH
    DK = GLA_DK
    DV = GLA_DV

    @pl.when(pl.program_id(1) == 0)
    def _init():
        if zero_init:
            s_ref[...] = jnp.zeros(s_ref.shape, F32)
        else:
            s_ref[...] = s0_ref[...]

    zero_k = jnp.zeros((L, KW), BF16)
    zero_v = jnp.zeros((L, VW), BF16)
    lane_k = lax.broadcasted_iota(jnp.int32, (L, KW), 1) // DK
    lane_v = lax.broadcasted_iota(jnp.int32, (L, VW), 1) // DV

    def stack_heads(a, lane_head, zero):
        return jnp.concatenate([jnp.where(lane_head == h, a, zero) for h in range(H)], axis=0)

    def tile_rows(a):
        return jnp.concatenate([a] * H, axis=0)

    dirs = ((qf_ref, kf_ref, vf_ref, af_ref, of_ref), (qb_ref, kb_ref, vb_ref, ab_ref, ob_ref))
    work = []
    for i in range(GLA_STEP_CHUNKS):
        for d, (q_ref, k_ref, v_ref, a_ref, o_ref) in enumerate(dirs):
            j = i if d == 0 else GLA_STEP_CHUNKS - 1 - i
            rows = slice(j * L, (j + 1) * L)
            sums = _dot(sel_ref[d], jnp.concatenate(_split3(a_ref[rows, :]), axis=0))
            q_b = q_ref[rows, :].astype(BF16)
            work.append(dict(d=d, rows=rows, sums=sums, q_b=q_b, q_st=stack_heads(q_b, lane_k, zero_k),
                             k_st=stack_heads(k_ref[rows, :].astype(BF16), lane_k, zero_k)))

    for lv in range(GLA_LEVELS + 1):
        for wk in work:
            if lv == GLA_LEVELS:
                part = _dot_nt(wk["q_b"], wk["k_st"])
            else:
                wl = jnp.exp(wk["sums"][(1 + lv) * L:(2 + lv) * L]).astype(BF16)
                part = _dot_nt(wk["q_b"] * wl, wk["k_st"] * tile_rows(wl))
            part = part * msk_ref[wk["d"], lv]
            wk["scores"] = part if lv == 0 else wk["scores"] + part

    state = [s_ref[d] for d in range(N_DIR)]
    for wk in work:
        d, rows, sums = wk["d"], wk["rows"], wk["sums"]
        _, k_ref, v_ref, _, o_ref = dirs[d]
        v = v_ref[rows, :]
        s_prev = state[d]
        intra = _dot(wk["scores"].astype(BF16), stack_heads(v, lane_v, zero_v))
        q_in = wk["q_st"] * tile_rows(jnp.exp(sums[0:L]).astype(BF16))
        inter = _dot(q_in, s_prev.astype(BF16))
        o_ref[rows, :] = intra + jnp.concatenate([inter[h * L:(h + 1) * L] for h in range(H)], axis=1)

        k_out = k_ref[rows, :] * jnp.exp(sums[(GLA_LEVELS + 1) * L:(GLA_LEVELS + 2) * L])
        dec = jnp.exp(sums[(GLA_LEVELS + 2) * L:])
        both_t = jnp.concatenate([k_out, dec], axis=0).T
        k_out_t = both_t[:, :L].astype(BF16)
        upd = [_dot(k_out_t[h * DK:(h + 1) * DK], v[:, h * DV:(h + 1) * DV]) for h in range(H)]
        state[d] = both_t[:, L:L + 1] * s_prev + jnp.concatenate(upd, axis=0)
    for d in range(N_DIR):
        s_ref[d] = state[d]


def _gla_call(gq, gk, gv, log_a, sel3, lvl_mask, s0):
    B, T, _ = gq.shape
    L = GLA_CHUNK * GLA_STEP_CHUNKS
    nc = T // L
    zero_init = s0 is None
    fwd = lambda w: pl.BlockSpec((None, L, w), lambda b, c: (b, c, 0))
    bwd = lambda w: pl.BlockSpec((None, L, w), lambda b, c: (b, nc - 1 - c, 0))
    a_fwd = pl.BlockSpec((None, L, GLA_KW), lambda b, c: (b, c, 0))
    a_bwd = pl.BlockSpec((None, L, GLA_KW), lambda b, c: (b, nc - 1 - c, 1))
    s_spec = pl.BlockSpec((None, N_DIR, GLA_KW, GLA_DV), lambda b, c: (b, 0, 0, 0))
    ins = [gq, gk, gv, log_a, gq, gk, gv, log_a, sel3, lvl_mask]
    in_specs = [fwd(GLA_KW), fwd(GLA_KW), fwd(GLA_WIDTH), a_fwd, bwd(GLA_KW), bwd(GLA_KW), bwd(GLA_WIDTH), a_bwd,
                _const_spec(sel3.shape), _const_spec(lvl_mask.shape)]
    if not zero_init:
        ins.append(s0)
        in_specs.append(s_spec)
    return pl.pallas_call(
        functools.partial(_gla_kernel, zero_init=zero_init),
        grid=(B, nc),
        in_specs=in_specs,
        out_specs=[fwd(GLA_WIDTH), bwd(GLA_WIDTH), s_spec],
        out_shape=[jax.ShapeDtypeStruct((B, T, GLA_WIDTH), F32), jax.ShapeDtypeStruct((B, T, GLA_WIDTH), F32),
                   jax.ShapeDtypeStruct((B, N_DIR, GLA_KW, GLA_DV), F32)],
        compiler_params=_cparams("parallel", "arbitrary"),
        name="gla",
    )(*ins)


def _block_diag(blocks):
    n, r, c = blocks.shape
    eye = jnp.eye(n, dtype=blocks.dtype)
    return (eye[:, None, :, None] * blocks[:, :, None, :]).reshape(n * r, n * c)


def _pad_cols(a, width):
    return jnp.pad(a, ((0, 0), (0, width - a.shape[1])))


def _rope_tables(n_tokens):
    rows = n_tokens // GRID_W
    row = jnp.repeat(jnp.arange(rows), GRID_W).astype(F32)
    col = jnp.tile(jnp.arange(GRID_W), rows).astype(F32)
    inv = jnp.power(ROPE_THETA, -jnp.arange(ROPE_PAIRS_PER_AXIS, dtype=F32) / ROPE_PAIRS_PER_AXIS)
    ang = jnp.concatenate([row[:, None] * inv, col[:, None] * inv], axis=-1)
    cos, sin = jnp.cos(ang), jnp.sin(ang)
    reps = LANES // HEAD_DIM
    return jnp.tile(jnp.concatenate([cos, cos], axis=-1), (1, reps)), jnp.tile(jnp.concatenate([-sin, sin], axis=-1), (1, reps))


def _cache_variants(cache):
    z = jnp.zeros_like(cache[:, 0])
    return jnp.concatenate([cache[:, 0], z, z, cache[:, 0], cache[:, 1], z, z, cache[:, 1]], axis=-1).astype(BF16)


def _mlstm_tri():
    idx = np.arange(MLSTM_CHUNK)
    upper = idx[:, None] <= idx[None, :]
    tri_t = np.stack([upper, upper.T]).astype(np.float32)
    return jnp.asarray(np.concatenate([tri_t] * 3, axis=1), BF16)


def kernel(x_prompt, x_sample, cache_attn_k, cache_attn_v, state_mlstm_C, state_mlstm_n, state_mlstm_m, state_lru_h, state_gla_S, c, c_ctx, ada_w, ada_b, norm_mix_pre, norm_mix_post, norm_ffn_pre, norm_ffn_post, w_out, ffn_w1, ffn_w2, w_in_even, attn_q_norm, attn_k_norm, mlstm_i_bias, mlstm_f_bias, mlstm_norm, w_in_odd, lru_conv_w, lru_conv_b, lru_w_r, lru_b_r, lru_w_i, lru_b_i, lru_lambda, gla_w_lr, gla_b_lr, gla_norm):
    depth = ada_w.shape[0]
    Bp = x_prompt.shape[0]
    Bs = x_sample.shape[0]
    row = lambda a: a.reshape(1, -1)

    n_rows = -(-(1 + Bs) // SUBLANES) * SUBLANES
    cvec = jnp.zeros((n_rows, D_MODEL), F32).at[0].set(c_ctx).at[1:1 + Bs].set(c)
    mod_all = _ada_call(cvec, ada_w, ada_b).reshape(depth, n_rows, 6, 1, D_MODEL)

    rope_tabs = _rope_tables(x_sample.shape[1])
    ones64 = _block_diag(jnp.ones((ATT_HEADS, HEAD_DIM, HEAD_DIM), BF16))
    tri_t = _mlstm_tri()
    sel3, lvl_mask = _gla_constants()

    xp, xs = x_prompt, x_sample
    outs = {}
    for l in range(depth):
        mod = mod_all[l]
        tail = [row(norm_mix_post[l]), row(norm_ffn_pre[l]), row(norm_ffn_post[l]),
                w_out[l].astype(BF16), ffn_w1[l].astype(BF16), ffn_w2[l].astype(BF16)]
        if l % 2 == 0:
            e = l // 2
            w_in = w_in_even[e]
            o1 = ATT_WIDTH + 2 * ATT_KV_WIDTH
            o2 = o1 + 4 * MLSTM_WIDTH
            prm = {
                "g_mix_pre": row(norm_mix_pre[l]),
                "w_qkv": w_in[:, :o1].astype(BF16),
                "w_m": jnp.concatenate([w_in[:, o1 + MLSTM_WIDTH:o1 + 2 * MLSTM_WIDTH], w_in[:, o1 + 3 * MLSTM_WIDTH:o2]],
                                       axis=1).astype(BF16),
                "w_mt": jnp.concatenate([w_in[:, o1:o1 + MLSTM_WIDTH], w_in[:, o1 + 2 * MLSTM_WIDTH:o1 + 3 * MLSTM_WIDTH]],
                                        axis=1).T.astype(BF16),
                "w_gt": w_in[:, o2:].T.astype(BF16),
                "gate_bias_t": jnp.concatenate([mlstm_i_bias[e].reshape(-1), mlstm_f_bias[e].reshape(-1)]).reshape(-1, 1),
                "q_gain": row(jnp.tile(attn_q_norm[e], ATT_HEADS)),
                "k_gain": row(jnp.tile(attn_k_norm[e], ATT_KV_HEADS)),
                "ones64": ones64,
            }
            consts = [row(mlstm_norm[e])] + tail
            for path in ("prompt", "sample"):
                if path == "prompt":
                    x, r0, rs, tabs, cache, st0 = xp, 0, 0, None, None, None
                else:
                    x, r0, rs, tabs = xs, 1, 1, rope_tabs
                    cache = (_cache_variants(cache_attn_k[:, e]), _cache_variants(cache_attn_v[:, e]))
                    st0 = (state_mlstm_C[:, e], state_mlstm_n[:, e].reshape(Bs, N_DIR * MLSTM_HEADS, MLSTM_HD),
                           _pad_cols(state_mlstm_m[:, e].reshape(Bs, -1), LANES).reshape(Bs, 1, LANES))
                q, kk, vv, k_n, v_n, mq_t, mk, mv_t, mo, gates_t = _inproj_even_call(x, mod, r0, rs, prm, tabs)
                attn = _attn_call(q, kk, vv, cache)
                hf, hb, c_fin, n_fin, m_fin = _mlstm_call(mq_t, mk, mv_t, gates_t, tri_t, st0)
                x_new = _post_call(_post_even_kernel, "post_even", x, [attn, hf, hb, mo], mod, r0, rs, consts)
                if path == "prompt":
                    xp = x_new
                    T = x.shape[1]
                    outs.setdefault("k", []).append(jnp.swapaxes(k_n.reshape(Bp, T, ATT_KV_HEADS, HEAD_DIM), 1, 2))
                    outs.setdefault("v", []).append(jnp.swapaxes(v_n.reshape(Bp, T, ATT_KV_HEADS, HEAD_DIM), 1, 2))
                    outs.setdefault("C", []).append(c_fin)
                    outs.setdefault("n", []).append(n_fin.reshape(Bp, N_DIR, MLSTM_HEADS, MLSTM_HD))
                    outs.setdefault("m", []).append(m_fin[:, 0, :N_DIR * MLSTM_HEADS].reshape(Bp, N_DIR, MLSTM_HEADS))
                else:
                    xs = x_new
        else:
            o = l // 2
            w_in = w_in_odd[o]
            o1 = 2 * LRU_WIDTH + 2 * GLA_KW + 2 * GLA_WIDTH
            w_lr = _block_diag(gla_w_lr[o])
            w_lr = jnp.pad(w_lr, ((0, LANES - w_lr.shape[0]), (0, 0)))
            w_lr_hi = w_lr.astype(BF16)
            prm = {
                "g_mix_pre": row(norm_mix_pre[l]),
                "w_main": w_in[:, :o1].astype(BF16),
                "w_lr_in": _pad_cols(w_in[:, o1:], LANES).astype(BF16),
                "w_lr_hi": w_lr_hi,
                "w_lr_lo": (w_lr - w_lr_hi.astype(F32)).astype(BF16),
                "b_lr": row(gla_b_lr[o]),
            }
            lru_prm = {
                "conv_w": lru_conv_w[o],
                "conv_b": row(lru_conv_b[o]),
                "w_gate": (0.5 * jnp.concatenate([_block_diag(lru_w_r[o, 0]), _block_diag(lru_w_i[o, 0]),
                                                  _block_diag(lru_w_r[o, 1]), _block_diag(lru_w_i[o, 1])], axis=1)).astype(BF16),
                "b_gate": 0.5 * row(jnp.stack([lru_b_r[o, 0], lru_b_i[o, 0], lru_b_r[o, 1], lru_b_i[o, 1]])),
                "lam": lru_lambda[o],
            }
            consts = [row(gla_norm[o])] + tail
            for path in ("prompt", "sample"):
                if path == "prompt":
                    x, r0, rs, h0, s0 = xp, 0, 0, None, None
                else:
                    x, r0, rs, h0 = xs, 1, 1, state_lru_h[:, o]
                    s0 = state_gla_S[:, o].reshape(Bs, N_DIR, GLA_KW, GLA_DV)
                rg, gq, gk, gv, gg, log_a, *lru_coeffs = _inproj_odd_call(x, mod, r0, rs, prm, lru_prm)
                lru, h_last = _lru_call(lru_coeffs, h0)
                of, ob, s_fin = _gla_call(gq, gk, gv, log_a, sel3, lvl_mask, s0)
                x_new = _post_call(_post_odd_kernel, "post_odd", x, [lru, rg, of, ob, gg], mod, r0, rs, consts)
                if path == "prompt":
                    xp = x_new
                    outs.setdefault("h", []).append(h_last)
                    outs.setdefault("S", []).append(s_fin.reshape(Bp, N_DIR, GLA_HEADS, GLA_DK, GLA_DV))
                else:
                    xs = x_new

    stack = lambda name: jnp.stack(outs[name], axis=1)
    return (xp, xs, stack("k"), stack("v"), stack("C"), stack("n"), stack("m"), stack("h"), stack("S"))
```

```python
import functools

import numpy as np
import jax
import jax.numpy as jnp
from jax import lax
from jax.experimental import pallas as pl
from jax.experimental.pallas import tpu as pltpu

F32 = jnp.float32
BF16 = jnp.bfloat16

D_MODEL = 1024
D_FF = 4 * D_MODEL
GRID_W = 64
EPS = 1e-6
LOG2_E = 1.4426950408889634
N_DIR = 2

ATT_HEADS = 8
ATT_KV_HEADS = 2
HEAD_DIM = 64
ATT_WIDTH = ATT_HEADS * HEAD_DIM
ATT_KV_WIDTH = ATT_KV_HEADS * HEAD_DIM
ROPE_THETA = 10000.0
ROPE_PAIRS_PER_AXIS = HEAD_DIM // 4

MLSTM_HEADS = 4
MLSTM_HD = 128
MLSTM_WIDTH = MLSTM_HEADS * MLSTM_HD
MLSTM_CHUNK = 256
MLSTM_STAGE_LAG = 2
M_INIT = -1e30

LRU_WIDTH = 512
LRU_BLOCKS = 8
LRU_BD = LRU_WIDTH // LRU_BLOCKS
LRU_C = 8.0
CONV_W = 4
CONV_LEFT = 2

GLA_HEADS = 4
GLA_DK = 64
GLA_DV = 128
GLA_KW = GLA_HEADS * GLA_DK
GLA_WIDTH = GLA_HEADS * GLA_DV
GLA_RANK = 16
GLA_TAU = 16.0
GLA_CHUNK = 64
GLA_STEP_CHUNKS = 4

LANES = 128
SUBLANES = 8
VMEM_LIMIT = 56 * 1024 * 1024

TOKEN_TILE = 512
ATTN_Q_TILE = 256
ATTN_SCORES_AHEAD = 1
FF_CHUNK = 1024
POST_TILE = 512
POST_SUBTILE = 256
ODD_GROUP_ROWS = 256
CONV_HALO = 2 * SUBLANES


def _cparams(*sem):
    return pltpu.CompilerParams(dimension_semantics=sem, vmem_limit_bytes=VMEM_LIMIT)


def _const_spec(shape):
    n = len(shape)
    return pl.BlockSpec(shape, lambda *_: (0,) * n, pipeline_mode=pl.Buffered(1))


def _dot(a, b):
    return jnp.dot(a, b, preferred_element_type=F32)


def _dot_nt(a, b):
    return lax.dot_general(a, b, (((1,), (1,)), ((), ())), preferred_element_type=F32)


def _dot_tn(a, b):
    return lax.dot_general(a, b, (((0,), (0,)), ((), ())), preferred_element_type=F32)


def _split3(x):
    hi = x.astype(BF16)
    r = x - hi.astype(F32)
    mid = r.astype(BF16)
    lo = (r - mid.astype(F32)).astype(BF16)
    return hi, mid, lo


def _split2(x):
    hi = x.astype(BF16)
    lo = (x - hi.astype(F32)).astype(BF16)
    return hi, lo


def _rms(x, g):
    return x * lax.rsqrt(jnp.mean(x * x, axis=-1, keepdims=True) + EPS) * g


def _log_sigmoid(x):
    return jnp.minimum(x, 0.0) - jnp.log1p(jnp.exp(-jnp.abs(x)))


def _softplus(x):
    return jnp.maximum(x, 0.0) + jnp.log1p(jnp.exp(-jnp.abs(x)))


def _group_rms(y, gain, ones_bd, width):
    hi, lo = _split2(y * y)
    ss = _dot(hi, ones_bd) + _dot(lo, ones_bd)
    return y * lax.rsqrt(ss * (1.0 / width) + EPS) * gain


def _ada_kernel(c_ref, w_ref, b_ref, o_ref):
    s = jax.nn.silu(c_ref[...])
    o_ref[...] = _dot(s.astype(BF16), w_ref[...].astype(BF16)) + b_ref[...]


def _ada_call(cvec, ada_w, ada_b):
    depth = ada_w.shape[0]
    rows = cvec.shape[0]
    n_col = ada_w.shape[2] // D_MODEL
    return pl.pallas_call(
        _ada_kernel,
        grid=(depth, n_col),
        in_specs=[
            pl.BlockSpec((rows, D_MODEL), lambda l, j: (0, 0)),
            pl.BlockSpec((None, D_MODEL, D_MODEL), lambda l, j: (l, 0, j)),
            pl.BlockSpec((None, 1, D_MODEL), lambda l, j: (l, 0, j)),
        ],
        out_specs=pl.BlockSpec((None, rows, D_MODEL), lambda l, j: (l, 0, j)),
        out_shape=jax.ShapeDtypeStruct((depth, rows, ada_w.shape[2]), F32),
        compiler_params=_cparams("arbitrary", "arbitrary"),
        name="ada_mod",
    )(cvec, ada_w, ada_b.reshape(depth, 1, -1))


def _mod_spec(row0, row_stride):
    return pl.BlockSpec((None, 6, 1, D_MODEL), lambda b, i: (row0 + row_stride * b, 0, 0, 0))


def _tok_spec(tm, width):
    return pl.BlockSpec((None, tm, width), lambda b, i: (b, i, 0))


def _rope(y, cos, sin_signed):
    width = y.shape[1]
    reps = width // LANES
    cosw = jnp.concatenate([cos] * reps, axis=1) if reps > 1 else cos
    sinw = jnp.concatenate([sin_signed] * reps, axis=1) if reps > 1 else sin_signed
    lane = lax.broadcasted_iota(jnp.int32, y.shape, 1)
    first_half = (lane % HEAD_DIM) < (HEAD_DIM // 2)
    partner = jnp.where(first_half, pltpu.roll(y, width - HEAD_DIM // 2, axis=1), pltpu.roll(y, HEAD_DIM // 2, axis=1))
    return y * cosw + partner * sinw


def _kv_variants(a, ones_lane=False):
    lane = lax.broadcasted_iota(jnp.int32, a.shape, 1)
    low = lane < HEAD_DIM
    swapped = pltpu.roll(a, HEAD_DIM, axis=1)
    pad_hi = jnp.where(lane == HEAD_DIM, 1.0, 0.0) if ones_lane else jnp.zeros_like(a)
    pad_lo = jnp.where(lane == 0, 1.0, 0.0) if ones_lane else jnp.zeros_like(a)
    return jnp.concatenate([
        jnp.where(low, a, pad_hi),
        jnp.where(low, pad_lo, swapped),
        jnp.where(low, swapped, pad_hi),
        jnp.where(low, pad_lo, a),
    ], axis=1)


def _inproj_even_kernel(*refs, rope):
    if rope:
        (x_ref, mod_ref, g_ref, wqkv_ref, wm_ref, wmt_ref, wgt_ref, qg_ref, kg_ref, gbt_ref, ones_ref, cos_ref, sin_ref,
         q_out, kk_out, vv_out, k_out, v_out, mqt_out, mk_out, mvt_out, mo_out, gatet_out) = refs
    else:
        (x_ref, mod_ref, g_ref, wqkv_ref, wm_ref, wmt_ref, wgt_ref, qg_ref, kg_ref, gbt_ref, ones_ref,
         q_out, kk_out, vv_out, k_out, v_out, mqt_out, mk_out, mvt_out, mo_out, gatet_out) = refs
    h = _rms(x_ref[...], g_ref[...]) * (1.0 + mod_ref[1]) + mod_ref[0]
    hb = h.astype(BF16)

    qkv = _dot(hb, wqkv_ref[...])
    w = MLSTM_WIDTH
    mm = _dot(hb, wm_ref[...])
    mk_out[...] = mm[:, :w].astype(BF16)
    mo_out[...] = mm[:, w:]
    q = _group_rms(qkv[:, :ATT_WIDTH], qg_ref[...], ones_ref[...], HEAD_DIM)
    k = _group_rms(qkv[:, ATT_WIDTH:ATT_WIDTH + ATT_KV_WIDTH], kg_ref[...], ones_ref[:ATT_KV_WIDTH, :ATT_KV_WIDTH], HEAD_DIM)
    v = qkv[:, ATT_WIDTH + ATT_KV_WIDTH:]
    if rope:
        q = _rope(q, cos_ref[...], sin_ref[...])
        k = _rope(k, cos_ref[...], sin_ref[...])
    q_out[...] = (q * (HEAD_DIM ** -0.5 * LOG2_E)).astype(BF16)
    k_out[...] = k
    v_out[...] = v
    kk_out[...] = _kv_variants(k).astype(BF16)
    vv_out[...] = _kv_variants(v, ones_lane=True).astype(BF16)

    mmt = _dot_nt(wmt_ref[...], hb)
    mqt_out[...] = (mmt[:w] * MLSTM_HD ** -0.5).astype(BF16)
    mvt_out[...] = mmt[w:].astype(BF16)

    gates_t = _dot_nt(wgt_ref[...], hb) + gbt_ref[...]
    sub = lax.broadcasted_iota(jnp.int32, gates_t.shape, 0)
    gatet_out[...] = jnp.where(sub < N_DIR * MLSTM_HEADS, gates_t, _log_sigmoid(gates_t))


def _inproj_even_call(x, mod, mod_row0, mod_stride, prm, rope_tabs):
    B, T, _ = x.shape
    tm = min(TOKEN_TILE, T)
    rope = rope_tabs is not None
    ins = [x, mod, prm["g_mix_pre"], prm["w_qkv"], prm["w_m"], prm["w_mt"], prm["w_gt"], prm["q_gain"],
           prm["k_gain"], prm["gate_bias_t"], prm["ones64"]]
    in_specs = [_tok_spec(tm, D_MODEL), _mod_spec(mod_row0, mod_stride)] + [_const_spec(a.shape) for a in ins[2:]]
    if rope:
        ins += list(rope_tabs)
        in_specs += [pl.BlockSpec((tm, LANES), lambda b, i: (i, 0))] * 2
    n_gate = 2 * N_DIR * MLSTM_HEADS
    outs = [(ATT_WIDTH, BF16, True), (4 * LANES, BF16, True), (4 * LANES, BF16, True), (ATT_KV_WIDTH, F32, True),
            (ATT_KV_WIDTH, F32, True), (MLSTM_WIDTH, BF16, False), (MLSTM_WIDTH, BF16, True), (MLSTM_WIDTH, BF16, False),
            (MLSTM_WIDTH, F32, True), (n_gate, F32, False)]
    feat_spec = lambda w: pl.BlockSpec((None, w, tm), lambda b, i: (b, 0, i))
    return pl.pallas_call(
        functools.partial(_inproj_even_kernel, rope=rope),
        grid=(B, T // tm),
        in_specs=in_specs,
        out_specs=[_tok_spec(tm, w) if tok else feat_spec(w) for w, _, tok in outs],
        out_shape=[jax.ShapeDtypeStruct((B, T, w) if tok else (B, w, T), dt) for w, dt, tok in outs],
        compiler_params=_cparams("parallel", "parallel"),
        name="inproj_even",
    )(*ins)


def _attn_kernel(*refs, cached):
    if cached:
        q_ref, kk_ref, vv_ref, kc_ref, vc_ref, o_ref = refs
    else:
        q_ref, kk_ref, vv_ref, o_ref = refs
    def value_col(head):
        kv = head // (ATT_HEADS // ATT_KV_HEADS)
        return (2 * kv + head % 2) * LANES

    sources = [(kk_ref, vv_ref)] + ([(kc_ref, vc_ref)] if cached else [])

    def scores(head):
        qp = q_ref[:, (head // 2) * LANES:(head // 2 + 1) * LANES]
        col = value_col(head)
        return [_dot_nt(qp, k_ref[:, col:col + LANES]) for k_ref, _ in sources]

    def weighted_values(head, s_blocks):
        col = value_col(head)
        m = None
        for s in s_blocks:
            bm = jnp.max(s, axis=-1, keepdims=True)
            m = bm if m is None else jnp.maximum(m, bm)
        o = None
        for s, (_, v_ref) in zip(s_blocks, sources):
            bo = _dot(jnp.exp2(s - m).astype(BF16), v_ref[:, col:col + LANES])
            o = bo if o is None else o + bo
        own_low = head % 2 == 0
        den_lane = HEAD_DIM if own_low else 0
        lane = lax.broadcasted_iota(jnp.int32, o.shape, 1)
        own = (lane < HEAD_DIM) if own_low else (lane >= HEAD_DIM)
        return jnp.where(own, o / o[:, den_lane:den_lane + 1], 0.0)

    pending = [scores(h) for h in range(ATTN_SCORES_AHEAD)]
    acc = None
    for head in range(ATT_HEADS):
        if head + ATTN_SCORES_AHEAD < ATT_HEADS:
            pending.append(scores(head + ATTN_SCORES_AHEAD))
        o = weighted_values(head, pending[head])
        if head % 2 == 0:
            acc = o
        else:
            pair = head // 2
            o_ref[:, pair * LANES:(pair + 1) * LANES] = (acc + o).astype(o_ref.dtype)


def _attn_call(q, kk, vv, cache=None):
    B, T, _ = q.shape
    tq = min(ATTN_Q_TILE, T)
    cached = cache is not None
    ins = [q, kk, vv]
    full = lambda n: pl.BlockSpec((None, n, 4 * LANES), lambda b, i: (b, 0, 0))
    in_specs = [_tok_spec(tq, ATT_WIDTH), full(T), full(T)]
    if cached:
        ins += list(cache)
        in_specs += [full(cache[0].shape[1])] * 2
    return pl.pallas_call(
        functools.partial(_attn_kernel, cached=cached),
        grid=(B, T // tq),
        in_specs=in_specs,
        out_specs=_tok_spec(tq, ATT_WIDTH),
        out_shape=jax.ShapeDtypeStruct((B, T, ATT_WIDTH), BF16),
        compiler_params=_cparams("parallel", "parallel"),
        name="attention",
    )(*ins)


def _mlstm_kernel(*refs, zero_init):
    if zero_init:
        (qf_ref, kf_ref, vf_ref, gtf_ref, qb_ref, kb_ref, vb_ref, gtb_ref, trit_ref,
         hf_ref, hb_ref, c_ref, n_ref, m_ref) = refs
    else:
        (qf_ref, kf_ref, vf_ref, gtf_ref, qb_ref, kb_ref, vb_ref, gtb_ref, trit_ref,
         c0_ref, n0_ref, m0_ref, hf_ref, hb_ref, c_ref, n_ref, m_ref) = refs
    L = MLSTM_CHUNK
    H = MLSTM_HEADS
    HD = MLSTM_HD
    PAD = 2 * SUBLANES
    chunk = pl.program_id(1)

    @pl.when(chunk == 0)
    def _init():
        if zero_init:
            c_ref[...] = jnp.zeros(c_ref.shape, F32)
            n_ref[...] = jnp.zeros(n_ref.shape, F32)
            m_ref[...] = jnp.full(m_ref.shape, M_INIT, F32)
        else:
            for d in range(N_DIR):
                for hd in range(H):
                    c_ref[d, hd] = c0_ref[d, hd].T
            n_ref[...] = n0_ref[...]
            m_ref[...] = m0_ref[...]

    m_all = m_ref[...]
    n_all = n_ref[...]
    m_row = m_all
    m_lane = lax.broadcasted_iota(jnp.int32, m_all.shape, 1)
    key = lax.broadcasted_iota(jnp.int32, (L, L), 0)
    qry = lax.broadcasted_iota(jnp.int32, (L, L), 1)
    dirs = ((qf_ref, kf_ref, vf_ref, gtf_ref, hf_ref, key <= qry, L - 1),
            (qb_ref, kb_ref, vb_ref, gtb_ref, hb_ref, key >= qry, 0))
    n_in = N_DIR * H
    order = [(d, hd) for d in range(N_DIR) for hd in range(H)]

    def query_stage(d, hd):
        qt_ref, k_ref = dirs[d][0], dirs[d][1]
        j = d * H + hd
        sl = slice(hd * HD, (hd + 1) * HD)
        n_blk = jnp.broadcast_to(n_all[j:j + 1, :], (PAD, HD)).astype(BF16)
        return _dot(jnp.concatenate([k_ref[:, sl], c_ref[d, hd].astype(BF16), n_blk], axis=0), qt_ref[sl, :])

    sums = []
    for d, (_, _, _, gt_ref, _, _, _) in enumerate(dirs):
        gates_t = gt_ref[...]
        bcum_t = _dot(jnp.concatenate(_split3(gates_t), axis=1), trit_ref[d])
        sums.append((gates_t, bcum_t))
    c_rows = [sums[d][0][d * H:(d + 1) * H] - sums[d][1][n_in + d * H:n_in + (d + 1) * H] for d in range(N_DIR)]
    c_cols = jnp.concatenate(c_rows, axis=0).T

    def first_stage(d, hd, r):
        _, k_ref, vt_ref, _, _, mask, last = dirs[d]
        gates_t, bcum_t = sums[d]
        j = d * H + hd
        jf = n_in + j
        sl = slice(hd * HD, (hd + 1) * HD)
        k = k_ref[:, sl]
        v_t = vt_ref[sl, :]
        b_row = bcum_t[jf:jf + 1, :]
        i_row = gates_t[j:j + 1, :]
        c_col = c_cols[:, j:j + 1]
        m_prev = m_all[0:1, j:j + 1]
        ct_prev = c_ref[d, hd]
        n_prev = n_all[j:j + 1, :]

        g = b_row + m_prev
        dlog = jnp.where(mask, c_col + b_row, -jnp.inf)
        m_t = jnp.maximum(g, jnp.max(dlog, axis=0, keepdims=True))
        w = jnp.exp(dlog - m_t)
        w_inter = jnp.exp(g - m_t)

        b_last = bcum_t[jf:jf + 1, last:last + 1]
        wlog = b_last - b_row + i_row
        m_new = jnp.maximum(b_last + m_prev, jnp.max(wlog, axis=1, keepdims=True))
        ws = jnp.exp(wlog - m_new)
        decay = jnp.exp(b_last + m_prev - m_new)
        vw = (v_t.astype(F32) * ws).astype(BF16)
        ws_blk = jnp.broadcast_to(ws, (PAD, L)).astype(BF16)
        upd = _dot(jnp.concatenate([vw, ws_blk], axis=0), k)
        c_new = decay * ct_prev + upd[:HD]
        n_new = decay * n_prev + upd[HD:HD + 1]
        return dict(d=d, hd=hd, j=j, r=r, w=w, w_inter=w_inter, m_t=m_t, v_t=v_t, c_new=c_new, n_new=n_new, m_new=m_new)

    def second_stage(s):
        h_ref = dirs[s["d"]][4]
        sl = slice(s["hd"] * HD, (s["hd"] + 1) * HD)
        r = s["r"]
        qk = r[:L] * s["w"]
        num = _dot(s["v_t"], qk.astype(BF16)) + s["w_inter"] * r[L:L + HD]
        den = jnp.sum(qk, axis=0, keepdims=True) + s["w_inter"] * r[L + HD:L + HD + 1]
        h_ref[:, sl] = (num / jnp.maximum(jnp.abs(den), jnp.exp(-s["m_t"]))).T

    staged = []
    for idx, (d, hd) in enumerate(order):
        staged.append(first_stage(d, hd, query_stage(d, hd)))
        if idx >= MLSTM_STAGE_LAG:
            second_stage(staged[idx - MLSTM_STAGE_LAG])
    for s in staged[len(order) - MLSTM_STAGE_LAG:]:
        second_stage(s)
    for s in staged:
        c_ref[s["d"], s["hd"]] = s["c_new"]
        m_row = jnp.where(m_lane == s["j"], s["m_new"], m_row)
    n_ref[...] = jnp.concatenate([s["n_new"] for s in staged], axis=0)
    m_ref[...] = m_row

    @pl.when(chunk == pl.num_programs(1) - 1)
    def _finish():
        for d in range(N_DIR):
            for hd in range(H):
                c_ref[d, hd] = c_ref[d, hd].T


def _mlstm_call(mq_t, mk, mv_t, gates_t, tri_t, state0):
    B, T, _ = mk.shape
    L = MLSTM_CHUNK
    nc = T // L
    zero_init = state0 is None
    fwd = lambda w: pl.BlockSpec((None, L, w), lambda b, c: (b, c, 0))
    bwd = lambda w: pl.BlockSpec((None, L, w), lambda b, c: (b, nc - 1 - c, 0))
    fwd_t = lambda w: pl.BlockSpec((None, w, L), lambda b, c: (b, 0, c))
    bwd_t = lambda w: pl.BlockSpec((None, w, L), lambda b, c: (b, 0, nc - 1 - c))
    W = MLSTM_WIDTH
    NG = 2 * N_DIR * MLSTM_HEADS
    c_spec = pl.BlockSpec((None, N_DIR, MLSTM_HEADS, MLSTM_HD, MLSTM_HD), lambda b, c: (b, 0, 0, 0, 0))
    n_spec = pl.BlockSpec((None, N_DIR * MLSTM_HEADS, MLSTM_HD), lambda b, c: (b, 0, 0))
    m_spec = pl.BlockSpec((None, 1, LANES), lambda b, c: (b, 0, 0))
    ins = [mq_t, mk, mv_t, gates_t, mq_t, mk, mv_t, gates_t, tri_t]
    in_specs = [fwd_t(W), fwd(W), fwd_t(W), fwd_t(NG), bwd_t(W), bwd(W), bwd_t(W), bwd_t(NG), _const_spec(tri_t.shape)]
    if not zero_init:
        ins += list(state0)
        in_specs += [c_spec, n_spec, m_spec]
    return pl.pallas_call(
        functools.partial(_mlstm_kernel, zero_init=zero_init),
        grid=(B, nc),
        in_specs=in_specs,
        out_specs=[fwd(W), bwd(W), c_spec, n_spec, m_spec],
        out_shape=[jax.ShapeDtypeStruct((B, T, W), F32), jax.ShapeDtypeStruct((B, T, W), F32),
                   jax.ShapeDtypeStruct((B, N_DIR, MLSTM_HEADS, MLSTM_HD, MLSTM_HD), F32),
                   jax.ShapeDtypeStruct((B, N_DIR * MLSTM_HEADS, MLSTM_HD), F32),
                   jax.ShapeDtypeStruct((B, 1, LANES), F32)],
        compiler_params=_cparams("parallel", "arbitrary"),
        name="mlstm",
    )(*ins)


def _residual_ffn(x_ref, mix_fn, mod_ref, gpost_ref, gfpre_ref, gfpost_ref, wout_ref, w1_ref, w2_ref, o_ref):
    half = D_MODEL // 2
    sub = min(POST_SUBTILE, x_ref.shape[0])
    rows = [slice(r0, r0 + sub) for r0 in range(0, x_ref.shape[0], sub)]
    ys = []
    for r in rows:
        mix_a, mix_b = mix_fn(r)
        ys.append(_dot(mix_a.astype(BF16), wout_ref[:half, :]) + _dot(mix_b.astype(BF16), wout_ref[half:, :]))
    for r, y in zip(rows, ys):
        x1 = x_ref[r, :] + mod_ref[2] * _rms(y, gpost_ref[...])
        h2 = (_rms(x1, gfpre_ref[...]) * (1.0 + mod_ref[4]) + mod_ref[3]).astype(BF16)
        chunks = list(range(0, D_FF, FF_CHUNK))
        up = _dot(h2, w1_ref[:, chunks[0]:chunks[0] + FF_CHUNK])
        f = None
        for i, c0 in enumerate(chunks):
            cur = up
            if i + 1 < len(chunks):
                up = _dot(h2, w1_ref[:, chunks[i + 1]:chunks[i + 1] + FF_CHUNK])
            part = _dot(jnp.square(jnp.maximum(cur, 0.0)).astype(BF16), w2_ref[c0:c0 + FF_CHUNK, :])
            f = part if f is None else f + part
        o_ref[r, :] = x1 + mod_ref[5] * _rms(f, gfpost_ref[...])


def _head_rms(y, gain_ref, width):
    outs = []
    for h0 in range(0, y.shape[1], width):
        seg = y[:, h0:h0 + width]
        outs.append(seg * lax.rsqrt(jnp.mean(seg * seg, axis=-1, keepdims=True) + EPS) * gain_ref[:, h0:h0 + width])
    return jnp.concatenate(outs, axis=1)


def _post_even_kernel(x_ref, attn_ref, hf_ref, hb_ref, mo_ref, mod_ref, gm_ref, gpost_ref, gfpre_ref, gfpost_ref,
                      wout_ref, w1_ref, w2_ref, o_ref):
    def mix(r):
        hm = _head_rms(hf_ref[r, :] + hb_ref[r, :], gm_ref, MLSTM_HD) * jax.nn.sigmoid(mo_ref[r, :])
        return attn_ref[r, :], hm

    _residual_ffn(x_ref, mix, mod_ref, gpost_ref, gfpre_ref, gfpost_ref, wout_ref, w1_ref, w2_ref, o_ref)


def _post_odd_kernel(x_ref, lru_ref, rg_ref, of_ref, ob_ref, gg_ref, mod_ref, gm_ref, gpost_ref, gfpre_ref, gfpost_ref,
                     wout_ref, w1_ref, w2_ref, o_ref):
    def mix(r):
        lru = jax.nn.gelu(rg_ref[r, :]) * lru_ref[r, :]
        og = _head_rms(of_ref[r, :] + ob_ref[r, :], gm_ref, GLA_DV) * jax.nn.silu(gg_ref[r, :])
        return lru, og

    _residual_ffn(x_ref, mix, mod_ref, gpost_ref, gfpre_ref, gfpost_ref, wout_ref, w1_ref, w2_ref, o_ref)


def _post_call(kernel_fn, name, x, mix_ins, mod, mod_row0, mod_stride, consts):
    shape = x.shape
    if mod_stride == 0:
        x = x.reshape(1, -1, D_MODEL)
        mix_ins = [a.reshape(1, -1, a.shape[2]) for a in mix_ins]
    B, T, _ = x.shape
    tm = min(POST_TILE, T)
    ins = [x] + list(mix_ins) + [mod] + list(consts)
    in_specs = ([_tok_spec(tm, D_MODEL)] + [_tok_spec(tm, a.shape[2]) for a in mix_ins] + [_mod_spec(mod_row0, mod_stride)]
                + [_const_spec(a.shape) for a in consts])
    return pl.pallas_call(
        kernel_fn,
        grid=(B, T // tm),
        in_specs=in_specs,
        out_specs=_tok_spec(tm, D_MODEL),
        out_shape=jax.ShapeDtypeStruct(x.shape, F32),
        compiler_params=_cparams("parallel", "parallel"),
        name=name,
    )(*ins).reshape(shape)


def _inproj_odd_kernel(x_ref, xp_ref, xn_ref, mod_ref, g_ref, wmain_ref, wlr_in_ref, wlr_hi_ref, wlr_lo_ref, blr_ref,
                       cw_ref, cb_ref, wg_ref, bg_ref, lam_ref,
                       rg_out, gq_out, gk_out, gv_out, gg_out, la_out, af_out, uf_out, ab_out, ub_out):
    tm = x_ref.shape[0]
    W = LRU_WIDTH
    sub = min(ODD_GROUP_ROWS, tm)
    n_groups = tm // sub
    m = sub + 2 * CONV_HALO
    first = pl.program_id(1) == 0
    last = pl.program_id(1) == pl.num_programs(1) - 1
    rid = lax.broadcasted_iota(jnp.int32, (m, W), 0)
    neg_half_c_sp = (-0.5 * LRU_C) * _softplus(-lam_ref[...])

    def modulated(xv):
        return (_rms(xv, g_ref[...]) * (1.0 + mod_ref[1]) + mod_ref[0]).astype(BF16)

    for q in range(n_groups):
        rows = slice(q * sub, (q + 1) * sub)
        prev = xp_ref[...] if q == 0 else x_ref[q * sub - CONV_HALO:q * sub, :]
        nxt = xn_ref[...] if q == n_groups - 1 else x_ref[(q + 1) * sub:(q + 1) * sub + CONV_HALO, :]
        hb_ext = modulated(jnp.concatenate([prev, x_ref[rows, :], nxt], axis=0))
        hb = hb_ext[CONV_HALO:CONV_HALO + sub]

        glr_hi, glr_lo = _split2(_dot(hb, wlr_in_ref[...]))

        rx = _dot(hb_ext, wmain_ref[:, :W])
        outside = None
        if q == 0:
            outside = first & (rid < CONV_HALO)
        if q == n_groups - 1:
            tail = last & (rid >= sub + CONV_HALO)
            outside = tail if outside is None else outside | tail
        if outside is not None:
            rx = jnp.where(outside, 0.0, rx)

        mm = _dot(hb, wmain_ref[:, W:3 * W])
        rg_out[rows, :] = mm[:, :W]
        gq_out[rows, :] = mm[:, W:W + GLA_KW] * GLA_DK ** -0.5
        gk_out[rows, :] = mm[:, W + GLA_KW:]

        z = _dot(glr_hi, wlr_hi_ref[...]) + (_dot(glr_hi, wlr_lo_ref[...]) + _dot(glr_lo, wlr_hi_ref[...])) + blr_ref[...]
        la_out[rows, :] = _log_sigmoid(z) * (1.0 / GLA_TAU)

        xc = cb_ref[...] + rx * cw_ref[CONV_LEFT:CONV_LEFT + 1, :]
        for tap in range(CONV_W):
            off = tap - CONV_LEFT
            if off != 0:
                xc = xc + pltpu.roll(rx, (-off) % m, axis=0) * cw_ref[tap:tap + 1, :]
        xc = xc[CONV_HALO:CONV_HALO + sub, :]
        xcb = xc.astype(BF16)
        half_xc = 0.5 * xc
        for d, (a_out, u_out) in enumerate(((af_out, uf_out), (ab_out, ub_out))):
            cols = slice(2 * d * W, (2 * d + 2) * W)
            th = jnp.tanh(_dot(xcb, wg_ref[:, cols]) + bg_ref[:, cols])
            if d == 0:
                gv_out[rows, :] = _dot(hb, wmain_ref[:, 3 * W:4 * W]).astype(BF16)
            else:
                gg_out[rows, :] = _dot(hb, wmain_ref[:, 4 * W:])
            th_r = th[:, :W]
            th_i = th[:, W:]
            log_a = neg_half_c_sp[d:d + 1, :] * th_r + neg_half_c_sp[d:d + 1, :]
            a = jnp.exp(log_a)
            a_out[rows, :] = a
            v = jnp.tanh(log_a) * (-1.0 - a * a)
            u_out[rows, :] = jnp.where(v > 0.0, v * lax.rsqrt(v), 0.0) * (half_xc * th_i + half_xc)


def _inproj_odd_call(x, mod, mod_row0, mod_stride, prm, lru_prm):
    B, T, _ = x.shape
    tm = min(TOKEN_TILE, T)
    halo_per_tile = tm // CONV_HALO
    n_halo = T // CONV_HALO
    prev_spec = pl.BlockSpec((None, CONV_HALO, D_MODEL), lambda b, i: (b, jnp.maximum(i * halo_per_tile - 1, 0), 0))
    next_spec = pl.BlockSpec((None, CONV_HALO, D_MODEL),
                             lambda b, i: (b, jnp.minimum((i + 1) * halo_per_tile, n_halo - 1), 0))
    consts = [prm["g_mix_pre"], prm["w_main"], prm["w_lr_in"], prm["w_lr_hi"], prm["w_lr_lo"], prm["b_lr"],
              lru_prm["conv_w"], lru_prm["conv_b"], lru_prm["w_gate"], lru_prm["b_gate"], lru_prm["lam"]]
    ins = [x, x, x, mod] + consts
    in_specs = ([_tok_spec(tm, D_MODEL), prev_spec, next_spec, _mod_spec(mod_row0, mod_stride)]
                + [_const_spec(a.shape) for a in consts])
    widths = [(LRU_WIDTH, F32), (GLA_KW, F32), (GLA_KW, F32), (GLA_WIDTH, BF16), (GLA_WIDTH, F32),
              (N_DIR * GLA_KW, F32)] + [(LRU_WIDTH, F32)] * 4
    return pl.pallas_call(
        _inproj_odd_kernel,
        grid=(B, T // tm),
        in_specs=in_specs,
        out_specs=[_tok_spec(tm, w) for w, _ in widths],
        out_shape=[jax.ShapeDtypeStruct((B, T, w), dt) for w, dt in widths],
        compiler_params=_cparams("parallel", "parallel"),
        name="inproj_odd",
    )(*ins)


def _lru_kernel(*refs, zero_init):
    if zero_init:
        af_s, uf_s, ab_s, ub_s, o_ref, last_ref, hb_s = refs
    else:
        af_s, uf_s, ab_s, ub_s, h0_ref, o_ref, last_ref, hb_s = refs
    T, W = af_s.shape
    nblk = T // SUBLANES
    rid8 = lax.broadcasted_iota(jnp.int32, (SUBLANES, W), 0)

    def block_scan(a, u, reverse):
        for dist in (1, 2, 4):
            if reverse:
                keep = rid8 < SUBLANES - dist
                shift = SUBLANES - dist
            else:
                keep = rid8 >= dist
                shift = dist
            a_n = jnp.where(keep, pltpu.roll(a, shift, axis=0), 1.0)
            u_n = jnp.where(keep, pltpu.roll(u, shift, axis=0), 0.0)
            u = a * u_n + u
            a = a * a_n
        return a, u

    def body(blk, carry):
        hf, hb = carry
        rf = pl.multiple_of(blk * SUBLANES, SUBLANES)
        a, u = block_scan(af_s[pl.ds(rf, SUBLANES), :], uf_s[pl.ds(rf, SUBLANES), :], False)
        out_f = u + a * hf
        o_ref[pl.ds(rf, SUBLANES), :] = out_f
        rb = pl.multiple_of((nblk - 1 - blk) * SUBLANES, SUBLANES)
        a, u = block_scan(ab_s[pl.ds(rb, SUBLANES), :], ub_s[pl.ds(rb, SUBLANES), :], True)
        out_b = u + a * hb
        hb_s[pl.ds(rb, SUBLANES), :] = out_b
        return out_f[SUBLANES - 1:SUBLANES, :], out_b[0:1, :]

    if zero_init:
        init = (jnp.zeros((1, W), F32), jnp.zeros((1, W), F32))
    else:
        init = (h0_ref[0:1, :], h0_ref[1:2, :])
    hf, hb = lax.fori_loop(0, nblk, body, init)
    last_ref[0:1, :] = hf
    last_ref[1:2, :] = hb
    o_ref[...] = o_ref[...] + hb_s[...]


def _lru_call(coeffs, h0):
    B, T, W = coeffs[0].shape
    zero_init = h0 is None
    seq = pl.BlockSpec((None, T, W), lambda b: (b, 0, 0))
    st = pl.BlockSpec((None, N_DIR, W), lambda b: (b, 0, 0))
    ins = list(coeffs)
    in_specs = [seq] * len(ins)
    if not zero_init:
        ins.append(h0)
        in_specs.append(st)
    return pl.pallas_call(
        functools.partial(_lru_kernel, zero_init=zero_init),
        grid=(B,),
        in_specs=in_specs,
        out_specs=[seq, st],
        out_shape=[jax.ShapeDtypeStruct((B, T, W), F32), jax.ShapeDtypeStruct((B, N_DIR, W), F32)],
        scratch_shapes=[pltpu.VMEM((T, W), F32)],
        compiler_params=_cparams("parallel"),
        name="rglru",
    )(*ins)


GLA_LEVELS = GLA_CHUNK.bit_length() - 1


def _gla_constants():
    L = GLA_CHUNK
    idx = np.arange(L)
    sel = np.zeros((N_DIR, (GLA_LEVELS + 3) * L, L), np.float32)
    lvl_mask = np.zeros((N_DIR, GLA_LEVELS + 1, L, L), np.float32)
    for d in range(N_DIR):
        tri = (idx[None, :] <= idx[:, None]) if d == 0 else (idx[None, :] >= idx[:, None])
        tri = tri.astype(np.float32)
        last = L - 1 if d == 0 else 0
        sel[d, 0:L] = tri
        for lv in range(GLA_LEVELS):
            m = L >> (lv + 1)
            start = (idx // (2 * m)) * (2 * m)
            second = (idx - start) >= m
            boundary = start + (m - 1 if d == 0 else m)
            query_role = second if d == 0 else ~second
            diff = tri - tri[boundary]
            sel[d, (1 + lv) * L:(2 + lv) * L] = np.where(query_role[:, None], diff, -diff)
            same = start[:, None] == start[None, :]
            lvl_mask[d, lv] = same & query_role[:, None] & ~query_role[None, :]
        lvl_mask[d, GLA_LEVELS] = np.eye(L)
        sel[d, (GLA_LEVELS + 1) * L:(GLA_LEVELS + 2) * L] = tri[last][None, :] - tri
        sel[d, (GLA_LEVELS + 2) * L:] = tri[last][None, :]
    assert sel.min() >= 0.0 and sel.max() <= 1.0
    sel3 = np.concatenate([sel] * 3, axis=2)
    lvl_mask = np.tile(lvl_mask, (1, 1, 1, GLA_HEADS))
    return jnp.asarray(sel3, BF16), jnp.asarray(lvl_mask, F32)


def _gla_kernel(*refs, zero_init):
    if zero_init:
        (qf_ref, kf_ref, vf_ref, af_ref, qb_ref, kb_ref, vb_ref, ab_ref, sel_ref, msk_ref, of_ref, ob_ref, s_ref) = refs
    else:
        (qf_ref, kf_ref, vf_ref, af_ref, qb_ref, kb_ref, vb_ref, ab_ref, sel_ref, msk_ref, s0_ref,
         of_ref, ob_ref, s_ref) = refs
    L = GLA_CHUNK
    H = GLA_HEADS
    KW = GLA_KW
    VW = GLA_WIDTH
    DK = GLA_DK
    DV = GLA_DV

    @pl.when(pl.program_id(1) == 0)
    def _init():
        if zero_init:
            s_ref[...] = jnp.zeros(s_ref.shape, F32)
        else:
            s_ref[...] = s0_ref[...]

    zero_k = jnp.zeros((L, KW), BF16)
    zero_v = jnp.zeros((L, VW), BF16)
    lane_k = lax.broadcasted_iota(jnp.int32, (L, KW), 1) // DK
    lane_v = lax.broadcasted_iota(jnp.int32, (L, VW), 1) // DV

    def stack_heads(a, lane_head, zero):
        return jnp.concatenate([jnp.where(lane_head == h, a, zero) for h in range(H)], axis=0)

    def tile_rows(a):
        return jnp.concatenate([a] * H, axis=0)

    dirs = ((qf_ref, kf_ref, vf_ref, af_ref, of_ref), (qb_ref, kb_ref, vb_ref, ab_ref, ob_ref))
    work = []
    for i in range(GLA_STEP_CHUNKS):
        for d, (q_ref, k_ref, v_ref, a_ref, o_ref) in enumerate(dirs):
            j = i if d == 0 else GLA_STEP_CHUNKS - 1 - i
            rows = slice(j * L, (j + 1) * L)
            sums = _dot(sel_ref[d], jnp.concatenate(_split3(a_ref[rows, :]), axis=0))
            q_b = q_ref[rows, :].astype(BF16)
            work.append(dict(d=d, rows=rows, sums=sums, q_b=q_b, q_st=stack_heads(q_b, lane_k, zero_k),
                             k_st=stack_heads(k_ref[rows, :].astype(BF16), lane_k, zero_k)))

    for lv in range(GLA_LEVELS + 1):
        for wk in work:
            if lv == GLA_LEVELS:
                part = _dot_nt(wk["q_b"], wk["k_st"])
            else:
                wl = jnp.exp(wk["sums"][(1 + lv) * L:(2 + lv) * L]).astype(BF16)
                part = _dot_nt(wk["q_b"] * wl, wk["k_st"] * tile_rows(wl))
            part = part * msk_ref[wk["d"], lv]
            wk["scores"] = part if lv == 0 else wk["scores"] + part

    state = [s_ref[d] for d in range(N_DIR)]
    for wk in work:
        d, rows, sums = wk["d"], wk["rows"], wk["sums"]
        _, k_ref, v_ref, _, o_ref = dirs[d]
        v = v_ref[rows, :]
        s_prev = state[d]
        intra = _dot(wk["scores"].astype(BF16), stack_heads(v, lane_v, zero_v))
        q_in = wk["q_st"] * tile_rows(jnp.exp(sums[0:L]).astype(BF16))
        inter = _dot(q_in, s_prev.astype(BF16))
        o_ref[rows, :] = intra + jnp.concatenate([inter[h * L:(h + 1) * L] for h in range(H)], axis=1)

        k_out = k_ref[rows, :] * jnp.exp(sums[(GLA_LEVELS + 1) * L:(GLA_LEVELS + 2) * L])
        dec = jnp.exp(sums[(GLA_LEVELS + 2) * L:])
        both_t = jnp.concatenate([k_out, dec], axis=0).T
        k_out_t = both_t[:, :L].astype(BF16)
        upd = [_dot(k_out_t[h * DK:(h + 1) * DK], v[:, h * DV:(h + 1) * DV]) for h in range(H)]
        state[d] = both_t[:, L:L + 1] * s_prev + jnp.concatenate(upd, axis=0)
    for d in range(N_DIR):
        s_ref[d] = state[d]


def _gla_call(gq, gk, gv, log_a, sel3, lvl_mask, s0):
    B, T, _ = gq.shape
    L = GLA_CHUNK * GLA_STEP_CHUNKS
    nc = T // L
    zero_init = s0 is None
    fwd = lambda w: pl.BlockSpec((None, L, w), lambda b, c: (b, c, 0))
    bwd = lambda w: pl.BlockSpec((None, L, w), lambda b, c: (b, nc - 1 - c, 0))
    a_fwd = pl.BlockSpec((None, L, GLA_KW), lambda b, c: (b, c, 0))
    a_bwd = pl.BlockSpec((None, L, GLA_KW), lambda b, c: (b, nc - 1 - c, 1))
    s_spec = pl.BlockSpec((None, N_DIR, GLA_KW, GLA_DV), lambda b, c: (b, 0, 0, 0))
    ins = [gq, gk, gv, log_a, gq, gk, gv, log_a, sel3, lvl_mask]
    in_specs = [fwd(GLA_KW), fwd(GLA_KW), fwd(GLA_WIDTH), a_fwd, bwd(GLA_KW), bwd(GLA_KW), bwd(GLA_WIDTH), a_bwd,
                _const_spec(sel3.shape), _const_spec(lvl_mask.shape)]
    if not zero_init:
        ins.append(s0)
        in_specs.append(s_spec)
    return pl.pallas_call(
        functools.partial(_gla_kernel, zero_init=zero_init),
        grid=(B, nc),
        in_specs=in_specs,
        out_specs=[fwd(GLA_WIDTH), bwd(GLA_WIDTH), s_spec],
        out_shape=[jax.ShapeDtypeStruct((B, T, GLA_WIDTH), F32), jax.ShapeDtypeStruct((B, T, GLA_WIDTH), F32),
                   jax.ShapeDtypeStruct((B, N_DIR, GLA_KW, GLA_DV), F32)],
        compiler_params=_cparams("parallel", "arbitrary"),
        name="gla",
    )(*ins)


def _block_diag(blocks):
    n, r, c = blocks.shape
    eye = jnp.eye(n, dtype=blocks.dtype)
    return (eye[:, None, :, None] * blocks[:, :, None, :]).reshape(n * r, n * c)


def _pad_cols(a, width):
    return jnp.pad(a, ((0, 0), (0, width - a.shape[1])))


def _rope_tables(n_tokens):
    rows = n_tokens // GRID_W
    row = jnp.repeat(jnp.arange(rows), GRID_W).astype(F32)
    col = jnp.tile(jnp.arange(GRID_W), rows).astype(F32)
    inv = jnp.power(ROPE_THETA, -jnp.arange(ROPE_PAIRS_PER_AXIS, dtype=F32) / ROPE_PAIRS_PER_AXIS)
    ang = jnp.concatenate([row[:, None] * inv, col[:, None] * inv], axis=-1)
    cos, sin = jnp.cos(ang), jnp.sin(ang)
    reps = LANES // HEAD_DIM
    return jnp.tile(jnp.concatenate([cos, cos], axis=-1), (1, reps)), jnp.tile(jnp.concatenate([-sin, sin], axis=-1), (1, reps))


def _cache_variants(cache, ones_lane=False):
    z = jnp.zeros_like(cache[:, 0])
    if ones_lane:
        z = z.at[..., 0].set(1.0)
    return jnp.concatenate([cache[:, 0], z, z, cache[:, 0], cache[:, 1], z, z, cache[:, 1]], axis=-1).astype(BF16)


def _mlstm_tri():
    idx = np.arange(MLSTM_CHUNK)
    upper = idx[:, None] <= idx[None, :]
    tri_t = np.stack([upper, upper.T]).astype(np.float32)
    return jnp.asarray(np.concatenate([tri_t] * 3, axis=1), BF16)


def kernel(x_prompt, x_sample, cache_attn_k, cache_attn_v, state_mlstm_C, state_mlstm_n, state_mlstm_m, state_lru_h, state_gla_S, c, c_ctx, ada_w, ada_b, norm_mix_pre, norm_mix_post, norm_ffn_pre, norm_ffn_post, w_out, ffn_w1, ffn_w2, w_in_even, attn_q_norm, attn_k_norm, mlstm_i_bias, mlstm_f_bias, mlstm_norm, w_in_odd, lru_conv_w, lru_conv_b, lru_w_r, lru_b_r, lru_w_i, lru_b_i, lru_lambda, gla_w_lr, gla_b_lr, gla_norm):
    depth = ada_w.shape[0]
    Bp = x_prompt.shape[0]
    Bs = x_sample.shape[0]
    row = lambda a: a.reshape(1, -1)

    n_rows = -(-(1 + Bs) // SUBLANES) * SUBLANES
    cvec = jnp.zeros((n_rows, D_MODEL), F32).at[0].set(c_ctx).at[1:1 + Bs].set(c)
    mod_all = _ada_call(cvec, ada_w, ada_b).reshape(depth, n_rows, 6, 1, D_MODEL)

    rope_tabs = _rope_tables(x_sample.shape[1])
    ones64 = _block_diag(jnp.ones((ATT_HEADS, HEAD_DIM, HEAD_DIM), BF16))
    tri_t = _mlstm_tri()
    sel3, lvl_mask = _gla_constants()

    xp, xs = x_prompt, x_sample
    outs = {}
    for l in range(depth):
        mod = mod_all[l]
        tail = [row(norm_mix_post[l]), row(norm_ffn_pre[l]), row(norm_ffn_post[l]),
                w_out[l].astype(BF16), ffn_w1[l].astype(BF16), ffn_w2[l].astype(BF16)]
        if l % 2 == 0:
            e = l // 2
            w_in = w_in_even[e]
            o1 = ATT_WIDTH + 2 * ATT_KV_WIDTH
            o2 = o1 + 4 * MLSTM_WIDTH
            prm = {
                "g_mix_pre": row(norm_mix_pre[l]),
                "w_qkv": w_in[:, :o1].astype(BF16),
                "w_m": jnp.concatenate([w_in[:, o1 + MLSTM_WIDTH:o1 + 2 * MLSTM_WIDTH], w_in[:, o1 + 3 * MLSTM_WIDTH:o2]],
                                       axis=1).astype(BF16),
                "w_mt": jnp.concatenate([w_in[:, o1:o1 + MLSTM_WIDTH], w_in[:, o1 + 2 * MLSTM_WIDTH:o1 + 3 * MLSTM_WIDTH]],
                                        axis=1).T.astype(BF16),
                "w_gt": w_in[:, o2:].T.astype(BF16),
                "gate_bias_t": jnp.concatenate([mlstm_i_bias[e].reshape(-1), mlstm_f_bias[e].reshape(-1)]).reshape(-1, 1),
                "q_gain": row(jnp.tile(attn_q_norm[e], ATT_HEADS)),
                "k_gain": row(jnp.tile(attn_k_norm[e], ATT_KV_HEADS)),
                "ones64": ones64,
            }
            consts = [row(mlstm_norm[e])] + tail
            for path in ("prompt", "sample"):
                if path == "prompt":
                    x, r0, rs, tabs, cache, st0 = xp, 0, 0, None, None, None
                else:
                    x, r0, rs, tabs = xs, 1, 1, rope_tabs
                    cache = (_cache_variants(cache_attn_k[:, e]), _cache_variants(cache_attn_v[:, e], ones_lane=True))
                    st0 = (state_mlstm_C[:, e], state_mlstm_n[:, e].reshape(Bs, N_DIR * MLSTM_HEADS, MLSTM_HD),
                           _pad_cols(state_mlstm_m[:, e].reshape(Bs, -1), LANES).reshape(Bs, 1, LANES))
                q, kk, vv, k_n, v_n, mq_t, mk, mv_t, mo, gates_t = _inproj_even_call(x, mod, r0, rs, prm, tabs)
                attn = _attn_call(q, kk, vv, cache)
                hf, hb, c_fin, n_fin, m_fin = _mlstm_call(mq_t, mk, mv_t, gates_t, tri_t, st0)
                x_new = _post_call(_post_even_kernel, "post_even", x, [attn, hf, hb, mo], mod, r0, rs, consts)
                if path == "prompt":
                    xp = x_new
                    T = x.shape[1]
                    outs.setdefault("k", []).append(jnp.swapaxes(k_n.reshape(Bp, T, ATT_KV_HEADS, HEAD_DIM), 1, 2))
                    outs.setdefault("v", []).append(jnp.swapaxes(v_n.reshape(Bp, T, ATT_KV_HEADS, HEAD_DIM), 1, 2))
                    outs.setdefault("C", []).append(c_fin)
                    outs.setdefault("n", []).append(n_fin.reshape(Bp, N_DIR, MLSTM_HEADS, MLSTM_HD))
                    outs.setdefault("m", []).append(m_fin[:, 0, :N_DIR * MLSTM_HEADS].reshape(Bp, N_DIR, MLSTM_HEADS))
                else:
                    xs = x_new
        else:
            o = l // 2
            w_in = w_in_odd[o]
            o1 = 2 * LRU_WIDTH + 2 * GLA_KW + 2 * GLA_WIDTH
            w_lr = _block_diag(gla_w_lr[o])
            w_lr = jnp.pad(w_lr, ((0, LANES - w_lr.shape[0]), (0, 0)))
            w_lr_hi = w_lr.astype(BF16)
            prm = {
                "g_mix_pre": row(norm_mix_pre[l]),
                "w_main": w_in[:, :o1].astype(BF16),
                "w_lr_in": _pad_cols(w_in[:, o1:], LANES).astype(BF16),
                "w_lr_hi": w_lr_hi,
                "w_lr_lo": (w_lr - w_lr_hi.astype(F32)).astype(BF16),
                "b_lr": row(gla_b_lr[o]),
            }
            lru_prm = {
                "conv_w": lru_conv_w[o],
                "conv_b": row(lru_conv_b[o]),
                "w_gate": (0.5 * jnp.concatenate([_block_diag(lru_w_r[o, 0]), _block_diag(lru_w_i[o, 0]),
                                                  _block_diag(lru_w_r[o, 1]), _block_diag(lru_w_i[o, 1])], axis=1)).astype(BF16),
                "b_gate": 0.5 * row(jnp.stack([lru_b_r[o, 0], lru_b_i[o, 0], lru_b_r[o, 1], lru_b_i[o, 1]])),
                "lam": lru_lambda[o],
            }
            consts = [row(gla_norm[o])] + tail
            for path in ("prompt", "sample"):
                if path == "prompt":
                    x, r0, rs, h0, s0 = xp, 0, 0, None, None
                else:
                    x, r0, rs, h0 = xs, 1, 1, state_lru_h[:, o]
                    s0 = state_gla_S[:, o].reshape(Bs, N_DIR, GLA_KW, GLA_DV)
                rg, gq, gk, gv, gg, log_a, *lru_coeffs = _inproj_odd_call(x, mod, r0, rs, prm, lru_prm)
                lru, h_last = _lru_call(lru_coeffs, h0)
                of, ob, s_fin = _gla_call(gq, gk, gv, log_a, sel3, lvl_mask, s0)
                x_new = _post_call(_post_odd_kernel, "post_odd", x, [lru, rg, of, ob, gg], mod, r0, rs, consts)
                if path == "prompt":
                    xp = x_new
                    outs.setdefault("h", []).append(h_last)
                    outs.setdefault("S", []).append(s_fin.reshape(Bp, N_DIR, GLA_HEADS, GLA_DK, GLA_DV))
                else:
                    xs = x_new

    stack = lambda name: jnp.stack(outs[name], axis=1)
    return (xp, xs, stack("k"), stack("v"), stack("C"), stack("n"), stack("m"), stack("h"), stack("S"))
```

```python
import functools

import numpy as np
import jax
import jax.numpy as jnp
from jax import lax
from jax.experimental import pallas as pl
from jax.experimental.pallas import tpu as pltpu

F32 = jnp.float32
BF16 = jnp.bfloat16

D_MODEL = 1024
D_FF = 4 * D_MODEL
GRID_W = 64
EPS = 1e-6
LOG2_E = 1.4426950408889634
N_DIR = 2

ATT_HEADS = 8
ATT_KV_HEADS = 2
HEAD_DIM = 64
ATT_WIDTH = ATT_HEADS * HEAD_DIM
ATT_KV_WIDTH = ATT_KV_HEADS * HEAD_DIM
ROPE_THETA = 10000.0
ROPE_PAIRS_PER_AXIS = HEAD_DIM // 4

MLSTM_HEADS = 4
MLSTM_HD = 128
MLSTM_WIDTH = MLSTM_HEADS * MLSTM_HD
MLSTM_CHUNK = 256
MLSTM_STAGE_LAG = 2
M_INIT = -1e30

LRU_WIDTH = 512
LRU_BLOCKS = 8
LRU_BD = LRU_WIDTH // LRU_BLOCKS
LRU_C = 8.0
CONV_W = 4
CONV_LEFT = 2

GLA_HEADS = 4
GLA_DK = 64
GLA_DV = 128
GLA_KW = GLA_HEADS * GLA_DK
GLA_WIDTH = GLA_HEADS * GLA_DV
GLA_RANK = 16
GLA_TAU = 16.0
GLA_CHUNK = 64
GLA_STEP_CHUNKS = 4

LANES = 128
SUBLANES = 8
VMEM_LIMIT = 56 * 1024 * 1024

TOKEN_TILE = 512
ATTN_Q_TILE = 256
ATTN_SCORES_AHEAD = 1
ATTN_DEN_ON_MXU_MIN_KEYS = 1024
FF_CHUNK = 1024
POST_TILE = 512
POST_SUBTILE = 256
ODD_GROUP_ROWS = 256
CONV_HALO = 2 * SUBLANES


def _cparams(*sem):
    return pltpu.CompilerParams(dimension_semantics=sem, vmem_limit_bytes=VMEM_LIMIT)


def _const_spec(shape):
    n = len(shape)
    return pl.BlockSpec(shape, lambda *_: (0,) * n, pipeline_mode=pl.Buffered(1))


def _dot(a, b):
    return jnp.dot(a, b, preferred_element_type=F32)


def _dot_nt(a, b):
    return lax.dot_general(a, b, (((1,), (1,)), ((), ())), preferred_element_type=F32)


def _dot_tn(a, b):
    return lax.dot_general(a, b, (((0,), (0,)), ((), ())), preferred_element_type=F32)


def _split3(x):
    hi = x.astype(BF16)
    r = x - hi.astype(F32)
    mid = r.astype(BF16)
    lo = (r - mid.astype(F32)).astype(BF16)
    return hi, mid, lo


def _split2(x):
    hi = x.astype(BF16)
    lo = (x - hi.astype(F32)).astype(BF16)
    return hi, lo


def _rms(x, g):
    return x * lax.rsqrt(jnp.mean(x * x, axis=-1, keepdims=True) + EPS) * g


def _log_sigmoid(x):
    return jnp.minimum(x, 0.0) - jnp.log1p(jnp.exp(-jnp.abs(x)))


def _softplus(x):
    return jnp.maximum(x, 0.0) + jnp.log1p(jnp.exp(-jnp.abs(x)))


def _group_rms(y, gain, ones_bd, width):
    hi, lo = _split2(y * y)
    ss = _dot(hi, ones_bd) + _dot(lo, ones_bd)
    return y * lax.rsqrt(ss * (1.0 / width) + EPS) * gain


def _ada_kernel(c_ref, w_ref, b_ref, o_ref):
    s = jax.nn.silu(c_ref[...])
    o_ref[...] = _dot(s.astype(BF16), w_ref[...].astype(BF16)) + b_ref[...]


def _ada_call(cvec, ada_w, ada_b):
    depth = ada_w.shape[0]
    rows = cvec.shape[0]
    n_col = ada_w.shape[2] // D_MODEL
    return pl.pallas_call(
        _ada_kernel,
        grid=(depth, n_col),
        in_specs=[
            pl.BlockSpec((rows, D_MODEL), lambda l, j: (0, 0)),
            pl.BlockSpec((None, D_MODEL, D_MODEL), lambda l, j: (l, 0, j)),
            pl.BlockSpec((None, 1, D_MODEL), lambda l, j: (l, 0, j)),
        ],
        out_specs=pl.BlockSpec((None, rows, D_MODEL), lambda l, j: (l, 0, j)),
        out_shape=jax.ShapeDtypeStruct((depth, rows, ada_w.shape[2]), F32),
        compiler_params=_cparams("arbitrary", "arbitrary"),
        name="ada_mod",
    )(cvec, ada_w, ada_b.reshape(depth, 1, -1))


def _mod_spec(row0, row_stride):
    return pl.BlockSpec((None, 6, 1, D_MODEL), lambda b, i: (row0 + row_stride * b, 0, 0, 0))


def _tok_spec(tm, width):
    return pl.BlockSpec((None, tm, width), lambda b, i: (b, i, 0))


def _rope(y, cos, sin_signed):
    width = y.shape[1]
    reps = width // LANES
    cosw = jnp.concatenate([cos] * reps, axis=1) if reps > 1 else cos
    sinw = jnp.concatenate([sin_signed] * reps, axis=1) if reps > 1 else sin_signed
    lane = lax.broadcasted_iota(jnp.int32, y.shape, 1)
    first_half = (lane % HEAD_DIM) < (HEAD_DIM // 2)
    partner = jnp.where(first_half, pltpu.roll(y, width - HEAD_DIM // 2, axis=1), pltpu.roll(y, HEAD_DIM // 2, axis=1))
    return y * cosw + partner * sinw


def _kv_variants(a, ones_lane=False):
    lane = lax.broadcasted_iota(jnp.int32, a.shape, 1)
    low = lane < HEAD_DIM
    swapped = pltpu.roll(a, HEAD_DIM, axis=1)
    pad_hi = jnp.where(lane == HEAD_DIM, 1.0, 0.0) if ones_lane else jnp.zeros_like(a)
    pad_lo = jnp.where(lane == 0, 1.0, 0.0) if ones_lane else jnp.zeros_like(a)
    return jnp.concatenate([
        jnp.where(low, a, pad_hi),
        jnp.where(low, pad_lo, swapped),
        jnp.where(low, swapped, pad_hi),
        jnp.where(low, pad_lo, a),
    ], axis=1)


def _inproj_even_kernel(*refs, rope):
    if rope:
        (x_ref, mod_ref, g_ref, wqkv_ref, wm_ref, wmt_ref, wgt_ref, qg_ref, kg_ref, gbt_ref, ones_ref, cos_ref, sin_ref,
         q_out, kk_out, vv_out, k_out, v_out, mqt_out, mk_out, mvt_out, mo_out, gatet_out) = refs
    else:
        (x_ref, mod_ref, g_ref, wqkv_ref, wm_ref, wmt_ref, wgt_ref, qg_ref, kg_ref, gbt_ref, ones_ref,
         q_out, kk_out, vv_out, k_out, v_out, mqt_out, mk_out, mvt_out, mo_out, gatet_out) = refs
    h = _rms(x_ref[...], g_ref[...]) * (1.0 + mod_ref[1]) + mod_ref[0]
    hb = h.astype(BF16)

    qkv = _dot(hb, wqkv_ref[...])
    w = MLSTM_WIDTH
    mm = _dot(hb, wm_ref[...])
    mk_out[...] = mm[:, :w].astype(BF16)
    mo_out[...] = mm[:, w:]
    q = _group_rms(qkv[:, :ATT_WIDTH], qg_ref[...], ones_ref[...], HEAD_DIM)
    k = _group_rms(qkv[:, ATT_WIDTH:ATT_WIDTH + ATT_KV_WIDTH], kg_ref[...], ones_ref[:ATT_KV_WIDTH, :ATT_KV_WIDTH], HEAD_DIM)
    v = qkv[:, ATT_WIDTH + ATT_KV_WIDTH:]
    if rope:
        q = _rope(q, cos_ref[...], sin_ref[...])
        k = _rope(k, cos_ref[...], sin_ref[...])
    q_out[...] = (q * (HEAD_DIM ** -0.5 * LOG2_E)).astype(BF16)
    k_out[...] = k
    v_out[...] = v
    kk_out[...] = _kv_variants(k).astype(BF16)
    vv_out[...] = _kv_variants(v, ones_lane=True).astype(BF16)

    mmt = _dot_nt(wmt_ref[...], hb)
    mqt_out[...] = (mmt[:w] * MLSTM_HD ** -0.5).astype(BF16)
    mvt_out[...] = mmt[w:].astype(BF16)

    gates_t = _dot_nt(wgt_ref[...], hb) + gbt_ref[...]
    sub = lax.broadcasted_iota(jnp.int32, gates_t.shape, 0)
    gatet_out[...] = jnp.where(sub < N_DIR * MLSTM_HEADS, gates_t, _log_sigmoid(gates_t))


def _inproj_even_call(x, mod, mod_row0, mod_stride, prm, rope_tabs):
    B, T, _ = x.shape
    tm = min(TOKEN_TILE, T)
    rope = rope_tabs is not None
    ins = [x, mod, prm["g_mix_pre"], prm["w_qkv"], prm["w_m"], prm["w_mt"], prm["w_gt"], prm["q_gain"],
           prm["k_gain"], prm["gate_bias_t"], prm["ones64"]]
    in_specs = [_tok_spec(tm, D_MODEL), _mod_spec(mod_row0, mod_stride)] + [_const_spec(a.shape) for a in ins[2:]]
    if rope:
        ins += list(rope_tabs)
        in_specs += [pl.BlockSpec((tm, LANES), lambda b, i: (i, 0))] * 2
    n_gate = 2 * N_DIR * MLSTM_HEADS
    outs = [(ATT_WIDTH, BF16, True), (4 * LANES, BF16, True), (4 * LANES, BF16, True), (ATT_KV_WIDTH, F32, True),
            (ATT_KV_WIDTH, F32, True), (MLSTM_WIDTH, BF16, False), (MLSTM_WIDTH, BF16, True), (MLSTM_WIDTH, BF16, False),
            (MLSTM_WIDTH, F32, True), (n_gate, F32, False)]
    feat_spec = lambda w: pl.BlockSpec((None, w, tm), lambda b, i: (b, 0, i))
    return pl.pallas_call(
        functools.partial(_inproj_even_kernel, rope=rope),
        grid=(B, T // tm),
        in_specs=in_specs,
        out_specs=[_tok_spec(tm, w) if tok else feat_spec(w) for w, _, tok in outs],
        out_shape=[jax.ShapeDtypeStruct((B, T, w) if tok else (B, w, T), dt) for w, dt, tok in outs],
        compiler_params=_cparams("parallel", "parallel"),
        name="inproj_even",
    )(*ins)


def _attn_kernel(*refs, cached):
    if cached:
        q_ref, kk_ref, vv_ref, kc_ref, vc_ref, o_ref = refs
    else:
        q_ref, kk_ref, vv_ref, o_ref = refs
    def value_col(head):
        kv = head // (ATT_HEADS // ATT_KV_HEADS)
        return (2 * kv + head % 2) * LANES

    sources = [(kk_ref, vv_ref)] + ([(kc_ref, vc_ref)] if cached else [])
    den_from_values = sum(k_ref.shape[0] for k_ref, _ in sources) >= ATTN_DEN_ON_MXU_MIN_KEYS

    def scores(head):
        qp = q_ref[:, (head // 2) * LANES:(head // 2 + 1) * LANES]
        col = value_col(head)
        return [_dot_nt(qp, k_ref[:, col:col + LANES]) for k_ref, _ in sources]

    def weighted_values(head, s_blocks):
        col = value_col(head)
        m = None
        for s in s_blocks:
            bm = jnp.max(s, axis=-1, keepdims=True)
            m = bm if m is None else jnp.maximum(m, bm)
        o = None
        den = None
        for s, (_, v_ref) in zip(s_blocks, sources):
            p = jnp.exp2(s - m)
            if not den_from_values:
                bd = jnp.sum(p, axis=-1, keepdims=True)
                den = bd if den is None else den + bd
            bo = _dot(p.astype(BF16), v_ref[:, col:col + LANES])
            o = bo if o is None else o + bo
        own_low = head % 2 == 0
        if den_from_values:
            den_lane = HEAD_DIM if own_low else 0
            den = o[:, den_lane:den_lane + 1]
        lane = lax.broadcasted_iota(jnp.int32, o.shape, 1)
        own = (lane < HEAD_DIM) if own_low else (lane >= HEAD_DIM)
        return jnp.where(own, o / den, 0.0)

    pending = [scores(h) for h in range(ATTN_SCORES_AHEAD)]
    acc = None
    for head in range(ATT_HEADS):
        if head + ATTN_SCORES_AHEAD < ATT_HEADS:
            pending.append(scores(head + ATTN_SCORES_AHEAD))
        o = weighted_values(head, pending[head])
        if head % 2 == 0:
            acc = o
        else:
            pair = head // 2
            o_ref[:, pair * LANES:(pair + 1) * LANES] = (acc + o).astype(o_ref.dtype)


def _attn_call(q, kk, vv, cache=None):
    B, T, _ = q.shape
    tq = min(ATTN_Q_TILE, T)
    cached = cache is not None
    ins = [q, kk, vv]
    full = lambda n: pl.BlockSpec((None, n, 4 * LANES), lambda b, i: (b, 0, 0))
    in_specs = [_tok_spec(tq, ATT_WIDTH), full(T), full(T)]
    if cached:
        ins += list(cache)
        in_specs += [full(cache[0].shape[1])] * 2
    return pl.pallas_call(
        functools.partial(_attn_kernel, cached=cached),
        grid=(B, T // tq),
        in_specs=in_specs,
        out_specs=_tok_spec(tq, ATT_WIDTH),
        out_shape=jax.ShapeDtypeStruct((B, T, ATT_WIDTH), BF16),
        compiler_params=_cparams("parallel", "parallel"),
        name="attention",
    )(*ins)


def _mlstm_kernel(*refs, zero_init):
    if zero_init:
        (qf_ref, kf_ref, vf_ref, gtf_ref, qb_ref, kb_ref, vb_ref, gtb_ref, trit_ref,
         hf_ref, hb_ref, c_ref, n_ref, m_ref) = refs
    else:
        (qf_ref, kf_ref, vf_ref, gtf_ref, qb_ref, kb_ref, vb_ref, gtb_ref, trit_ref,
         c0_ref, n0_ref, m0_ref, hf_ref, hb_ref, c_ref, n_ref, m_ref) = refs
    L = MLSTM_CHUNK
    H = MLSTM_HEADS
    HD = MLSTM_HD
    PAD = 2 * SUBLANES
    chunk = pl.program_id(1)

    @pl.when(chunk == 0)
    def _init():
        if zero_init:
            c_ref[...] = jnp.zeros(c_ref.shape, F32)
            n_ref[...] = jnp.zeros(n_ref.shape, F32)
            m_ref[...] = jnp.full(m_ref.shape, M_INIT, F32)
        else:
            for d in range(N_DIR):
                for hd in range(H):
                    c_ref[d, hd] = c0_ref[d, hd].T
            n_ref[...] = n0_ref[...]
            m_ref[...] = m0_ref[...]

    m_all = m_ref[...]
    n_all = n_ref[...]
    m_row = m_all
    m_lane = lax.broadcasted_iota(jnp.int32, m_all.shape, 1)
    key = lax.broadcasted_iota(jnp.int32, (L, L), 0)
    qry = lax.broadcasted_iota(jnp.int32, (L, L), 1)
    dirs = ((qf_ref, kf_ref, vf_ref, gtf_ref, hf_ref, key <= qry, L - 1),
            (qb_ref, kb_ref, vb_ref, gtb_ref, hb_ref, key >= qry, 0))
    n_in = N_DIR * H
    order = [(d, hd) for d in range(N_DIR) for hd in range(H)]

    def query_stage(d, hd):
        qt_ref, k_ref = dirs[d][0], dirs[d][1]
        j = d * H + hd
        sl = slice(hd * HD, (hd + 1) * HD)
        n_blk = jnp.broadcast_to(n_all[j:j + 1, :], (PAD, HD)).astype(BF16)
        return _dot(jnp.concatenate([k_ref[:, sl], c_ref[d, hd].astype(BF16), n_blk], axis=0), qt_ref[sl, :])

    sums = []
    for d, (_, _, _, gt_ref, _, _, _) in enumerate(dirs):
        gates_t = gt_ref[...]
        bcum_t = _dot(jnp.concatenate(_split3(gates_t), axis=1), trit_ref[d])
        sums.append((gates_t, bcum_t))
    c_rows = [sums[d][0][d * H:(d + 1) * H] - sums[d][1][n_in + d * H:n_in + (d + 1) * H] for d in range(N_DIR)]
    c_cols = jnp.concatenate(c_rows, axis=0).T

    def first_stage(d, hd, r):
        _, k_ref, vt_ref, _, _, mask, last = dirs[d]
        gates_t, bcum_t = sums[d]
        j = d * H + hd
        jf = n_in + j
        sl = slice(hd * HD, (hd + 1) * HD)
        k = k_ref[:, sl]
        v_t = vt_ref[sl, :]
        b_row = bcum_t[jf:jf + 1, :]
        i_row = gates_t[j:j + 1, :]
        c_col = c_cols[:, j:j + 1]
        m_prev = m_all[0:1, j:j + 1]
        ct_prev = c_ref[d, hd]
        n_prev = n_all[j:j + 1, :]

        g = b_row + m_prev
        dlog = jnp.where(mask, c_col + b_row, -jnp.inf)
        m_t = jnp.maximum(g, jnp.max(dlog, axis=0, keepdims=True))
        w = jnp.exp(dlog - m_t)
        w_inter = jnp.exp(g - m_t)

        b_last = bcum_t[jf:jf + 1, last:last + 1]
        wlog = b_last - b_row + i_row
        m_new = jnp.maximum(b_last + m_prev, jnp.max(wlog, axis=1, keepdims=True))
        ws = jnp.exp(wlog - m_new)
        decay = jnp.exp(b_last + m_prev - m_new)
        vw = (v_t.astype(F32) * ws).astype(BF16)
        ws_blk = jnp.broadcast_to(ws, (PAD, L)).astype(BF16)
        upd = _dot(jnp.concatenate([vw, ws_blk], axis=0), k)
        c_new = decay * ct_prev + upd[:HD]
        n_new = decay * n_prev + upd[HD:HD + 1]
        return dict(d=d, hd=hd, j=j, r=r, w=w, w_inter=w_inter, m_t=m_t, v_t=v_t, c_new=c_new, n_new=n_new, m_new=m_new)

    def second_stage(s):
        h_ref = dirs[s["d"]][4]
        sl = slice(s["hd"] * HD, (s["hd"] + 1) * HD)
        r = s["r"]
        qk = r[:L] * s["w"]
        num = _dot(s["v_t"], qk.astype(BF16)) + s["w_inter"] * r[L:L + HD]
        den = jnp.sum(qk, axis=0, keepdims=True) + s["w_inter"] * r[L + HD:L + HD + 1]
        h_ref[:, sl] = (num / jnp.maximum(jnp.abs(den), jnp.exp(-s["m_t"]))).T

    staged = []
    for idx, (d, hd) in enumerate(order):
        staged.append(first_stage(d, hd, query_stage(d, hd)))
        if idx >= MLSTM_STAGE_LAG:
            second_stage(staged[idx - MLSTM_STAGE_LAG])
    for s in staged[len(order) - MLSTM_STAGE_LAG:]:
        second_stage(s)
    for s in staged:
        c_ref[s["d"], s["hd"]] = s["c_new"]
        m_row = jnp.where(m_lane == s["j"], s["m_new"], m_row)
    n_ref[...] = jnp.concatenate([s["n_new"] for s in staged], axis=0)
    m_ref[...] = m_row

    @pl.when(chunk == pl.num_programs(1) - 1)
    def _finish():
        for d in range(N_DIR):
            for hd in range(H):
                c_ref[d, hd] = c_ref[d, hd].T


def _mlstm_call(mq_t, mk, mv_t, gates_t, tri_t, state0):
    B, T, _ = mk.shape
    L = MLSTM_CHUNK
    nc = T // L
    zero_init = state0 is None
    fwd = lambda w: pl.BlockSpec((None, L, w), lambda b, c: (b, c, 0))
    bwd = lambda w: pl.BlockSpec((None, L, w), lambda b, c: (b, nc - 1 - c, 0))
    fwd_t = lambda w: pl.BlockSpec((None, w, L), lambda b, c: (b, 0, c))
    bwd_t = lambda w: pl.BlockSpec((None, w, L), lambda b, c: (b, 0, nc - 1 - c))
    W = MLSTM_WIDTH
    NG = 2 * N_DIR * MLSTM_HEADS
    c_spec = pl.BlockSpec((None, N_DIR, MLSTM_HEADS, MLSTM_HD, MLSTM_HD), lambda b, c: (b, 0, 0, 0, 0))
    n_spec = pl.BlockSpec((None, N_DIR * MLSTM_HEADS, MLSTM_HD), lambda b, c: (b, 0, 0))
    m_spec = pl.BlockSpec((None, 1, LANES), lambda b, c: (b, 0, 0))
    ins = [mq_t, mk, mv_t, gates_t, mq_t, mk, mv_t, gates_t, tri_t]
    in_specs = [fwd_t(W), fwd(W), fwd_t(W), fwd_t(NG), bwd_t(W), bwd(W), bwd_t(W), bwd_t(NG), _const_spec(tri_t.shape)]
    if not zero_init:
        ins += list(state0)
        in_specs += [c_spec, n_spec, m_spec]
    return pl.pallas_call(
        functools.partial(_mlstm_kernel, zero_init=zero_init),
        grid=(B, nc),
        in_specs=in_specs,
        out_specs=[fwd(W), bwd(W), c_spec, n_spec, m_spec],
        out_shape=[jax.ShapeDtypeStruct((B, T, W), F32), jax.ShapeDtypeStruct((B, T, W), F32),
                   jax.ShapeDtypeStruct((B, N_DIR, MLSTM_HEADS, MLSTM_HD, MLSTM_HD), F32),
                   jax.ShapeDtypeStruct((B, N_DIR * MLSTM_HEADS, MLSTM_HD), F32),
                   jax.ShapeDtypeStruct((B, 1, LANES), F32)],
        compiler_params=_cparams("parallel", "arbitrary"),
        name="mlstm",
    )(*ins)


def _residual_ffn(x_ref, mix_fn, mod_ref, gpost_ref, gfpre_ref, gfpost_ref, wout_ref, w1_ref, w2_ref, o_ref):
    half = D_MODEL // 2
    sub = min(POST_SUBTILE, x_ref.shape[0])
    rows = [slice(r0, r0 + sub) for r0 in range(0, x_ref.shape[0], sub)]
    ys = []
    for r in rows:
        mix_a, mix_b = mix_fn(r)
        ys.append(_dot(mix_a.astype(BF16), wout_ref[:half, :]) + _dot(mix_b.astype(BF16), wout_ref[half:, :]))
    for r, y in zip(rows, ys):
        x1 = x_ref[r, :] + mod_ref[2] * _rms(y, gpost_ref[...])
        h2 = (_rms(x1, gfpre_ref[...]) * (1.0 + mod_ref[4]) + mod_ref[3]).astype(BF16)
        chunks = list(range(0, D_FF, FF_CHUNK))
        up = _dot(h2, w1_ref[:, chunks[0]:chunks[0] + FF_CHUNK])
        f = None
        for i, c0 in enumerate(chunks):
            cur = up
            if i + 1 < len(chunks):
                up = _dot(h2, w1_ref[:, chunks[i + 1]:chunks[i + 1] + FF_CHUNK])
            part = _dot(jnp.square(jnp.maximum(cur, 0.0)).astype(BF16), w2_ref[c0:c0 + FF_CHUNK, :])
            f = part if f is None else f + part
        o_ref[r, :] = x1 + mod_ref[5] * _rms(f, gfpost_ref[...])


def _head_rms(y, gain_ref, width):
    outs = []
    for h0 in range(0, y.shape[1], width):
        seg = y[:, h0:h0 + width]
        outs.append(seg * lax.rsqrt(jnp.mean(seg * seg, axis=-1, keepdims=True) + EPS) * gain_ref[:, h0:h0 + width])
    return jnp.concatenate(outs, axis=1)


def _post_even_kernel(x_ref, attn_ref, hf_ref, hb_ref, mo_ref, mod_ref, gm_ref, gpost_ref, gfpre_ref, gfpost_ref,
                      wout_ref, w1_ref, w2_ref, o_ref):
    def mix(r):
        hm = _head_rms(hf_ref[r, :] + hb_ref[r, :], gm_ref, MLSTM_HD) * jax.nn.sigmoid(mo_ref[r, :])
        return attn_ref[r, :], hm

    _residual_ffn(x_ref, mix, mod_ref, gpost_ref, gfpre_ref, gfpost_ref, wout_ref, w1_ref, w2_ref, o_ref)


def _post_odd_kernel(x_ref, lru_ref, rg_ref, of_ref, ob_ref, gg_ref, mod_ref, gm_ref, gpost_ref, gfpre_ref, gfpost_ref,
                     wout_ref, w1_ref, w2_ref, o_ref):
    def mix(r):
        lru = jax.nn.gelu(rg_ref[r, :]) * lru_ref[r, :]
        og = _head_rms(of_ref[r, :] + ob_ref[r, :], gm_ref, GLA_DV) * jax.nn.silu(gg_ref[r, :])
        return lru, og

    _residual_ffn(x_ref, mix, mod_ref, gpost_ref, gfpre_ref, gfpost_ref, wout_ref, w1_ref, w2_ref, o_ref)


def _post_call(kernel_fn, name, x, mix_ins, mod, mod_row0, mod_stride, consts):
    shape = x.shape
    if mod_stride == 0:
        x = x.reshape(1, -1, D_MODEL)
        mix_ins = [a.reshape(1, -1, a.shape[2]) for a in mix_ins]
    B, T, _ = x.shape
    tm = min(POST_TILE, T)
    ins = [x] + list(mix_ins) + [mod] + list(consts)
    in_specs = ([_tok_spec(tm, D_MODEL)] + [_tok_spec(tm, a.shape[2]) for a in mix_ins] + [_mod_spec(mod_row0, mod_stride)]
                + [_const_spec(a.shape) for a in consts])
    return pl.pallas_call(
        kernel_fn,
        grid=(B, T // tm),
        in_specs=in_specs,
        out_specs=_tok_spec(tm, D_MODEL),
        out_shape=jax.ShapeDtypeStruct(x.shape, F32),
        compiler_params=_cparams("parallel", "parallel"),
        name=name,
    )(*ins).reshape(shape)


def _inproj_odd_kernel(x_ref, xp_ref, xn_ref, mod_ref, g_ref, wmain_ref, wlr_in_ref, wlr_hi_ref, wlr_lo_ref, blr_ref,
                       cw_ref, cb_ref, wg_ref, bg_ref, lam_ref,
                       rg_out, gq_out, gk_out, gv_out, gg_out, la_out, af_out, uf_out, ab_out, ub_out):
    tm = x_ref.shape[0]
    W = LRU_WIDTH
    sub = min(ODD_GROUP_ROWS, tm)
    n_groups = tm // sub
    m = sub + 2 * CONV_HALO
    first = pl.program_id(1) == 0
    last = pl.program_id(1) == pl.num_programs(1) - 1
    rid = lax.broadcasted_iota(jnp.int32, (m, W), 0)
    neg_half_c_sp = (-0.5 * LRU_C) * _softplus(-lam_ref[...])

    def modulated(xv):
        return (_rms(xv, g_ref[...]) * (1.0 + mod_ref[1]) + mod_ref[0]).astype(BF16)

    for q in range(n_groups):
        rows = slice(q * sub, (q + 1) * sub)
        prev = xp_ref[...] if q == 0 else x_ref[q * sub - CONV_HALO:q * sub, :]
        nxt = xn_ref[...] if q == n_groups - 1 else x_ref[(q + 1) * sub:(q + 1) * sub + CONV_HALO, :]
        hb_ext = modulated(jnp.concatenate([prev, x_ref[rows, :], nxt], axis=0))
        hb = hb_ext[CONV_HALO:CONV_HALO + sub]

        glr_hi, glr_lo = _split2(_dot(hb, wlr_in_ref[...]))

        rx = _dot(hb_ext, wmain_ref[:, :W])
        outside = None
        if q == 0:
            outside = first & (rid < CONV_HALO)
        if q == n_groups - 1:
            tail = last & (rid >= sub + CONV_HALO)
            outside = tail if outside is None else outside | tail
        if outside is not None:
            rx = jnp.where(outside, 0.0, rx)

        mm = _dot(hb, wmain_ref[:, W:3 * W])
        rg_out[rows, :] = mm[:, :W]
        gq_out[rows, :] = mm[:, W:W + GLA_KW] * GLA_DK ** -0.5
        gk_out[rows, :] = mm[:, W + GLA_KW:]

        z = _dot(glr_hi, wlr_hi_ref[...]) + (_dot(glr_hi, wlr_lo_ref[...]) + _dot(glr_lo, wlr_hi_ref[...])) + blr_ref[...]
        la_out[rows, :] = _log_sigmoid(z) * (1.0 / GLA_TAU)

        xc = cb_ref[...] + rx * cw_ref[CONV_LEFT:CONV_LEFT + 1, :]
        for tap in range(CONV_W):
            off = tap - CONV_LEFT
            if off != 0:
                xc = xc + pltpu.roll(rx, (-off) % m, axis=0) * cw_ref[tap:tap + 1, :]
        xc = xc[CONV_HALO:CONV_HALO + sub, :]
        xcb = xc.astype(BF16)
        half_xc = 0.5 * xc
        for d, (a_out, u_out) in enumerate(((af_out, uf_out), (ab_out, ub_out))):
            cols = slice(2 * d * W, (2 * d + 2) * W)
            th = jnp.tanh(_dot(xcb, wg_ref[:, cols]) + bg_ref[:, cols])
            if d == 0:
                gv_out[rows, :] = _dot(hb, wmain_ref[:, 3 * W:4 * W]).astype(BF16)
            else:
                gg_out[rows, :] = _dot(hb, wmain_ref[:, 4 * W:])
            th_r = th[:, :W]
            th_i = th[:, W:]
            log_a = neg_half_c_sp[d:d + 1, :] * th_r + neg_half_c_sp[d:d + 1, :]
            a = jnp.exp(log_a)
            a_out[rows, :] = a
            v = jnp.tanh(log_a) * (-1.0 - a * a)
            u_out[rows, :] = jnp.where(v > 0.0, v * lax.rsqrt(v), 0.0) * (half_xc * th_i + half_xc)


def _inproj_odd_call(x, mod, mod_row0, mod_stride, prm, lru_prm):
    B, T, _ = x.shape
    tm = min(TOKEN_TILE, T)
    halo_per_tile = tm // CONV_HALO
    n_halo = T // CONV_HALO
    prev_spec = pl.BlockSpec((None, CONV_HALO, D_MODEL), lambda b, i: (b, jnp.maximum(i * halo_per_tile - 1, 0), 0))
    next_spec = pl.BlockSpec((None, CONV_HALO, D_MODEL),
                             lambda b, i: (b, jnp.minimum((i + 1) * halo_per_tile, n_halo - 1), 0))
    consts = [prm["g_mix_pre"], prm["w_main"], prm["w_lr_in"], prm["w_lr_hi"], prm["w_lr_lo"], prm["b_lr"],
              lru_prm["conv_w"], lru_prm["conv_b"], lru_prm["w_gate"], lru_prm["b_gate"], lru_prm["lam"]]
    ins = [x, x, x, mod] + consts
    in_specs = ([_tok_spec(tm, D_MODEL), prev_spec, next_spec, _mod_spec(mod_row0, mod_stride)]
                + [_const_spec(a.shape) for a in consts])
    widths = [(LRU_WIDTH, F32), (GLA_KW, F32), (GLA_KW, F32), (GLA_WIDTH, BF16), (GLA_WIDTH, F32),
              (N_DIR * GLA_KW, F32)] + [(LRU_WIDTH, F32)] * 4
    return pl.pallas_call(
        _inproj_odd_kernel,
        grid=(B, T // tm),
        in_specs=in_specs,
        out_specs=[_tok_spec(tm, w) for w, _ in widths],
        out_shape=[jax.ShapeDtypeStruct((B, T, w), dt) for w, dt in widths],
        compiler_params=_cparams("parallel", "parallel"),
        name="inproj_odd",
    )(*ins)


def _lru_kernel(*refs, zero_init):
    if zero_init:
        af_s, uf_s, ab_s, ub_s, o_ref, last_ref, hb_s = refs
    else:
        af_s, uf_s, ab_s, ub_s, h0_ref, o_ref, last_ref, hb_s = refs
    T, W = af_s.shape
    nblk = T // SUBLANES
    rid8 = lax.broadcasted_iota(jnp.int32, (SUBLANES, W), 0)

    def block_scan(a, u, reverse):
        for dist in (1, 2, 4):
            if reverse:
                keep = rid8 < SUBLANES - dist
                shift = SUBLANES - dist
            else:
                keep = rid8 >= dist
                shift = dist
            a_n = jnp.where(keep, pltpu.roll(a, shift, axis=0), 1.0)
            u_n = jnp.where(keep, pltpu.roll(u, shift, axis=0), 0.0)
            u = a * u_n + u
            a = a * a_n
        return a, u

    def body(blk, carry):
        hf, hb = carry
        rf = pl.multiple_of(blk * SUBLANES, SUBLANES)
        a, u = block_scan(af_s[pl.ds(rf, SUBLANES), :], uf_s[pl.ds(rf, SUBLANES), :], False)
        out_f = u + a * hf
        o_ref[pl.ds(rf, SUBLANES), :] = out_f
        rb = pl.multiple_of((nblk - 1 - blk) * SUBLANES, SUBLANES)
        a, u = block_scan(ab_s[pl.ds(rb, SUBLANES), :], ub_s[pl.ds(rb, SUBLANES), :], True)
        out_b = u + a * hb
        hb_s[pl.ds(rb, SUBLANES), :] = out_b
        return out_f[SUBLANES - 1:SUBLANES, :], out_b[0:1, :]

    if zero_init:
        init = (jnp.zeros((1, W), F32), jnp.zeros((1, W), F32))
    else:
        init = (h0_ref[0:1, :], h0_ref[1:2, :])
    hf, hb = lax.fori_loop(0, nblk, body, init)
    last_ref[0:1, :] = hf
    last_ref[1:2, :] = hb
    o_ref[...] = o_ref[...] + hb_s[...]


def _lru_call(coeffs, h0):
    B, T, W = coeffs[0].shape
    zero_init = h0 is None
    seq = pl.BlockSpec((None, T, W), lambda b: (b, 0, 0))
    st = pl.BlockSpec((None, N_DIR, W), lambda b: (b, 0, 0))
    ins = list(coeffs)
    in_specs = [seq] * len(ins)
    if not zero_init:
        ins.append(h0)
        in_specs.append(st)
    return pl.pallas_call(
        functools.partial(_lru_kernel, zero_init=zero_init),
        grid=(B,),
        in_specs=in_specs,
        out_specs=[seq, st],
        out_shape=[jax.ShapeDtypeStruct((B, T, W), F32), jax.ShapeDtypeStruct((B, N_DIR, W), F32)],
        scratch_shapes=[pltpu.VMEM((T, W), F32)],
        compiler_params=_cparams("parallel"),
        name="rglru",
    )(*ins)


GLA_LEVELS = GLA_CHUNK.bit_length() - 1


def _gla_constants():
    L = GLA_CHUNK
    idx = np.arange(L)
    sel = np.zeros((N_DIR, (GLA_LEVELS + 3) * L, L), np.float32)
    lvl_mask = np.zeros((N_DIR, GLA_LEVELS + 1, L, L), np.float32)
    for d in range(N_DIR):
        tri = (idx[None, :] <= idx[:, None]) if d == 0 else (idx[None, :] >= idx[:, None])
        tri = tri.astype(np.float32)
        last = L - 1 if d == 0 else 0
        sel[d, 0:L] = tri
        for lv in range(GLA_LEVELS):
            m = L >> (lv + 1)
            start = (idx // (2 * m)) * (2 * m)
            second = (idx - start) >= m
            boundary = start + (m - 1 if d == 0 else m)
            query_role = second if d == 0 else ~second
            diff = tri - tri[boundary]
            sel[d, (1 + lv) * L:(2 + lv) * L] = np.where(query_role[:, None], diff, -diff)
            same = start[:, None] == start[None, :]
            lvl_mask[d, lv] = same & query_role[:, None] & ~query_role[None, :]
        lvl_mask[d, GLA_LEVELS] = np.eye(L)
        sel[d, (GLA_LEVELS + 1) * L:(GLA_LEVELS + 2) * L] = tri[last][None, :] - tri
        sel[d, (GLA_LEVELS + 2) * L:] = tri[last][None, :]
    assert sel.min() >= 0.0 and sel.max() <= 1.0
    sel3 = np.concatenate([sel] * 3, axis=2)
    lvl_mask = np.tile(lvl_mask, (1, 1, 1, GLA_HEADS))
    return jnp.asarray(sel3, BF16), jnp.asarray(lvl_mask, F32)


def _gla_kernel(*refs, zero_init):
    if zero_init:
        (qf_ref, kf_ref, vf_ref, af_ref, qb_ref, kb_ref, vb_ref, ab_ref, sel_ref, msk_ref, of_ref, ob_ref, s_ref) = refs
    else:
        (qf_ref, kf_ref, vf_ref, af_ref, qb_ref, kb_ref, vb_ref, ab_ref, sel_ref, msk_ref, s0_ref,
         of_ref, ob_ref, s_ref) = refs
    L = GLA_CHUNK
    H = GLA_HEADS
    KW = GLA_KW
    VW = GLA_WIDTH
    DK = GLA_DK
    DV = GLA_DV

    @pl.when(pl.program_id(1) == 0)
    def _init():
        if zero_init:
            s_ref[...] = jnp.zeros(s_ref.shape, F32)
        else:
            s_ref[...] = s0_ref[...]

    zero_k = jnp.zeros((L, KW), BF16)
    zero_v = jnp.zeros((L, VW), BF16)
    lane_k = lax.broadcasted_iota(jnp.int32, (L, KW), 1) // DK
    lane_v = lax.broadcasted_iota(jnp.int32, (L, VW), 1) // DV

    def stack_heads(a, lane_head, zero):
        return jnp.concatenate([jnp.where(lane_head == h, a, zero) for h in range(H)], axis=0)

    def tile_rows(a):
        return jnp.concatenate([a] * H, axis=0)

    dirs = ((qf_ref, kf_ref, vf_ref, af_ref, of_ref), (qb_ref, kb_ref, vb_ref, ab_ref, ob_ref))
    work = []
    for i in range(GLA_STEP_CHUNKS):
        for d, (q_ref, k_ref, v_ref, a_ref, o_ref) in enumerate(dirs):
            j = i if d == 0 else GLA_STEP_CHUNKS - 1 - i
            rows = slice(j * L, (j + 1) * L)
            sums = _dot(sel_ref[d], jnp.concatenate(_split3(a_ref[rows, :]), axis=0))
            q_b = q_ref[rows, :].astype(BF16)
            work.append(dict(d=d, rows=rows, sums=sums, q_b=q_b, q_st=stack_heads(q_b, lane_k, zero_k),
                             k_st=stack_heads(k_ref[rows, :].astype(BF16), lane_k, zero_k)))

    for lv in range(GLA_LEVELS + 1):
        for wk in work:
            if lv == GLA_LEVELS:
                part = _dot_nt(wk["q_b"], wk["k_st"])
            else:
                wl = jnp.exp(wk["sums"][(1 + lv) * L:(2 + lv) * L]).astype(BF16)
                part = _dot_nt(wk["q_b"] * wl, wk["k_st"] * tile_rows(wl))
            part = part * msk_ref[wk["d"], lv]
            wk["scores"] = part if lv == 0 else wk["scores"] + part

    state = [s_ref[d] for d in range(N_DIR)]
    for wk in work:
        d, rows, sums = wk["d"], wk["rows"], wk["sums"]
        _, k_ref, v_ref, _, o_ref = dirs[d]
        v = v_ref[rows, :]
        s_prev = state[d]
        intra = _dot(wk["scores"].astype(BF16), stack_heads(v, lane_v, zero_v))
        q_in = wk["q_st"] * tile_rows(jnp.exp(sums[0:L]).astype(BF16))
        inter = _dot(q_in, s_prev.astype(BF16))
        o_ref[rows, :] = intra + jnp.concatenate([inter[h * L:(h + 1) * L] for h in range(H)], axis=1)

        k_out = k_ref[rows, :] * jnp.exp(sums[(GLA_LEVELS + 1) * L:(GLA_LEVELS + 2) * L])
        dec = jnp.exp(sums[(GLA_LEVELS + 2) * L:])
        both_t = jnp.concatenate([k_out, dec], axis=0).T
        k_out_t = both_t[:, :L].astype(BF16)
        upd = [_dot(k_out_t[h * DK:(h + 1) * DK], v[:, h * DV:(h + 1) * DV]) for h in range(H)]
        state[d] = both_t[:, L:L + 1] * s_prev + jnp.concatenate(upd, axis=0)
    for d in range(N_DIR):
        s_ref[d] = state[d]


def _gla_call(gq, gk, gv, log_a, sel3, lvl_mask, s0):
    B, T, _ = gq.shape
    L = GLA_CHUNK * GLA_STEP_CHUNKS
    nc = T // L
    zero_init = s0 is None
    fwd = lambda w: pl.BlockSpec((None, L, w), lambda b, c: (b, c, 0))
    bwd = lambda w: pl.BlockSpec((None, L, w), lambda b, c: (b, nc - 1 - c, 0))
    a_fwd = pl.BlockSpec((None, L, GLA_KW), lambda b, c: (b, c, 0))
    a_bwd = pl.BlockSpec((None, L, GLA_KW), lambda b, c: (b, nc - 1 - c, 1))
    s_spec = pl.BlockSpec((None, N_DIR, GLA_KW, GLA_DV), lambda b, c: (b, 0, 0, 0))
    ins = [gq, gk, gv, log_a, gq, gk, gv, log_a, sel3, lvl_mask]
    in_specs = [fwd(GLA_KW), fwd(GLA_KW), fwd(GLA_WIDTH), a_fwd, bwd(GLA_KW), bwd(GLA_KW), bwd(GLA_WIDTH), a_bwd,
                _const_spec(sel3.shape), _const_spec(lvl_mask.shape)]
    if not zero_init:
        ins.append(s0)
        in_specs.append(s_spec)
    return pl.pallas_call(
        functools.partial(_gla_kernel, zero_init=zero_init),
        grid=(B, nc),
        in_specs=in_specs,
        out_specs=[fwd(GLA_WIDTH), bwd(GLA_WIDTH), s_spec],
        out_shape=[jax.ShapeDtypeStruct((B, T, GLA_WIDTH), F32), jax.ShapeDtypeStruct((B, T, GLA_WIDTH), F32),
                   jax.ShapeDtypeStruct((B, N_DIR, GLA_KW, GLA_DV), F32)],
        compiler_params=_cparams("parallel", "arbitrary"),
        name="gla",
    )(*ins)


def _block_diag(blocks):
    n, r, c = blocks.shape
    eye = jnp.eye(n, dtype=blocks.dtype)
    return (eye[:, None, :, None] * blocks[:, :, None, :]).reshape(n * r, n * c)


def _pad_cols(a, width):
    return jnp.pad(a, ((0, 0), (0, width - a.shape[1])))


def _rope_tables(n_tokens):
    rows = n_tokens // GRID_W
    row = jnp.repeat(jnp.arange(rows), GRID_W).astype(F32)
    col = jnp.tile(jnp.arange(GRID_W), rows).astype(F32)
    inv = jnp.power(ROPE_THETA, -jnp.arange(ROPE_PAIRS_PER_AXIS, dtype=F32) / ROPE_PAIRS_PER_AXIS)
    ang = jnp.concatenate([row[:, None] * inv, col[:, None] * inv], axis=-1)
    cos, sin = jnp.cos(ang), jnp.sin(ang)
    reps = LANES // HEAD_DIM
    return jnp.tile(jnp.concatenate([cos, cos], axis=-1), (1, reps)), jnp.tile(jnp.concatenate([-sin, sin], axis=-1), (1, reps))


def _cache_variants(cache, ones_lane=False):
    z = jnp.zeros_like(cache[:, 0])
    if ones_lane:
        z = z.at[..., 0].set(1.0)
    return jnp.concatenate([cache[:, 0], z, z, cache[:, 0], cache[:, 1], z, z, cache[:, 1]], axis=-1).astype(BF16)


def _mlstm_tri():
    idx = np.arange(MLSTM_CHUNK)
    upper = idx[:, None] <= idx[None, :]
    tri_t = np.stack([upper, upper.T]).astype(np.float32)
    return jnp.asarray(np.concatenate([tri_t] * 3, axis=1), BF16)


def kernel(x_prompt, x_sample, cache_attn_k, cache_attn_v, state_mlstm_C, state_mlstm_n, state_mlstm_m, state_lru_h, state_gla_S, c, c_ctx, ada_w, ada_b, norm_mix_pre, norm_mix_post, norm_ffn_pre, norm_ffn_post, w_out, ffn_w1, ffn_w2, w_in_even, attn_q_norm, attn_k_norm, mlstm_i_bias, mlstm_f_bias, mlstm_norm, w_in_odd, lru_conv_w, lru_conv_b, lru_w_r, lru_b_r, lru_w_i, lru_b_i, lru_lambda, gla_w_lr, gla_b_lr, gla_norm):
    depth = ada_w.shape[0]
    Bp = x_prompt.shape[0]
    Bs = x_sample.shape[0]
    row = lambda a: a.reshape(1, -1)

    n_rows = -(-(1 + Bs) // SUBLANES) * SUBLANES
    cvec = jnp.zeros((n_rows, D_MODEL), F32).at[0].set(c_ctx).at[1:1 + Bs].set(c)
    mod_all = _ada_call(cvec, ada_w, ada_b).reshape(depth, n_rows, 6, 1, D_MODEL)

    rope_tabs = _rope_tables(x_sample.shape[1])
    ones64 = _block_diag(jnp.ones((ATT_HEADS, HEAD_DIM, HEAD_DIM), BF16))
    tri_t = _mlstm_tri()
    sel3, lvl_mask = _gla_constants()

    xp, xs = x_prompt, x_sample
    outs = {}
    for l in range(depth):
        mod = mod_all[l]
        tail = [row(norm_mix_post[l]), row(norm_ffn_pre[l]), row(norm_ffn_post[l]),
                w_out[l].astype(BF16), ffn_w1[l].astype(BF16), ffn_w2[l].astype(BF16)]
        if l % 2 == 0:
            e = l // 2
            w_in = w_in_even[e]
            o1 = ATT_WIDTH + 2 * ATT_KV_WIDTH
            o2 = o1 + 4 * MLSTM_WIDTH
            prm = {
                "g_mix_pre": row(norm_mix_pre[l]),
                "w_qkv": w_in[:, :o1].astype(BF16),
                "w_m": jnp.concatenate([w_in[:, o1 + MLSTM_WIDTH:o1 + 2 * MLSTM_WIDTH], w_in[:, o1 + 3 * MLSTM_WIDTH:o2]],
                                       axis=1).astype(BF16),
                "w_mt": jnp.concatenate([w_in[:, o1:o1 + MLSTM_WIDTH], w_in[:, o1 + 2 * MLSTM_WIDTH:o1 + 3 * MLSTM_WIDTH]],
                                        axis=1).T.astype(BF16),
                "w_gt": w_in[:, o2:].T.astype(BF16),
                "gate_bias_t": jnp.concatenate([mlstm_i_bias[e].reshape(-1), mlstm_f_bias[e].reshape(-1)]).reshape(-1, 1),
                "q_gain": row(jnp.tile(attn_q_norm[e], ATT_HEADS)),
                "k_gain": row(jnp.tile(attn_k_norm[e], ATT_KV_HEADS)),
                "ones64": ones64,
            }
            consts = [row(mlstm_norm[e])] + tail
            for path in ("prompt", "sample"):
                if path == "prompt":
                    x, r0, rs, tabs, cache, st0 = xp, 0, 0, None, None, None
                else:
                    x, r0, rs, tabs = xs, 1, 1, rope_tabs
                    cache = (_cache_variants(cache_attn_k[:, e]), _cache_variants(cache_attn_v[:, e], ones_lane=True))
                    st0 = (state_mlstm_C[:, e], state_mlstm_n[:, e].reshape(Bs, N_DIR * MLSTM_HEADS, MLSTM_HD),
                           _pad_cols(state_mlstm_m[:, e].reshape(Bs, -1), LANES).reshape(Bs, 1, LANES))
                q, kk, vv, k_n, v_n, mq_t, mk, mv_t, mo, gates_t = _inproj_even_call(x, mod, r0, rs, prm, tabs)
                attn = _attn_call(q, kk, vv, cache)
                hf, hb, c_fin, n_fin, m_fin = _mlstm_call(mq_t, mk, mv_t, gates_t, tri_t, st0)
                x_new = _post_call(_post_even_kernel, "post_even", x, [attn, hf, hb, mo], mod, r0, rs, consts)
                if path == "prompt":
                    xp = x_new
                    T = x.shape[1]
                    outs.setdefault("k", []).append(jnp.swapaxes(k_n.reshape(Bp, T, ATT_KV_HEADS, HEAD_DIM), 1, 2))
                    outs.setdefault("v", []).append(jnp.swapaxes(v_n.reshape(Bp, T, ATT_KV_HEADS, HEAD_DIM), 1, 2))
                    outs.setdefault("C", []).append(c_fin)
                    outs.setdefault("n", []).append(n_fin.reshape(Bp, N_DIR, MLSTM_HEADS, MLSTM_HD))
                    outs.setdefault("m", []).append(m_fin[:, 0, :N_DIR * MLSTM_HEADS].reshape(Bp, N_DIR, MLSTM_HEADS))
                else:
                    xs = x_new
        else:
            o = l // 2
            w_in = w_in_odd[o]
            o1 = 2 * LRU_WIDTH + 2 * GLA_KW + 2 * GLA_WIDTH
            w_lr = _block_diag(gla_w_lr[o])
            w_lr = jnp.pad(w_lr, ((0, LANES - w_lr.shape[0]), (0, 0)))
            w_lr_hi = w_lr.astype(BF16)
            prm = {
                "g_mix_pre": row(norm_mix_pre[l]),
                "w_main": w_in[:, :o1].astype(BF16),
                "w_lr_in": _pad_cols(w_in[:, o1:], LANES).astype(BF16),
                "w_lr_hi": w_lr_hi,
                "w_lr_lo": (w_lr - w_lr_hi.astype(F32)).astype(BF16),
                "b_lr": row(gla_b_lr[o]),
            }
            lru_prm = {
                "conv_w": lru_conv_w[o],
                "conv_b": row(lru_conv_b[o]),
                "w_gate": (0.5 * jnp.concatenate([_block_diag(lru_w_r[o, 0]), _block_diag(lru_w_i[o, 0]),
                                                  _block_diag(lru_w_r[o, 1]), _block_diag(lru_w_i[o, 1])], axis=1)).astype(BF16),
                "b_gate": 0.5 * row(jnp.stack([lru_b_r[o, 0], lru_b_i[o, 0], lru_b_r[o, 1], lru_b_i[o, 1]])),
                "lam": lru_lambda[o],
            }
            consts = [row(gla_norm[o])] + tail
            for path in ("prompt", "sample"):
                if path == "prompt":
                    x, r0, rs, h0, s0 = xp, 0, 0, None, None
                else:
                    x, r0, rs, h0 = xs, 1, 1, state_lru_h[:, o]
                    s0 = state_gla_S[:, o].reshape(Bs, N_DIR, GLA_KW, GLA_DV)
                rg, gq, gk, gv, gg, log_a, *lru_coeffs = _inproj_odd_call(x, mod, r0, rs, prm, lru_prm)
                lru, h_last = _lru_call(lru_coeffs, h0)
                of, ob, s_fin = _gla_call(gq, gk, gv, log_a, sel3, lvl_mask, s0)
                x_new = _post_call(_post_odd_kernel, "post_odd", x, [lru, rg, of, ob, gg], mod, r0, rs, consts)
                if path == "prompt":
                    xp = x_new
                    outs.setdefault("h", []).append(h_last)
                    outs.setdefault("S", []).append(s_fin.reshape(Bp, N_DIR, GLA_HEADS, GLA_DK, GLA_DV))
                else:
                    xs = x_new

    stack = lambda name: jnp.stack(outs[name], axis=1)
    return (xp, xs, stack("k"), stack("v"), stack("C"), stack("n"), stack("m"), stack("h"), stack("S"))
```

```python
import functools

import numpy as np
import jax
import jax.numpy as jnp
from jax import lax
from jax.experimental import pallas as pl
from jax.experimental.pallas import tpu as pltpu

F32 = jnp.float32
BF16 = jnp.bfloat16

D_MODEL = 1024
D_FF = 4 * D_MODEL
GRID_W = 64
EPS = 1e-6
LOG2_E = 1.4426950408889634
N_DIR = 2

ATT_HEADS = 8
ATT_KV_HEADS = 2
HEAD_DIM = 64
ATT_WIDTH = ATT_HEADS * HEAD_DIM
ATT_KV_WIDTH = ATT_KV_HEADS * HEAD_DIM
ROPE_THETA = 10000.0
ROPE_PAIRS_PER_AXIS = HEAD_DIM // 4

MLSTM_HEADS = 4
MLSTM_HD = 128
MLSTM_WIDTH = MLSTM_HEADS * MLSTM_HD
MLSTM_CHUNK = 256
MLSTM_STAGE_LAG = 2
M_INIT = -1e30

LRU_WIDTH = 512
LRU_BLOCKS = 8
LRU_BD = LRU_WIDTH // LRU_BLOCKS
LRU_C = 8.0
CONV_W = 4
CONV_LEFT = 2

GLA_HEADS = 4
GLA_DK = 64
GLA_DV = 128
GLA_KW = GLA_HEADS * GLA_DK
GLA_WIDTH = GLA_HEADS * GLA_DV
GLA_RANK = 16
GLA_TAU = 16.0
GLA_CHUNK = 64
GLA_STEP_CHUNKS = 4

LANES = 128
SUBLANES = 8
VMEM_LIMIT = 56 * 1024 * 1024

TOKEN_TILE = 512
ATTN_Q_TILE = 256
ATTN_SCORES_AHEAD = 1
ATTN_DEN_ON_MXU_MIN_KEYS = 1024
FF_CHUNK = 1024
POST_TILE = 512
POST_SUBTILE = 256
ODD_GROUP_ROWS = 512
CONV_HALO = 2 * SUBLANES


def _cparams(*sem):
    return pltpu.CompilerParams(dimension_semantics=sem, vmem_limit_bytes=VMEM_LIMIT)


def _const_spec(shape):
    n = len(shape)
    return pl.BlockSpec(shape, lambda *_: (0,) * n, pipeline_mode=pl.Buffered(1))


def _dot(a, b):
    return jnp.dot(a, b, preferred_element_type=F32)


def _dot_nt(a, b):
    return lax.dot_general(a, b, (((1,), (1,)), ((), ())), preferred_element_type=F32)


def _dot_tn(a, b):
    return lax.dot_general(a, b, (((0,), (0,)), ((), ())), preferred_element_type=F32)


def _split3(x):
    hi = x.astype(BF16)
    r = x - hi.astype(F32)
    mid = r.astype(BF16)
    lo = (r - mid.astype(F32)).astype(BF16)
    return hi, mid, lo


def _split2(x):
    hi = x.astype(BF16)
    lo = (x - hi.astype(F32)).astype(BF16)
    return hi, lo


def _rms(x, g):
    return x * lax.rsqrt(jnp.mean(x * x, axis=-1, keepdims=True) + EPS) * g


def _log_sigmoid(x):
    return jnp.minimum(x, 0.0) - jnp.log1p(jnp.exp(-jnp.abs(x)))


def _softplus(x):
    return jnp.maximum(x, 0.0) + jnp.log1p(jnp.exp(-jnp.abs(x)))


def _group_rms(y, gain, ones_bd, width):
    hi, lo = _split2(y * y)
    ss = _dot(hi, ones_bd) + _dot(lo, ones_bd)
    return y * lax.rsqrt(ss * (1.0 / width) + EPS) * gain


def _ada_kernel(c_ref, w_ref, b_ref, o_ref):
    s = jax.nn.silu(c_ref[...])
    o_ref[...] = _dot(s.astype(BF16), w_ref[...].astype(BF16)) + b_ref[...]


def _ada_call(cvec, ada_w, ada_b):
    depth = ada_w.shape[0]
    rows = cvec.shape[0]
    n_col = ada_w.shape[2] // D_MODEL
    return pl.pallas_call(
        _ada_kernel,
        grid=(depth, n_col),
        in_specs=[
            pl.BlockSpec((rows, D_MODEL), lambda l, j: (0, 0)),
            pl.BlockSpec((None, D_MODEL, D_MODEL), lambda l, j: (l, 0, j)),
            pl.BlockSpec((None, 1, D_MODEL), lambda l, j: (l, 0, j)),
        ],
        out_specs=pl.BlockSpec((None, rows, D_MODEL), lambda l, j: (l, 0, j)),
        out_shape=jax.ShapeDtypeStruct((depth, rows, ada_w.shape[2]), F32),
        compiler_params=_cparams("arbitrary", "arbitrary"),
        name="ada_mod",
    )(cvec, ada_w, ada_b.reshape(depth, 1, -1))


def _mod_spec(row0, row_stride):
    return pl.BlockSpec((None, 6, 1, D_MODEL), lambda b, i: (row0 + row_stride * b, 0, 0, 0))


def _tok_spec(tm, width):
    return pl.BlockSpec((None, tm, width), lambda b, i: (b, i, 0))


def _rope(y, cos, sin_signed):
    width = y.shape[1]
    reps = width // LANES
    cosw = jnp.concatenate([cos] * reps, axis=1) if reps > 1 else cos
    sinw = jnp.concatenate([sin_signed] * reps, axis=1) if reps > 1 else sin_signed
    lane = lax.broadcasted_iota(jnp.int32, y.shape, 1)
    first_half = (lane % HEAD_DIM) < (HEAD_DIM // 2)
    partner = jnp.where(first_half, pltpu.roll(y, width - HEAD_DIM // 2, axis=1), pltpu.roll(y, HEAD_DIM // 2, axis=1))
    return y * cosw + partner * sinw


def _kv_variants(a, ones_lane=False):
    lane = lax.broadcasted_iota(jnp.int32, a.shape, 1)
    low = lane < HEAD_DIM
    swapped = pltpu.roll(a, HEAD_DIM, axis=1)
    pad_hi = jnp.where(lane == HEAD_DIM, 1.0, 0.0) if ones_lane else jnp.zeros_like(a)
    pad_lo = jnp.where(lane == 0, 1.0, 0.0) if ones_lane else jnp.zeros_like(a)
    return jnp.concatenate([
        jnp.where(low, a, pad_hi),
        jnp.where(low, pad_lo, swapped),
        jnp.where(low, swapped, pad_hi),
        jnp.where(low, pad_lo, a),
    ], axis=1)


def _inproj_even_kernel(*refs, rope):
    if rope:
        (x_ref, mod_ref, g_ref, wqkv_ref, wm_ref, wmt_ref, qg_ref, kg_ref, gbt_ref, ones_ref, cos_ref, sin_ref,
         q_out, kk_out, vv_out, mqt_out, mk_out, mvt_out, mo_out, gatet_out) = refs
    else:
        (x_ref, mod_ref, g_ref, wqkv_ref, wm_ref, wmt_ref, qg_ref, kg_ref, gbt_ref, ones_ref,
         q_out, kk_out, vv_out, mqt_out, mk_out, mvt_out, mo_out, gatet_out, k_out, v_out) = refs
    h = _rms(x_ref[...], g_ref[...]) * (1.0 + mod_ref[1]) + mod_ref[0]
    hb = h.astype(BF16)

    qkv = _dot(hb, wqkv_ref[...])
    w = MLSTM_WIDTH
    mm = _dot(hb, wm_ref[...])
    mk_out[...] = mm[:, :w].astype(BF16)
    mo_out[...] = mm[:, w:]
    q = _group_rms(qkv[:, :ATT_WIDTH], qg_ref[...], ones_ref[...], HEAD_DIM)
    k = _group_rms(qkv[:, ATT_WIDTH:ATT_WIDTH + ATT_KV_WIDTH], kg_ref[...], ones_ref[:ATT_KV_WIDTH, :ATT_KV_WIDTH], HEAD_DIM)
    v = qkv[:, ATT_WIDTH + ATT_KV_WIDTH:]
    if rope:
        q = _rope(q, cos_ref[...], sin_ref[...])
        k = _rope(k, cos_ref[...], sin_ref[...])
    q_out[...] = (q * (HEAD_DIM ** -0.5 * LOG2_E)).astype(BF16)
    if not rope:
        for src, dst in ((k, k_out), (v, v_out)):
            dst[0] = src[:, :HEAD_DIM]
            dst[1] = pltpu.roll(src, HEAD_DIM, axis=1)[:, :HEAD_DIM]
    kk_out[...] = _kv_variants(k).astype(BF16)
    vv_out[...] = _kv_variants(v, ones_lane=True).astype(BF16)

    mmt = _dot_nt(wmt_ref[...], hb)
    mqt_out[...] = (mmt[:w] * MLSTM_HD ** -0.5).astype(BF16)
    mvt_out[...] = mmt[w:2 * w].astype(BF16)

    gates_t = mmt[2 * w:] + gbt_ref[...]
    sub = lax.broadcasted_iota(jnp.int32, gates_t.shape, 0)
    gatet_out[...] = jnp.where(sub < N_DIR * MLSTM_HEADS, gates_t, _log_sigmoid(gates_t))


def _inproj_even_call(x, mod, mod_row0, mod_stride, prm, rope_tabs):
    B, T, _ = x.shape
    tm = min(TOKEN_TILE, T)
    rope = rope_tabs is not None
    ins = [x, mod, prm["g_mix_pre"], prm["w_qkv"], prm["w_m"], prm["w_mt"], prm["q_gain"],
           prm["k_gain"], prm["gate_bias_t"], prm["ones64"]]
    in_specs = [_tok_spec(tm, D_MODEL), _mod_spec(mod_row0, mod_stride)] + [_const_spec(a.shape) for a in ins[2:]]
    if rope:
        ins += list(rope_tabs)
        in_specs += [pl.BlockSpec((tm, LANES), lambda b, i: (i, 0))] * 2
    n_gate = 2 * N_DIR * MLSTM_HEADS
    outs = [(ATT_WIDTH, BF16, True), (4 * LANES, BF16, True), (4 * LANES, BF16, True),
            (MLSTM_WIDTH, BF16, False), (MLSTM_WIDTH, BF16, True), (MLSTM_WIDTH, BF16, False),
            (MLSTM_WIDTH, F32, True), (n_gate, F32, False)]
    feat_spec = lambda w: pl.BlockSpec((None, w, tm), lambda b, i: (b, 0, i))
    out_specs = [_tok_spec(tm, w) if tok else feat_spec(w) for w, _, tok in outs]
    out_shape = [jax.ShapeDtypeStruct((B, T, w) if tok else (B, w, T), dt) for w, dt, tok in outs]
    if not rope:
        out_specs += [pl.BlockSpec((None, ATT_KV_HEADS, tm, HEAD_DIM), lambda b, i: (b, 0, i, 0))] * 2
        out_shape += [jax.ShapeDtypeStruct((B, ATT_KV_HEADS, T, HEAD_DIM), F32)] * 2
    return pl.pallas_call(
        functools.partial(_inproj_even_kernel, rope=rope),
        grid=(B, T // tm),
        in_specs=in_specs,
        out_specs=out_specs,
        out_shape=out_shape,
        compiler_params=_cparams("parallel", "parallel"),
        name="inproj_even",
    )(*ins)


def _attn_kernel(*refs, cached):
    if cached:
        q_ref, kk_ref, vv_ref, kc_ref, vc_ref, o_ref = refs
    else:
        q_ref, kk_ref, vv_ref, o_ref = refs
    tq = q_ref.shape[0]
    group = ATT_HEADS // ATT_KV_HEADS
    units = [(kv, half) for kv in range(ATT_KV_HEADS) for half in range(2)]

    def unit_heads(kv, half):
        return [kv * group + half + 2 * i for i in range(group // 2)]

    sources = [(kk_ref, vv_ref)] + ([(kc_ref, vc_ref)] if cached else [])
    den_from_values = sum(k_ref.shape[0] for k_ref, _ in sources) >= ATTN_DEN_ON_MXU_MIN_KEYS

    def scores(kv, half):
        qs = jnp.concatenate([q_ref[:, (h // 2) * LANES:(h // 2 + 1) * LANES] for h in unit_heads(kv, half)], axis=0)
        col = (2 * kv + half) * LANES
        return [_dot_nt(qs, k_ref[:, col:col + LANES]) for k_ref, _ in sources]

    def weighted_values(kv, half, s_blocks):
        col = (2 * kv + half) * LANES
        m = None
        for s in s_blocks:
            bm = jnp.max(s, axis=-1, keepdims=True)
            m = bm if m is None else jnp.maximum(m, bm)
        o = None
        den = None
        for s, (_, v_ref) in zip(s_blocks, sources):
            p = jnp.exp2(s - m)
            if not den_from_values:
                bd = jnp.sum(p, axis=-1, keepdims=True)
                den = bd if den is None else den + bd
            bo = _dot(p.astype(BF16), v_ref[:, col:col + LANES])
            o = bo if o is None else o + bo
        own_low = half == 0
        if den_from_values:
            den_lane = HEAD_DIM if own_low else 0
            den = o[:, den_lane:den_lane + 1]
        lane = lax.broadcasted_iota(jnp.int32, o.shape, 1)
        own = (lane < HEAD_DIM) if own_low else (lane >= HEAD_DIM)
        return jnp.where(own, o / den, 0.0)

    pending = [scores(*u) for u in units[:ATTN_SCORES_AHEAD]]
    acc = {}
    for idx, (kv, half) in enumerate(units):
        if idx + ATTN_SCORES_AHEAD < len(units):
            pending.append(scores(*units[idx + ATTN_SCORES_AHEAD]))
        o = weighted_values(kv, half, pending[idx])
        for i, h in enumerate(unit_heads(kv, half)):
            pair = h // 2
            part = o[i * tq:(i + 1) * tq]
            if pair in acc:
                o_ref[:, pair * LANES:(pair + 1) * LANES] = (acc.pop(pair) + part).astype(o_ref.dtype)
            else:
                acc[pair] = part


def _attn_call(q, kk, vv, cache=None):
    B, T, _ = q.shape
    tq = min(ATTN_Q_TILE, T)
    cached = cache is not None
    ins = [q, kk, vv]
    full = lambda n: pl.BlockSpec((None, n, 4 * LANES), lambda b, i: (b, 0, 0))
    in_specs = [_tok_spec(tq, ATT_WIDTH), full(T), full(T)]
    if cached:
        ins += list(cache)
        in_specs += [full(cache[0].shape[1])] * 2
    return pl.pallas_call(
        functools.partial(_attn_kernel, cached=cached),
        grid=(B, T // tq),
        in_specs=in_specs,
        out_specs=_tok_spec(tq, ATT_WIDTH),
        out_shape=jax.ShapeDtypeStruct((B, T, ATT_WIDTH), BF16),
        compiler_params=_cparams("parallel", "parallel"),
        name="attention",
    )(*ins)


def _mlstm_kernel(*refs, zero_init):
    if zero_init:
        (qf_ref, kf_ref, vf_ref, gtf_ref, qb_ref, kb_ref, vb_ref, gtb_ref, trit_ref,
         hf_ref, hb_ref, c_ref, n_ref, m_ref) = refs
    else:
        (qf_ref, kf_ref, vf_ref, gtf_ref, qb_ref, kb_ref, vb_ref, gtb_ref, trit_ref,
         c0_ref, n0_ref, m0_ref, hf_ref, hb_ref, c_ref, n_ref, m_ref) = refs
    L = MLSTM_CHUNK
    H = MLSTM_HEADS
    HD = MLSTM_HD
    PAD = 2 * SUBLANES
    chunk = pl.program_id(1)

    @pl.when(chunk == 0)
    def _init():
        if zero_init:
            c_ref[...] = jnp.zeros(c_ref.shape, F32)
            n_ref[...] = jnp.zeros(n_ref.shape, F32)
            m_ref[...] = jnp.full(m_ref.shape, M_INIT, F32)
        else:
            for d in range(N_DIR):
                for hd in range(H):
                    c_ref[d, hd] = c0_ref[d, hd].T
            n_ref[...] = n0_ref[...]
            m_ref[...] = m0_ref[...]

    m_all = m_ref[...]
    n_all = n_ref[...]
    m_row = m_all
    m_lane = lax.broadcasted_iota(jnp.int32, m_all.shape, 1)
    key = lax.broadcasted_iota(jnp.int32, (L, L), 0)
    qry = lax.broadcasted_iota(jnp.int32, (L, L), 1)
    dirs = ((qf_ref, kf_ref, vf_ref, gtf_ref, hf_ref, key <= qry, L - 1),
            (qb_ref, kb_ref, vb_ref, gtb_ref, hb_ref, key >= qry, 0))
    n_in = N_DIR * H
    order = [(d, hd) for d in range(N_DIR) for hd in range(H)]

    def query_stage(d, hd):
        qt_ref, k_ref = dirs[d][0], dirs[d][1]
        j = d * H + hd
        sl = slice(hd * HD, (hd + 1) * HD)
        n_blk = jnp.broadcast_to(n_all[j:j + 1, :], (PAD, HD)).astype(BF16)
        return _dot(jnp.concatenate([k_ref[:, sl], c_ref[d, hd].astype(BF16), n_blk], axis=0), qt_ref[sl, :])

    sums = []
    for d, (_, _, _, gt_ref, _, _, _) in enumerate(dirs):
        gates_t = gt_ref[...]
        bcum_t = _dot(jnp.concatenate(_split3(gates_t), axis=1), trit_ref[d])
        sums.append((gates_t, bcum_t))
    c_rows = [sums[d][0][d * H:(d + 1) * H] - sums[d][1][n_in + d * H:n_in + (d + 1) * H] for d in range(N_DIR)]
    c_cols = jnp.concatenate(c_rows, axis=0).T

    def first_stage(d, hd, r):
        _, k_ref, vt_ref, _, _, mask, last = dirs[d]
        gates_t, bcum_t = sums[d]
        j = d * H + hd
        jf = n_in + j
        sl = slice(hd * HD, (hd + 1) * HD)
        k = k_ref[:, sl]
        v_t = vt_ref[sl, :]
        b_row = bcum_t[jf:jf + 1, :]
        i_row = gates_t[j:j + 1, :]
        c_col = c_cols[:, j:j + 1]
        m_prev = m_all[0:1, j:j + 1]
        ct_prev = c_ref[d, hd]
        n_prev = n_all[j:j + 1, :]

        g = b_row + m_prev
        dlog = jnp.where(mask, c_col + b_row, -jnp.inf)
        m_t = jnp.maximum(g, jnp.max(dlog, axis=0, keepdims=True))
        w = jnp.exp(dlog - m_t)
        w_inter = jnp.exp(g - m_t)

        b_last = bcum_t[jf:jf + 1, last:last + 1]
        wlog = b_last - b_row + i_row
        m_new = jnp.maximum(b_last + m_prev, jnp.max(wlog, axis=1, keepdims=True))
        ws = jnp.exp(wlog - m_new)
        decay = jnp.exp(b_last + m_prev - m_new)
        vw = (v_t.astype(F32) * ws).astype(BF16)
        ws_blk = jnp.broadcast_to(ws, (PAD, L)).astype(BF16)
        upd = _dot(jnp.concatenate([vw, ws_blk], axis=0), k)
        c_new = decay * ct_prev + upd[:HD]
        n_new = decay * n_prev + upd[HD:HD + 1]
        return dict(d=d, hd=hd, j=j, r=r, w=w, w_inter=w_inter, m_t=m_t, v_t=v_t, c_new=c_new, n_new=n_new, m_new=m_new)

    def second_stage(s):
        h_ref = dirs[s["d"]][4]
        sl = slice(s["hd"] * HD, (s["hd"] + 1) * HD)
        r = s["r"]
        qk = r[:L] * s["w"]
        num = _dot(s["v_t"], qk.astype(BF16)) + s["w_inter"] * r[L:L + HD]
        den = jnp.sum(qk, axis=0, keepdims=True) + s["w_inter"] * r[L + HD:L + HD + 1]
        h_ref[:, sl] = (num / jnp.maximum(jnp.abs(den), jnp.exp(-s["m_t"]))).T

    staged = []
    for idx, (d, hd) in enumerate(order):
        staged.append(first_stage(d, hd, query_stage(d, hd)))
        if idx >= MLSTM_STAGE_LAG:
            second_stage(staged[idx - MLSTM_STAGE_LAG])
    for s in staged[len(order) - MLSTM_STAGE_LAG:]:
        second_stage(s)
    for s in staged:
        c_ref[s["d"], s["hd"]] = s["c_new"]
        m_row = jnp.where(m_lane == s["j"], s["m_new"], m_row)
    n_ref[...] = jnp.concatenate([s["n_new"] for s in staged], axis=0)
    m_ref[...] = m_row

    @pl.when(chunk == pl.num_programs(1) - 1)
    def _finish():
        for d in range(N_DIR):
            for hd in range(H):
                c_ref[d, hd] = c_ref[d, hd].T


def _mlstm_call(mq_t, mk, mv_t, gates_t, tri_t, state0):
    B, T, _ = mk.shape
    L = MLSTM_CHUNK
    nc = T // L
    zero_init = state0 is None
    fwd = lambda w: pl.BlockSpec((None, L, w), lambda b, c: (b, c, 0))
    bwd = lambda w: pl.BlockSpec((None, L, w), lambda b, c: (b, nc - 1 - c, 0))
    fwd_t = lambda w: pl.BlockSpec((None, w, L), lambda b, c: (b, 0, c))
    bwd_t = lambda w: pl.BlockSpec((None, w, L), lambda b, c: (b, 0, nc - 1 - c))
    W = MLSTM_WIDTH
    NG = 2 * N_DIR * MLSTM_HEADS
    c_spec = pl.BlockSpec((None, N_DIR, MLSTM_HEADS, MLSTM_HD, MLSTM_HD), lambda b, c: (b, 0, 0, 0, 0))
    n_spec = pl.BlockSpec((None, N_DIR * MLSTM_HEADS, MLSTM_HD), lambda b, c: (b, 0, 0))
    m_spec = pl.BlockSpec((None, 1, LANES), lambda b, c: (b, 0, 0))
    ins = [mq_t, mk, mv_t, gates_t, mq_t, mk, mv_t, gates_t, tri_t]
    in_specs = [fwd_t(W), fwd(W), fwd_t(W), fwd_t(NG), bwd_t(W), bwd(W), bwd_t(W), bwd_t(NG), _const_spec(tri_t.shape)]
    if not zero_init:
        ins += list(state0)
        in_specs += [c_spec, n_spec, m_spec]
    return pl.pallas_call(
        functools.partial(_mlstm_kernel, zero_init=zero_init),
        grid=(B, nc),
        in_specs=in_specs,
        out_specs=[fwd(W), bwd(W), c_spec, n_spec, m_spec],
        out_shape=[jax.ShapeDtypeStruct((B, T, W), F32), jax.ShapeDtypeStruct((B, T, W), F32),
                   jax.ShapeDtypeStruct((B, N_DIR, MLSTM_HEADS, MLSTM_HD, MLSTM_HD), F32),
                   jax.ShapeDtypeStruct((B, N_DIR * MLSTM_HEADS, MLSTM_HD), F32),
                   jax.ShapeDtypeStruct((B, 1, LANES), F32)],
        compiler_params=_cparams("parallel", "arbitrary"),
        name="mlstm",
    )(*ins)


def _residual_ffn(x_ref, mix_fn, mod_ref, gpost_ref, gfpre_ref, gfpost_ref, wout_ref, w1_ref, w2_ref, o_ref):
    half = D_MODEL // 2
    sub = min(POST_SUBTILE, x_ref.shape[0])
    rows = [slice(r0, r0 + sub) for r0 in range(0, x_ref.shape[0], sub)]
    ys = []
    for r in rows:
        mix_a, mix_b = mix_fn(r)
        ys.append(_dot(mix_a.astype(BF16), wout_ref[:half, :]) + _dot(mix_b.astype(BF16), wout_ref[half:, :]))
    for r, y in zip(rows, ys):
        x1 = x_ref[r, :] + mod_ref[2] * _rms(y, gpost_ref[...])
        h2 = (_rms(x1, gfpre_ref[...]) * (1.0 + mod_ref[4]) + mod_ref[3]).astype(BF16)
        chunks = list(range(0, D_FF, FF_CHUNK))
        up = _dot(h2, w1_ref[:, chunks[0]:chunks[0] + FF_CHUNK])
        f = None
        for i, c0 in enumerate(chunks):
            cur = up
            if i + 1 < len(chunks):
                up = _dot(h2, w1_ref[:, chunks[i + 1]:chunks[i + 1] + FF_CHUNK])
            part = _dot(jnp.square(jnp.maximum(cur, 0.0)).astype(BF16), w2_ref[c0:c0 + FF_CHUNK, :])
            f = part if f is None else f + part
        o_ref[r, :] = x1 + mod_ref[5] * _rms(f, gfpost_ref[...])


def _head_rms(y, gain_ref, width):
    outs = []
    for h0 in range(0, y.shape[1], width):
        seg = y[:, h0:h0 + width]
        outs.append(seg * lax.rsqrt(jnp.mean(seg * seg, axis=-1, keepdims=True) + EPS) * gain_ref[:, h0:h0 + width])
    return jnp.concatenate(outs, axis=1)


def _post_even_kernel(x_ref, attn_ref, hf_ref, hb_ref, mo_ref, mod_ref, gm_ref, gpost_ref, gfpre_ref, gfpost_ref,
                      wout_ref, w1_ref, w2_ref, o_ref):
    def mix(r):
        hm = _head_rms(hf_ref[r, :] + hb_ref[r, :], gm_ref, MLSTM_HD) * jax.nn.sigmoid(mo_ref[r, :])
        return attn_ref[r, :], hm

    _residual_ffn(x_ref, mix, mod_ref, gpost_ref, gfpre_ref, gfpost_ref, wout_ref, w1_ref, w2_ref, o_ref)


def _post_odd_kernel(x_ref, lru_ref, rg_ref, of_ref, ob_ref, gg_ref, mod_ref, gm_ref, gpost_ref, gfpre_ref, gfpost_ref,
                     wout_ref, w1_ref, w2_ref, o_ref):
    def mix(r):
        lru = jax.nn.gelu(rg_ref[r, :]) * lru_ref[r, :]
        og = _head_rms(of_ref[r, :] + ob_ref[r, :], gm_ref, GLA_DV) * jax.nn.silu(gg_ref[r, :])
        return lru, og

    _residual_ffn(x_ref, mix, mod_ref, gpost_ref, gfpre_ref, gfpost_ref, wout_ref, w1_ref, w2_ref, o_ref)


def _post_call(kernel_fn, name, x, mix_ins, mod, mod_row0, mod_stride, consts):
    shape = x.shape
    if mod_stride == 0:
        x = x.reshape(1, -1, D_MODEL)
        mix_ins = [a.reshape(1, -1, a.shape[2]) for a in mix_ins]
    B, T, _ = x.shape
    tm = min(POST_TILE, T)
    ins = [x] + list(mix_ins) + [mod] + list(consts)
    in_specs = ([_tok_spec(tm, D_MODEL)] + [_tok_spec(tm, a.shape[2]) for a in mix_ins] + [_mod_spec(mod_row0, mod_stride)]
                + [_const_spec(a.shape) for a in consts])
    return pl.pallas_call(
        kernel_fn,
        grid=(B, T // tm),
        in_specs=in_specs,
        out_specs=_tok_spec(tm, D_MODEL),
        out_shape=jax.ShapeDtypeStruct(x.shape, F32),
        compiler_params=_cparams("parallel", "parallel"),
        name=name,
    )(*ins).reshape(shape)


def _inproj_odd_kernel(x_ref, xp_ref, xn_ref, mod_ref, g_ref, wmain_ref, wlr_in_ref, wlr_hi_ref, wlr_lo_ref, blr_ref,
                       cw_ref, cb_ref, wg_ref, bg_ref, lam_ref,
                       rg_out, gq_out, gk_out, gv_out, gg_out, la_out, af_out, uf_out, ab_out, ub_out):
    tm = x_ref.shape[0]
    W = LRU_WIDTH
    sub = min(ODD_GROUP_ROWS, tm)
    n_groups = tm // sub
    m = sub + 2 * CONV_HALO
    first = pl.program_id(1) == 0
    last = pl.program_id(1) == pl.num_programs(1) - 1
    rid = lax.broadcasted_iota(jnp.int32, (m, W), 0)
    neg_half_c_sp = (-0.5 * LRU_C) * _softplus(-lam_ref[...])

    def modulated(xv):
        return (_rms(xv, g_ref[...]) * (1.0 + mod_ref[1]) + mod_ref[0]).astype(BF16)

    for q in range(n_groups):
        rows = slice(q * sub, (q + 1) * sub)
        prev = xp_ref[...] if q == 0 else x_ref[q * sub - CONV_HALO:q * sub, :]
        nxt = xn_ref[...] if q == n_groups - 1 else x_ref[(q + 1) * sub:(q + 1) * sub + CONV_HALO, :]
        hb_ext = modulated(jnp.concatenate([prev, x_ref[rows, :], nxt], axis=0))
        hb = hb_ext[CONV_HALO:CONV_HALO + sub]

        glr_hi, glr_lo = _split2(_dot(hb, wlr_in_ref[...]))

        rx = _dot(hb_ext, wmain_ref[:, :W])
        outside = None
        if q == 0:
            outside = first & (rid < CONV_HALO)
        if q == n_groups - 1:
            tail = last & (rid >= sub + CONV_HALO)
            outside = tail if outside is None else outside | tail
        if outside is not None:
            rx = jnp.where(outside, 0.0, rx)

        mm = _dot(hb, wmain_ref[:, W:3 * W])
        rg_out[rows, :] = mm[:, :W]
        gq_out[rows, :] = mm[:, W:W + GLA_KW] * GLA_DK ** -0.5
        gk_out[rows, :] = mm[:, W + GLA_KW:]

        z = _dot(glr_hi, wlr_hi_ref[...]) + (_dot(glr_hi, wlr_lo_ref[...]) + _dot(glr_lo, wlr_hi_ref[...])) + blr_ref[...]
        la_out[rows, :] = _log_sigmoid(z) * (1.0 / GLA_TAU)

        xc = cb_ref[...] + rx * cw_ref[CONV_LEFT:CONV_LEFT + 1, :]
        for tap in range(CONV_W):
            off = tap - CONV_LEFT
            if off != 0:
                xc = xc + pltpu.roll(rx, (-off) % m, axis=0) * cw_ref[tap:tap + 1, :]
        xc = xc[CONV_HALO:CONV_HALO + sub, :]
        xcb = xc.astype(BF16)
        half_xc = 0.5 * xc
        for d, (a_out, u_out) in enumerate(((af_out, uf_out), (ab_out, ub_out))):
            cols = slice(2 * d * W, (2 * d + 2) * W)
            th = jnp.tanh(_dot(xcb, wg_ref[:, cols]) + bg_ref[:, cols])
            if d == 0:
                gv_out[rows, :] = _dot(hb, wmain_ref[:, 3 * W:4 * W]).astype(BF16)
            else:
                gg_out[rows, :] = _dot(hb, wmain_ref[:, 4 * W:])
            th_r = th[:, :W]
            th_i = th[:, W:]
            log_a = neg_half_c_sp[d:d + 1, :] * th_r + neg_half_c_sp[d:d + 1, :]
            a = jnp.exp(log_a)
            a_out[rows, :] = a
            v = jnp.tanh(log_a) * (-1.0 - a * a)
            u_out[rows, :] = jnp.where(v > 0.0, v * lax.rsqrt(v), 0.0) * (half_xc * th_i + half_xc)


def _inproj_odd_call(x, mod, mod_row0, mod_stride, prm, lru_prm):
    B, T, _ = x.shape
    tm = min(TOKEN_TILE, T)
    halo_per_tile = tm // CONV_HALO
    n_halo = T // CONV_HALO
    prev_spec = pl.BlockSpec((None, CONV_HALO, D_MODEL), lambda b, i: (b, jnp.maximum(i * halo_per_tile - 1, 0), 0))
    next_spec = pl.BlockSpec((None, CONV_HALO, D_MODEL),
                             lambda b, i: (b, jnp.minimum((i + 1) * halo_per_tile, n_halo - 1), 0))
    consts = [prm["g_mix_pre"], prm["w_main"], prm["w_lr_in"], prm["w_lr_hi"], prm["w_lr_lo"], prm["b_lr"],
              lru_prm["conv_w"], lru_prm["conv_b"], lru_prm["w_gate"], lru_prm["b_gate"], lru_prm["lam"]]
    ins = [x, x, x, mod] + consts
    in_specs = ([_tok_spec(tm, D_MODEL), prev_spec, next_spec, _mod_spec(mod_row0, mod_stride)]
                + [_const_spec(a.shape) for a in consts])
    widths = [(LRU_WIDTH, F32), (GLA_KW, F32), (GLA_KW, F32), (GLA_WIDTH, BF16), (GLA_WIDTH, F32),
              (N_DIR * GLA_KW, F32)] + [(LRU_WIDTH, F32)] * 4
    return pl.pallas_call(
        _inproj_odd_kernel,
        grid=(B, T // tm),
        in_specs=in_specs,
        out_specs=[_tok_spec(tm, w) for w, _ in widths],
        out_shape=[jax.ShapeDtypeStruct((B, T, w), dt) for w, dt in widths],
        compiler_params=_cparams("parallel", "parallel"),
        name="inproj_odd",
    )(*ins)


def _lru_kernel(*refs, zero_init):
    if zero_init:
        af_s, uf_s, ab_s, ub_s, o_ref, last_ref, hb_s = refs
    else:
        af_s, uf_s, ab_s, ub_s, h0_ref, o_ref, last_ref, hb_s = refs
    T, W = af_s.shape
    nblk = T // SUBLANES
    rid8 = lax.broadcasted_iota(jnp.int32, (SUBLANES, W), 0)

    def block_scan(a, u, reverse):
        for dist in (1, 2, 4):
            if reverse:
                keep = rid8 < SUBLANES - dist
                shift = SUBLANES - dist
            else:
                keep = rid8 >= dist
                shift = dist
            a_n = jnp.where(keep, pltpu.roll(a, shift, axis=0), 1.0)
            u_n = jnp.where(keep, pltpu.roll(u, shift, axis=0), 0.0)
            u = a * u_n + u
            a = a * a_n
        return a, u

    def body(blk, carry):
        hf, hb = carry
        rf = pl.multiple_of(blk * SUBLANES, SUBLANES)
        a, u = block_scan(af_s[pl.ds(rf, SUBLANES), :], uf_s[pl.ds(rf, SUBLANES), :], False)
        out_f = u + a * hf
        o_ref[pl.ds(rf, SUBLANES), :] = out_f
        rb = pl.multiple_of((nblk - 1 - blk) * SUBLANES, SUBLANES)
        a, u = block_scan(ab_s[pl.ds(rb, SUBLANES), :], ub_s[pl.ds(rb, SUBLANES), :], True)
        out_b = u + a * hb
        hb_s[pl.ds(rb, SUBLANES), :] = out_b
        return out_f[SUBLANES - 1:SUBLANES, :], out_b[0:1, :]

    if zero_init:
        init = (jnp.zeros((1, W), F32), jnp.zeros((1, W), F32))
    else:
        init = (h0_ref[0:1, :], h0_ref[1:2, :])
    hf, hb = lax.fori_loop(0, nblk, body, init)
    last_ref[0:1, :] = hf
    last_ref[1:2, :] = hb
    o_ref[...] = o_ref[...] + hb_s[...]


def _lru_call(coeffs, h0):
    B, T, W = coeffs[0].shape
    zero_init = h0 is None
    seq = pl.BlockSpec((None, T, W), lambda b: (b, 0, 0))
    st = pl.BlockSpec((None, N_DIR, W), lambda b: (b, 0, 0))
    ins = list(coeffs)
    in_specs = [seq] * len(ins)
    if not zero_init:
        ins.append(h0)
        in_specs.append(st)
    return pl.pallas_call(
        functools.partial(_lru_kernel, zero_init=zero_init),
        grid=(B,),
        in_specs=in_specs,
        out_specs=[seq, st],
        out_shape=[jax.ShapeDtypeStruct((B, T, W), F32), jax.ShapeDtypeStruct((B, N_DIR, W), F32)],
        scratch_shapes=[pltpu.VMEM((T, W), F32)],
        compiler_params=_cparams("parallel"),
        name="rglru",
    )(*ins)


GLA_LEVELS = GLA_CHUNK.bit_length() - 1


def _gla_constants():
    L = GLA_CHUNK
    idx = np.arange(L)
    sel = np.zeros((N_DIR, (GLA_LEVELS + 3) * L, L), np.float32)
    lvl_mask = np.zeros((N_DIR, GLA_LEVELS + 1, L, L), np.float32)
    for d in range(N_DIR):
        tri = (idx[None, :] <= idx[:, None]) if d == 0 else (idx[None, :] >= idx[:, None])
        tri = tri.astype(np.float32)
        last = L - 1 if d == 0 else 0
        sel[d, 0:L] = tri
        for lv in range(GLA_LEVELS):
            m = L >> (lv + 1)
            start = (idx // (2 * m)) * (2 * m)
            second = (idx - start) >= m
            boundary = start + (m - 1 if d == 0 else m)
            query_role = second if d == 0 else ~second
            diff = tri - tri[boundary]
            sel[d, (1 + lv) * L:(2 + lv) * L] = np.where(query_role[:, None], diff, -diff)
            same = start[:, None] == start[None, :]
            lvl_mask[d, lv] = same & query_role[:, None] & ~query_role[None, :]
        lvl_mask[d, GLA_LEVELS] = np.eye(L)
        sel[d, (GLA_LEVELS + 1) * L:(GLA_LEVELS + 2) * L] = tri[last][None, :] - tri
        sel[d, (GLA_LEVELS + 2) * L:] = tri[last][None, :]
    assert sel.min() >= 0.0 and sel.max() <= 1.0
    sel3 = np.concatenate([sel] * 3, axis=2)
    lvl_mask = np.tile(lvl_mask, (1, 1, 1, GLA_HEADS))
    return jnp.asarray(sel3, BF16), jnp.asarray(lvl_mask, F32)


def _gla_kernel(*refs, zero_init):
    if zero_init:
        (qf_ref, kf_ref, vf_ref, af_ref, qb_ref, kb_ref, vb_ref, ab_ref, sel_ref, msk_ref, of_ref, ob_ref, s_ref) = refs
    else:
        (qf_ref, kf_ref, vf_ref, af_ref, qb_ref, kb_ref, vb_ref, ab_ref, sel_ref, msk_ref, s0_ref,
         of_ref, ob_ref, s_ref) = refs
    L = GLA_CHUNK
    H = GLA_HEADS
    KW = GLA_KW
    VW = GLA_WIDTH
    DK = GLA_DK
    DV = GLA_DV

    @pl.when(pl.program_id(1) == 0)
    def _init():
        if zero_init:
            s_ref[...] = jnp.zeros(s_ref.shape, F32)
        else:
            s_ref[...] = s0_ref[...]

    zero_k = jnp.zeros((L, KW), BF16)
    zero_v = jnp.zeros((L, VW), BF16)
    lane_k = lax.broadcasted_iota(jnp.int32, (L, KW), 1) // DK
    lane_v = lax.broadcasted_iota(jnp.int32, (L, VW), 1) // DV

    def stack_heads(a, lane_head, zero):
        return jnp.concatenate([jnp.where(lane_head == h, a, zero) for h in range(H)], axis=0)

    def tile_rows(a):
        return jnp.concatenate([a] * H, axis=0)

    dirs = ((qf_ref, kf_ref, vf_ref, af_ref, of_ref), (qb_ref, kb_ref, vb_ref, ab_ref, ob_ref))
    work = []
    for i in range(GLA_STEP_CHUNKS):
        for d, (q_ref, k_ref, v_ref, a_ref, o_ref) in enumerate(dirs):
            j = i if d == 0 else GLA_STEP_CHUNKS - 1 - i
            rows = slice(j * L, (j + 1) * L)
            sums = _dot(sel_ref[d], jnp.concatenate(_split3(a_ref[rows, :]), axis=0))
            q_b = q_ref[rows, :].astype(BF16)
            work.append(dict(d=d, rows=rows, sums=sums, q_b=q_b, q_st=stack_heads(q_b, lane_k, zero_k),
                             k_st=stack_heads(k_ref[rows, :].astype(BF16), lane_k, zero_k)))

    for lv in range(GLA_LEVELS + 1):
        for wk in work:
            if lv == GLA_LEVELS:
                part = _dot_nt(wk["q_b"], wk["k_st"])
            else:
                wl = jnp.exp(wk["sums"][(1 + lv) * L:(2 + lv) * L]).astype(BF16)
                part = _dot_nt(wk["q_b"] * wl, wk["k_st"] * tile_rows(wl))
            part = part * msk_ref[wk["d"], lv]
            wk["scores"] = part if lv == 0 else wk["scores"] + part

    state = [s_ref[d] for d in range(N_DIR)]
    for wk in work:
        d, rows, sums = wk["d"], wk["rows"], wk["sums"]
        _, k_ref, v_ref, _, o_ref = dirs[d]
        v = v_ref[rows, :]
        s_prev = state[d]
        intra = _dot(wk["scores"].astype(BF16), stack_heads(v, lane_v, zero_v))
        q_in = wk["q_st"] * tile_rows(jnp.exp(sums[0:L]).astype(BF16))
        inter = _dot(q_in, s_prev.astype(BF16))
        o_ref[rows, :] = intra + jnp.concatenate([inter[h * L:(h + 1) * L] for h in range(H)], axis=1)

        k_out = k_ref[rows, :] * jnp.exp(sums[(GLA_LEVELS + 1) * L:(GLA_LEVELS + 2) * L])
        dec = jnp.exp(sums[(GLA_LEVELS + 2) * L:])
        both_t = jnp.concatenate([k_out, dec], axis=0).T
        k_out_t = both_t[:, :L].astype(BF16)
        upd = [_dot(k_out_t[h * DK:(h + 1) * DK], v[:, h * DV:(h + 1) * DV]) for h in range(H)]
        state[d] = both_t[:, L:L + 1] * s_prev + jnp.concatenate(upd, axis=0)
    for d in range(N_DIR):
        s_ref[d] = state[d]


def _gla_call(gq, gk, gv, log_a, sel3, lvl_mask, s0):
    B, T, _ = gq.shape
    L = GLA_CHUNK * GLA_STEP_CHUNKS
    nc = T // L
    zero_init = s0 is None
    fwd = lambda w: pl.BlockSpec((None, L, w), lambda b, c: (b, c, 0))
    bwd = lambda w: pl.BlockSpec((None, L, w), lambda b, c: (b, nc - 1 - c, 0))
    a_fwd = pl.BlockSpec((None, L, GLA_KW), lambda b, c: (b, c, 0))
    a_bwd = pl.BlockSpec((None, L, GLA_KW), lambda b, c: (b, nc - 1 - c, 1))
    s_spec = pl.BlockSpec((None, N_DIR, GLA_KW, GLA_DV), lambda b, c: (b, 0, 0, 0))
    ins = [gq, gk, gv, log_a, gq, gk, gv, log_a, sel3, lvl_mask]
    in_specs = [fwd(GLA_KW), fwd(GLA_KW), fwd(GLA_WIDTH), a_fwd, bwd(GLA_KW), bwd(GLA_KW), bwd(GLA_WIDTH), a_bwd,
                _const_spec(sel3.shape), _const_spec(lvl_mask.shape)]
    if not zero_init:
        ins.append(s0)
        in_specs.append(s_spec)
    return pl.pallas_call(
        functools.partial(_gla_kernel, zero_init=zero_init),
        grid=(B, nc),
        in_specs=in_specs,
        out_specs=[fwd(GLA_WIDTH), bwd(GLA_WIDTH), s_spec],
        out_shape=[jax.ShapeDtypeStruct((B, T, GLA_WIDTH), F32), jax.ShapeDtypeStruct((B, T, GLA_WIDTH), F32),
                   jax.ShapeDtypeStruct((B, N_DIR, GLA_KW, GLA_DV), F32)],
        compiler_params=_cparams("parallel", "arbitrary"),
        name="gla",
    )(*ins)


def _block_diag(blocks):
    n, r, c = blocks.shape
    eye = jnp.eye(n, dtype=blocks.dtype)
    return (eye[:, None, :, None] * blocks[:, :, None, :]).reshape(n * r, n * c)


def _pad_cols(a, width):
    return jnp.pad(a, ((0, 0), (0, width - a.shape[1])))


def _rope_tables(n_tokens):
    rows = n_tokens // GRID_W
    row = jnp.repeat(jnp.arange(rows), GRID_W).astype(F32)
    col = jnp.tile(jnp.arange(GRID_W), rows).astype(F32)
    inv = jnp.power(ROPE_THETA, -jnp.arange(ROPE_PAIRS_PER_AXIS, dtype=F32) / ROPE_PAIRS_PER_AXIS)
    ang = jnp.concatenate([row[:, None] * inv, col[:, None] * inv], axis=-1)
    cos, sin = jnp.cos(ang), jnp.sin(ang)
    reps = LANES // HEAD_DIM
    return jnp.tile(jnp.concatenate([cos, cos], axis=-1), (1, reps)), jnp.tile(jnp.concatenate([-sin, sin], axis=-1), (1, reps))


def _cache_variants(cache, ones_lane=False):
    z = jnp.zeros_like(cache[:, 0])
    if ones_lane:
        z = z.at[..., 0].set(1.0)
    return jnp.concatenate([cache[:, 0], z, z, cache[:, 0], cache[:, 1], z, z, cache[:, 1]], axis=-1).astype(BF16)


def _mlstm_tri():
    idx = np.arange(MLSTM_CHUNK)
    upper = idx[:, None] <= idx[None, :]
    tri_t = np.stack([upper, upper.T]).astype(np.float32)
    return jnp.asarray(np.concatenate([tri_t] * 3, axis=1), BF16)


def kernel(x_prompt, x_sample, cache_attn_k, cache_attn_v, state_mlstm_C, state_mlstm_n, state_mlstm_m, state_lru_h, state_gla_S, c, c_ctx, ada_w, ada_b, norm_mix_pre, norm_mix_post, norm_ffn_pre, norm_ffn_post, w_out, ffn_w1, ffn_w2, w_in_even, attn_q_norm, attn_k_norm, mlstm_i_bias, mlstm_f_bias, mlstm_norm, w_in_odd, lru_conv_w, lru_conv_b, lru_w_r, lru_b_r, lru_w_i, lru_b_i, lru_lambda, gla_w_lr, gla_b_lr, gla_norm):
    depth = ada_w.shape[0]
    Bp = x_prompt.shape[0]
    Bs = x_sample.shape[0]
    row = lambda a: a.reshape(1, -1)

    n_rows = -(-(1 + Bs) // SUBLANES) * SUBLANES
    cvec = jnp.zeros((n_rows, D_MODEL), F32).at[0].set(c_ctx).at[1:1 + Bs].set(c)
    mod_all = _ada_call(cvec, ada_w, ada_b).reshape(depth, n_rows, 6, 1, D_MODEL)

    rope_tabs = _rope_tables(x_sample.shape[1])
    ones64 = _block_diag(jnp.ones((ATT_HEADS, HEAD_DIM, HEAD_DIM), BF16))
    tri_t = _mlstm_tri()
    sel3, lvl_mask = _gla_constants()

    xp, xs = x_prompt, x_sample
    outs = {}
    for l in range(depth):
        mod = mod_all[l]
        tail = [row(norm_mix_post[l]), row(norm_ffn_pre[l]), row(norm_ffn_post[l]),
                w_out[l].astype(BF16), ffn_w1[l].astype(BF16), ffn_w2[l].astype(BF16)]
        if l % 2 == 0:
            e = l // 2
            w_in = w_in_even[e]
            o1 = ATT_WIDTH + 2 * ATT_KV_WIDTH
            o2 = o1 + 4 * MLSTM_WIDTH
            prm = {
                "g_mix_pre": row(norm_mix_pre[l]),
                "w_qkv": w_in[:, :o1].astype(BF16),
                "w_m": jnp.concatenate([w_in[:, o1 + MLSTM_WIDTH:o1 + 2 * MLSTM_WIDTH], w_in[:, o1 + 3 * MLSTM_WIDTH:o2]],
                                       axis=1).astype(BF16),
                "w_mt": jnp.concatenate([w_in[:, o1:o1 + MLSTM_WIDTH], w_in[:, o1 + 2 * MLSTM_WIDTH:o1 + 3 * MLSTM_WIDTH],
                                         w_in[:, o2:]], axis=1).T.astype(BF16),
                "gate_bias_t": jnp.concatenate([mlstm_i_bias[e].reshape(-1), mlstm_f_bias[e].reshape(-1)]).reshape(-1, 1),
                "q_gain": row(jnp.tile(attn_q_norm[e], ATT_HEADS)),
                "k_gain": row(jnp.tile(attn_k_norm[e], ATT_KV_HEADS)),
                "ones64": ones64,
            }
            consts = [row(mlstm_norm[e])] + tail
            for path in ("prompt", "sample"):
                if path == "prompt":
                    x, r0, rs, tabs, cache, st0 = xp, 0, 0, None, None, None
                else:
                    x, r0, rs, tabs = xs, 1, 1, rope_tabs
                    cache = (_cache_variants(cache_attn_k[:, e]), _cache_variants(cache_attn_v[:, e], ones_lane=True))
                    st0 = (state_mlstm_C[:, e], state_mlstm_n[:, e].reshape(Bs, N_DIR * MLSTM_HEADS, MLSTM_HD),
                           _pad_cols(state_mlstm_m[:, e].reshape(Bs, -1), LANES).reshape(Bs, 1, LANES))
                q, kk, vv, mq_t, mk, mv_t, mo, gates_t, *new_cache = _inproj_even_call(x, mod, r0, rs, prm, tabs)
                attn = _attn_call(q, kk, vv, cache)
                hf, hb, c_fin, n_fin, m_fin = _mlstm_call(mq_t, mk, mv_t, gates_t, tri_t, st0)
                x_new = _post_call(_post_even_kernel, "post_even", x, [attn, hf, hb, mo], mod, r0, rs, consts)
                if path == "prompt":
                    xp = x_new
                    outs.setdefault("k", []).append(new_cache[0])
                    outs.setdefault("v", []).append(new_cache[1])
                    outs.setdefault("C", []).append(c_fin)
                    outs.setdefault("n", []).append(n_fin.reshape(Bp, N_DIR, MLSTM_HEADS, MLSTM_HD))
                    outs.setdefault("m", []).append(m_fin[:, 0, :N_DIR * MLSTM_HEADS].reshape(Bp, N_DIR, MLSTM_HEADS))
                else:
                    xs = x_new
        else:
            o = l // 2
            w_in = w_in_odd[o]
            o1 = 2 * LRU_WIDTH + 2 * GLA_KW + 2 * GLA_WIDTH
            w_lr = _block_diag(gla_w_lr[o])
            w_lr = jnp.pad(w_lr, ((0, LANES - w_lr.shape[0]), (0, 0)))
            w_lr_hi = w_lr.astype(BF16)
            prm = {
                "g_mix_pre": row(norm_mix_pre[l]),
                "w_main": w_in[:, :o1].astype(BF16),
                "w_lr_in": _pad_cols(w_in[:, o1:], LANES).astype(BF16),
                "w_lr_hi": w_lr_hi,
                "w_lr_lo": (w_lr - w_lr_hi.astype(F32)).astype(BF16),
                "b_lr": row(gla_b_lr[o]),
            }
            lru_prm = {
                "conv_w": lru_conv_w[o],
                "conv_b": row(lru_conv_b[o]),
                "w_gate": (0.5 * jnp.concatenate([_block_diag(lru_w_r[o, 0]), _block_diag(lru_w_i[o, 0]),
                                                  _block_diag(lru_w_r[o, 1]), _block_diag(lru_w_i[o, 1])], axis=1)).astype(BF16),
                "b_gate": 0.5 * row(jnp.stack([lru_b_r[o, 0], lru_b_i[o, 0], lru_b_r[o, 1], lru_b_i[o, 1]])),
                "lam": lru_lambda[o],
            }
            consts = [row(gla_norm[o])] + tail
            for path in ("prompt", "sample"):
                if path == "prompt":
                    x, r0, rs, h0, s0 = xp, 0, 0, None, None
                else:
                    x, r0, rs, h0 = xs, 1, 1, state_lru_h[:, o]
                    s0 = state_gla_S[:, o].reshape(Bs, N_DIR, GLA_KW, GLA_DV)
                rg, gq, gk, gv, gg, log_a, *lru_coeffs = _inproj_odd_call(x, mod, r0, rs, prm, lru_prm)
                lru, h_last = _lru_call(lru_coeffs, h0)
                of, ob, s_fin = _gla_call(gq, gk, gv, log_a, sel3, lvl_mask, s0)
                x_new = _post_call(_post_odd_kernel, "post_odd", x, [lru, rg, of, ob, gg], mod, r0, rs, consts)
                if path == "prompt":
                    xp = x_new
                    outs.setdefault("h", []).append(h_last)
                    outs.setdefault("S", []).append(s_fin.reshape(Bp, N_DIR, GLA_HEADS, GLA_DK, GLA_DV))
                else:
                    xs = x_new

    stack = lambda name: jnp.stack(outs[name], axis=1)
    return (xp, xs, stack("k"), stack("v"), stack("C"), stack("n"), stack("m"), stack("h"), stack("S"))
```

```python
import functools

import numpy as np
import jax
import jax.numpy as jnp
from jax import lax
from jax.experimental import pallas as pl
from jax.experimental.pallas import tpu as pltpu

F32 = jnp.float32
BF16 = jnp.bfloat16

D_MODEL = 1024
D_FF = 4 * D_MODEL
GRID_W = 64
EPS = 1e-6
LOG2_E = 1.4426950408889634
N_DIR = 2

ATT_HEADS = 8
ATT_KV_HEADS = 2
HEAD_DIM = 64
ATT_WIDTH = ATT_HEADS * HEAD_DIM
ATT_KV_WIDTH = ATT_KV_HEADS * HEAD_DIM
ROPE_THETA = 10000.0
ROPE_PAIRS_PER_AXIS = HEAD_DIM // 4

MLSTM_HEADS = 4
MLSTM_HD = 128
MLSTM_WIDTH = MLSTM_HEADS * MLSTM_HD
MLSTM_CHUNK = 256
MLSTM_STAGE_LAG = 2
M_INIT = -1e30

LRU_WIDTH = 512
LRU_C = 8.0
CONV_W = 4
CONV_LEFT = 2

GLA_HEADS = 4
GLA_DK = 64
GLA_DV = 128
GLA_KW = GLA_HEADS * GLA_DK
GLA_WIDTH = GLA_HEADS * GLA_DV
GLA_TAU = 16.0
GLA_CHUNK = 64
GLA_STEP_CHUNKS = 4

LANES = 128
SUBLANES = 8
VMEM_LIMIT = 56 * 1024 * 1024

TOKEN_TILE = 512
ATTN_Q_TILE = 256
ATTN_SCORES_AHEAD = 1
ATTN_DEN_ON_MXU_MIN_KEYS = 1024
FF_CHUNK = 1024
POST_TILE = 512
POST_SUBTILE = 256
ODD_GROUP_ROWS = 512
CONV_HALO = 2 * SUBLANES


def _cparams(*sem):
    return pltpu.CompilerParams(dimension_semantics=sem, vmem_limit_bytes=VMEM_LIMIT)


def _const_spec(shape):
    n = len(shape)
    return pl.BlockSpec(shape, lambda *_: (0,) * n, pipeline_mode=pl.Buffered(1))


def _dot(a, b):
    return jnp.dot(a, b, preferred_element_type=F32)


def _dot_nt(a, b):
    return lax.dot_general(a, b, (((1,), (1,)), ((), ())), preferred_element_type=F32)


def _split3(x):
    hi = x.astype(BF16)
    r = x - hi.astype(F32)
    mid = r.astype(BF16)
    lo = (r - mid.astype(F32)).astype(BF16)
    return hi, mid, lo


def _split2(x):
    hi = x.astype(BF16)
    lo = (x - hi.astype(F32)).astype(BF16)
    return hi, lo


def _rms(x, g):
    return x * lax.rsqrt(jnp.mean(x * x, axis=-1, keepdims=True) + EPS) * g


def _log_sigmoid(x):
    return jnp.minimum(x, 0.0) - jnp.log1p(jnp.exp(-jnp.abs(x)))


def _softplus(x):
    return jnp.maximum(x, 0.0) + jnp.log1p(jnp.exp(-jnp.abs(x)))


def _group_rms(y, gain, ones_bd, width):
    hi, lo = _split2(y * y)
    ss = _dot(hi, ones_bd) + _dot(lo, ones_bd)
    return y * lax.rsqrt(ss * (1.0 / width) + EPS) * gain


def _ada_kernel(c_ref, w_ref, b_ref, o_ref):
    s = jax.nn.silu(c_ref[...])
    o_ref[...] = _dot(s.astype(BF16), w_ref[...].astype(BF16)) + b_ref[...]


def _ada_call(cvec, ada_w, ada_b):
    depth = ada_w.shape[0]
    rows = cvec.shape[0]
    n_col = ada_w.shape[2] // D_MODEL
    return pl.pallas_call(
        _ada_kernel,
        grid=(depth, n_col),
        in_specs=[
            pl.BlockSpec((rows, D_MODEL), lambda l, j: (0, 0)),
            pl.BlockSpec((None, D_MODEL, D_MODEL), lambda l, j: (l, 0, j)),
            pl.BlockSpec((None, 1, D_MODEL), lambda l, j: (l, 0, j)),
        ],
        out_specs=pl.BlockSpec((None, rows, D_MODEL), lambda l, j: (l, 0, j)),
        out_shape=jax.ShapeDtypeStruct((depth, rows, ada_w.shape[2]), F32),
        compiler_params=_cparams("arbitrary", "arbitrary"),
        name="ada_mod",
    )(cvec, ada_w, ada_b.reshape(depth, 1, -1))


def _mod_spec(row0, row_stride):
    return pl.BlockSpec((None, 6, 1, D_MODEL), lambda b, i: (row0 + row_stride * b, 0, 0, 0))


def _tok_spec(tm, width):
    return pl.BlockSpec((None, tm, width), lambda b, i: (b, i, 0))


def _rope(y, cos, sin_signed):
    width = y.shape[1]
    reps = width // LANES
    cosw = jnp.concatenate([cos] * reps, axis=1) if reps > 1 else cos
    sinw = jnp.concatenate([sin_signed] * reps, axis=1) if reps > 1 else sin_signed
    lane = lax.broadcasted_iota(jnp.int32, y.shape, 1)
    first_half = (lane % HEAD_DIM) < (HEAD_DIM // 2)
    partner = jnp.where(first_half, pltpu.roll(y, width - HEAD_DIM // 2, axis=1), pltpu.roll(y, HEAD_DIM // 2, axis=1))
    return y * cosw + partner * sinw


def _kv_variants(a, ones_lane=False):
    lane = lax.broadcasted_iota(jnp.int32, a.shape, 1)
    low = lane < HEAD_DIM
    swapped = pltpu.roll(a, HEAD_DIM, axis=1)
    pad_hi = jnp.where(lane == HEAD_DIM, 1.0, 0.0) if ones_lane else jnp.zeros_like(a)
    pad_lo = jnp.where(lane == 0, 1.0, 0.0) if ones_lane else jnp.zeros_like(a)
    return jnp.concatenate([
        jnp.where(low, a, pad_hi),
        jnp.where(low, pad_lo, swapped),
        jnp.where(low, swapped, pad_hi),
        jnp.where(low, pad_lo, a),
    ], axis=1)


def _inproj_even_kernel(*refs, rope):
    if rope:
        (x_ref, mod_ref, g_ref, wqkv_ref, wm_ref, wmt_ref, qg_ref, kg_ref, gbt_ref, ones_ref, cos_ref, sin_ref,
         q_out, kk_out, vv_out, k_out, v_out, mqt_out, mk_out, mvt_out, mo_out, gatet_out) = refs
    else:
        (x_ref, mod_ref, g_ref, wqkv_ref, wm_ref, wmt_ref, qg_ref, kg_ref, gbt_ref, ones_ref,
         q_out, kk_out, vv_out, k_out, v_out, mqt_out, mk_out, mvt_out, mo_out, gatet_out) = refs
    h = _rms(x_ref[...], g_ref[...]) * (1.0 + mod_ref[1]) + mod_ref[0]
    hb = h.astype(BF16)

    qkv = _dot(hb, wqkv_ref[...])
    w = MLSTM_WIDTH
    mm = _dot(hb, wm_ref[...])
    mk_out[...] = mm[:, :w].astype(BF16)
    mo_out[...] = mm[:, w:]
    q = _group_rms(qkv[:, :ATT_WIDTH], qg_ref[...], ones_ref[...], HEAD_DIM)
    k = _group_rms(qkv[:, ATT_WIDTH:ATT_WIDTH + ATT_KV_WIDTH], kg_ref[...], ones_ref[:ATT_KV_WIDTH, :ATT_KV_WIDTH], HEAD_DIM)
    v = qkv[:, ATT_WIDTH + ATT_KV_WIDTH:]
    if rope:
        q = _rope(q, cos_ref[...], sin_ref[...])
        k = _rope(k, cos_ref[...], sin_ref[...])
    q_out[...] = (q * (HEAD_DIM ** -0.5 * LOG2_E)).astype(BF16)
    k_out[...] = k
    v_out[...] = v
    kk_out[...] = _kv_variants(k).astype(BF16)
    vv_out[...] = _kv_variants(v, ones_lane=True).astype(BF16)

    mmt = _dot_nt(wmt_ref[...], hb)
    mqt_out[...] = (mmt[:w] * MLSTM_HD ** -0.5).astype(BF16)
    mvt_out[...] = mmt[w:2 * w].astype(BF16)

    gates_t = mmt[2 * w:] + gbt_ref[...]
    sub = lax.broadcasted_iota(jnp.int32, gates_t.shape, 0)
    gatet_out[...] = jnp.where(sub < N_DIR * MLSTM_HEADS, gates_t, _log_sigmoid(gates_t))


def _inproj_even_call(x, mod, mod_row0, mod_stride, prm, rope_tabs):
    B, T, _ = x.shape
    tm = min(TOKEN_TILE, T)
    rope = rope_tabs is not None
    ins = [x, mod, prm["g_mix_pre"], prm["w_qkv"], prm["w_m"], prm["w_mt"], prm["q_gain"],
           prm["k_gain"], prm["gate_bias_t"], prm["ones64"]]
    in_specs = [_tok_spec(tm, D_MODEL), _mod_spec(mod_row0, mod_stride)] + [_const_spec(a.shape) for a in ins[2:]]
    if rope:
        ins += list(rope_tabs)
        in_specs += [pl.BlockSpec((tm, LANES), lambda b, i: (i, 0))] * 2
    n_gate = 2 * N_DIR * MLSTM_HEADS
    outs = [(ATT_WIDTH, BF16, True), (4 * LANES, BF16, True), (4 * LANES, BF16, True), (ATT_KV_WIDTH, F32, True),
            (ATT_KV_WIDTH, F32, True), (MLSTM_WIDTH, BF16, False), (MLSTM_WIDTH, BF16, True), (MLSTM_WIDTH, BF16, False),
            (MLSTM_WIDTH, F32, True), (n_gate, F32, False)]
    feat_spec = lambda w: pl.BlockSpec((None, w, tm), lambda b, i: (b, 0, i))
    return pl.pallas_call(
        functools.partial(_inproj_even_kernel, rope=rope),
        grid=(B, T // tm),
        in_specs=in_specs,
        out_specs=[_tok_spec(tm, w) if tok else feat_spec(w) for w, _, tok in outs],
        out_shape=[jax.ShapeDtypeStruct((B, T, w) if tok else (B, w, T), dt) for w, dt, tok in outs],
        compiler_params=_cparams("parallel", "parallel"),
        name="inproj_even",
    )(*ins)


def _attn_kernel(*refs, cached):
    if cached:
        q_ref, kk_ref, vv_ref, kc_ref, vc_ref, o_ref = refs
    else:
        q_ref, kk_ref, vv_ref, o_ref = refs
    tq = q_ref.shape[0]
    group = ATT_HEADS // ATT_KV_HEADS
    units = [(kv, half) for kv in range(ATT_KV_HEADS) for half in range(2)]

    def unit_heads(kv, half):
        return [kv * group + half + 2 * i for i in range(group // 2)]

    sources = [(kk_ref, vv_ref)] + ([(kc_ref, vc_ref)] if cached else [])
    den_from_values = sum(k_ref.shape[0] for k_ref, _ in sources) >= ATTN_DEN_ON_MXU_MIN_KEYS

    def scores(kv, half):
        qs = jnp.concatenate([q_ref[:, (h // 2) * LANES:(h // 2 + 1) * LANES] for h in unit_heads(kv, half)], axis=0)
        col = (2 * kv + half) * LANES
        return [_dot_nt(qs, k_ref[:, col:col + LANES]) for k_ref, _ in sources]

    def weighted_values(kv, half, s_blocks):
        col = (2 * kv + half) * LANES
        m = None
        for s in s_blocks:
            bm = jnp.max(s, axis=-1, keepdims=True)
            m = bm if m is None else jnp.maximum(m, bm)
        o = None
        den = None
        for s, (_, v_ref) in zip(s_blocks, sources):
            p = jnp.exp2(s - m)
            if not den_from_values:
                bd = jnp.sum(p, axis=-1, keepdims=True)
                den = bd if den is None else den + bd
            bo = _dot(p.astype(BF16), v_ref[:, col:col + LANES])
            o = bo if o is None else o + bo
        own_low = half == 0
        if den_from_values:
            den_lane = HEAD_DIM if own_low else 0
            den = o[:, den_lane:den_lane + 1]
        lane = lax.broadcasted_iota(jnp.int32, o.shape, 1)
        own = (lane < HEAD_DIM) if own_low else (lane >= HEAD_DIM)
        return jnp.where(own, o / den, 0.0)

    pending = [scores(*u) for u in units[:ATTN_SCORES_AHEAD]]
    acc = {}
    for idx, (kv, half) in enumerate(units):
        if idx + ATTN_SCORES_AHEAD < len(units):
            pending.append(scores(*units[idx + ATTN_SCORES_AHEAD]))
        o = weighted_values(kv, half, pending[idx])
        for i, h in enumerate(unit_heads(kv, half)):
            pair = h // 2
            part = o[i * tq:(i + 1) * tq]
            if pair in acc:
                o_ref[:, pair * LANES:(pair + 1) * LANES] = (acc.pop(pair) + part).astype(o_ref.dtype)
            else:
                acc[pair] = part


def _attn_call(q, kk, vv, cache=None):
    B, T, _ = q.shape
    tq = min(ATTN_Q_TILE, T)
    cached = cache is not None
    ins = [q, kk, vv]
    full = lambda n: pl.BlockSpec((None, n, 4 * LANES), lambda b, i: (b, 0, 0))
    in_specs = [_tok_spec(tq, ATT_WIDTH), full(T), full(T)]
    if cached:
        ins += list(cache)
        in_specs += [full(cache[0].shape[1])] * 2
    return pl.pallas_call(
        functools.partial(_attn_kernel, cached=cached),
        grid=(B, T // tq),
        in_specs=in_specs,
        out_specs=_tok_spec(tq, ATT_WIDTH),
        out_shape=jax.ShapeDtypeStruct((B, T, ATT_WIDTH), BF16),
        compiler_params=_cparams("parallel", "parallel"),
        name="attention",
    )(*ins)


def _mlstm_kernel(*refs, zero_init):
    if zero_init:
        (qf_ref, kf_ref, vf_ref, gtf_ref, qb_ref, kb_ref, vb_ref, gtb_ref, trit_ref,
         hf_ref, hb_ref, c_ref, n_ref, m_ref) = refs
    else:
        (qf_ref, kf_ref, vf_ref, gtf_ref, qb_ref, kb_ref, vb_ref, gtb_ref, trit_ref,
         c0_ref, n0_ref, m0_ref, hf_ref, hb_ref, c_ref, n_ref, m_ref) = refs
    L = MLSTM_CHUNK
    H = MLSTM_HEADS
    HD = MLSTM_HD
    PAD = 2 * SUBLANES
    chunk = pl.program_id(1)

    @pl.when(chunk == 0)
    def _init():
        if zero_init:
            c_ref[...] = jnp.zeros(c_ref.shape, F32)
            n_ref[...] = jnp.zeros(n_ref.shape, F32)
            m_ref[...] = jnp.full(m_ref.shape, M_INIT, F32)
        else:
            for d in range(N_DIR):
                for hd in range(H):
                    c_ref[d, hd] = c0_ref[d, hd].T
            n_ref[...] = n0_ref[...]
            m_ref[...] = m0_ref[...]

    m_all = m_ref[...]
    n_all = n_ref[...]
    m_row = m_all
    m_lane = lax.broadcasted_iota(jnp.int32, m_all.shape, 1)
    key = lax.broadcasted_iota(jnp.int32, (L, L), 0)
    qry = lax.broadcasted_iota(jnp.int32, (L, L), 1)
    dirs = ((qf_ref, kf_ref, vf_ref, gtf_ref, hf_ref, key <= qry, L - 1),
            (qb_ref, kb_ref, vb_ref, gtb_ref, hb_ref, key >= qry, 0))
    n_in = N_DIR * H
    order = [(d, hd) for d in range(N_DIR) for hd in range(H)]

    def query_stage(d, hd):
        qt_ref, k_ref = dirs[d][0], dirs[d][1]
        j = d * H + hd
        sl = slice(hd * HD, (hd + 1) * HD)
        n_blk = jnp.broadcast_to(n_all[j:j + 1, :], (PAD, HD)).astype(BF16)
        return _dot(jnp.concatenate([k_ref[:, sl], c_ref[d, hd].astype(BF16), n_blk], axis=0), qt_ref[sl, :])

    sums = []
    for d, (_, _, _, gt_ref, _, _, _) in enumerate(dirs):
        gates_t = gt_ref[...]
        bcum_t = _dot(jnp.concatenate(_split3(gates_t), axis=1), trit_ref[d])
        sums.append((gates_t, bcum_t))
    c_rows = [sums[d][0][d * H:(d + 1) * H] - sums[d][1][n_in + d * H:n_in + (d + 1) * H] for d in range(N_DIR)]
    c_cols = jnp.concatenate(c_rows, axis=0).T

    def first_stage(d, hd, r):
        _, k_ref, vt_ref, _, _, mask, last = dirs[d]
        gates_t, bcum_t = sums[d]
        j = d * H + hd
        jf = n_in + j
        sl = slice(hd * HD, (hd + 1) * HD)
        k = k_ref[:, sl]
        v_t = vt_ref[sl, :]
        b_row = bcum_t[jf:jf + 1, :]
        i_row = gates_t[j:j + 1, :]
        c_col = c_cols[:, j:j + 1]
        m_prev = m_all[0:1, j:j + 1]
        ct_prev = c_ref[d, hd]
        n_prev = n_all[j:j + 1, :]

        g = b_row + m_prev
        dlog = jnp.where(mask, c_col + b_row, -jnp.inf)
        m_t = jnp.maximum(g, jnp.max(dlog, axis=0, keepdims=True))
        w = jnp.exp(dlog - m_t)
        w_inter = jnp.exp(g - m_t)

        b_last = bcum_t[jf:jf + 1, last:last + 1]
        wlog = b_last - b_row + i_row
        m_new = jnp.maximum(b_last + m_prev, jnp.max(wlog, axis=1, keepdims=True))
        ws = jnp.exp(wlog - m_new)
        decay = jnp.exp(b_last + m_prev - m_new)
        vw = (v_t.astype(F32) * ws).astype(BF16)
        ws_blk = jnp.broadcast_to(ws, (PAD, L)).astype(BF16)
        upd = _dot(jnp.concatenate([vw, ws_blk], axis=0), k)
        c_new = decay * ct_prev + upd[:HD]
        n_new = decay * n_prev + upd[HD:HD + 1]
        return dict(d=d, hd=hd, j=j, r=r, w=w, w_inter=w_inter, m_t=m_t, v_t=v_t, c_new=c_new, n_new=n_new, m_new=m_new)

    def second_stage(s):
        h_ref = dirs[s["d"]][4]
        sl = slice(s["hd"] * HD, (s["hd"] + 1) * HD)
        r = s["r"]
        qk = r[:L] * s["w"]
        num = _dot(s["v_t"], qk.astype(BF16)) + s["w_inter"] * r[L:L + HD]
        den = jnp.sum(qk, axis=0, keepdims=True) + s["w_inter"] * r[L + HD:L + HD + 1]
        h_ref[:, sl] = (num / jnp.maximum(jnp.abs(den), jnp.exp(-s["m_t"]))).T

    staged = []
    for idx, (d, hd) in enumerate(order):
        staged.append(first_stage(d, hd, query_stage(d, hd)))
        if idx >= MLSTM_STAGE_LAG:
            second_stage(staged[idx - MLSTM_STAGE_LAG])
    for s in staged[len(order) - MLSTM_STAGE_LAG:]:
        second_stage(s)
    for s in staged:
        c_ref[s["d"], s["hd"]] = s["c_new"]
        m_row = jnp.where(m_lane == s["j"], s["m_new"], m_row)
    n_ref[...] = jnp.concatenate([s["n_new"] for s in staged], axis=0)
    m_ref[...] = m_row

    @pl.when(chunk == pl.num_programs(1) - 1)
    def _finish():
        for d in range(N_DIR):
            for hd in range(H):
                c_ref[d, hd] = c_ref[d, hd].T


def _mlstm_call(mq_t, mk, mv_t, gates_t, tri_t, state0):
    B, T, _ = mk.shape
    L = MLSTM_CHUNK
    nc = T // L
    zero_init = state0 is None
    fwd = lambda w: pl.BlockSpec((None, L, w), lambda b, c: (b, c, 0))
    bwd = lambda w: pl.BlockSpec((None, L, w), lambda b, c: (b, nc - 1 - c, 0))
    fwd_t = lambda w: pl.BlockSpec((None, w, L), lambda b, c: (b, 0, c))
    bwd_t = lambda w: pl.BlockSpec((None, w, L), lambda b, c: (b, 0, nc - 1 - c))
    W = MLSTM_WIDTH
    NG = 2 * N_DIR * MLSTM_HEADS
    c_spec = pl.BlockSpec((None, N_DIR, MLSTM_HEADS, MLSTM_HD, MLSTM_HD), lambda b, c: (b, 0, 0, 0, 0))
    n_spec = pl.BlockSpec((None, N_DIR * MLSTM_HEADS, MLSTM_HD), lambda b, c: (b, 0, 0))
    m_spec = pl.BlockSpec((None, 1, LANES), lambda b, c: (b, 0, 0))
    ins = [mq_t, mk, mv_t, gates_t, mq_t, mk, mv_t, gates_t, tri_t]
    in_specs = [fwd_t(W), fwd(W), fwd_t(W), fwd_t(NG), bwd_t(W), bwd(W), bwd_t(W), bwd_t(NG), _const_spec(tri_t.shape)]
    if not zero_init:
        ins += list(state0)
        in_specs += [c_spec, n_spec, m_spec]
    return pl.pallas_call(
        functools.partial(_mlstm_kernel, zero_init=zero_init),
        grid=(B, nc),
        in_specs=in_specs,
        out_specs=[fwd(W), bwd(W), c_spec, n_spec, m_spec],
        out_shape=[jax.ShapeDtypeStruct((B, T, W), F32), jax.ShapeDtypeStruct((B, T, W), F32),
                   jax.ShapeDtypeStruct((B, N_DIR, MLSTM_HEADS, MLSTM_HD, MLSTM_HD), F32),
                   jax.ShapeDtypeStruct((B, N_DIR * MLSTM_HEADS, MLSTM_HD), F32),
                   jax.ShapeDtypeStruct((B, 1, LANES), F32)],
        compiler_params=_cparams("parallel", "arbitrary"),
        name="mlstm",
    )(*ins)


def _residual_ffn(x_ref, mix_fn, mod_ref, gpost_ref, gfpre_ref, gfpost_ref, wout_ref, w1_ref, w2_ref, o_ref):
    half = D_MODEL // 2
    sub = min(POST_SUBTILE, x_ref.shape[0])
    rows = [slice(r0, r0 + sub) for r0 in range(0, x_ref.shape[0], sub)]
    ys = []
    for r in rows:
        mix_a, mix_b = mix_fn(r)
        ys.append(_dot(mix_a.astype(BF16), wout_ref[:half, :]) + _dot(mix_b.astype(BF16), wout_ref[half:, :]))
    for r, y in zip(rows, ys):
        x1 = x_ref[r, :] + mod_ref[2] * _rms(y, gpost_ref[...])
        h2 = (_rms(x1, gfpre_ref[...]) * (1.0 + mod_ref[4]) + mod_ref[3]).astype(BF16)
        chunks = list(range(0, D_FF, FF_CHUNK))
        up = _dot(h2, w1_ref[:, chunks[0]:chunks[0] + FF_CHUNK])
        f = None
        for i, c0 in enumerate(chunks):
            cur = up
            if i + 1 < len(chunks):
                up = _dot(h2, w1_ref[:, chunks[i + 1]:chunks[i + 1] + FF_CHUNK])
            part = _dot(jnp.square(jnp.maximum(cur, 0.0)).astype(BF16), w2_ref[c0:c0 + FF_CHUNK, :])
            f = part if f is None else f + part
        o_ref[r, :] = x1 + mod_ref[5] * _rms(f, gfpost_ref[...])


def _head_rms(y, gain_ref, width):
    outs = []
    for h0 in range(0, y.shape[1], width):
        seg = y[:, h0:h0 + width]
        outs.append(seg * lax.rsqrt(jnp.mean(seg * seg, axis=-1, keepdims=True) + EPS) * gain_ref[:, h0:h0 + width])
    return jnp.concatenate(outs, axis=1)


def _post_even_kernel(x_ref, attn_ref, hf_ref, hb_ref, mo_ref, mod_ref, gm_ref, gpost_ref, gfpre_ref, gfpost_ref,
                      wout_ref, w1_ref, w2_ref, o_ref):
    def mix(r):
        hm = _head_rms(hf_ref[r, :] + hb_ref[r, :], gm_ref, MLSTM_HD) * jax.nn.sigmoid(mo_ref[r, :])
        return attn_ref[r, :], hm

    _residual_ffn(x_ref, mix, mod_ref, gpost_ref, gfpre_ref, gfpost_ref, wout_ref, w1_ref, w2_ref, o_ref)


def _post_odd_kernel(x_ref, lru_ref, rg_ref, of_ref, ob_ref, gg_ref, mod_ref, gm_ref, gpost_ref, gfpre_ref, gfpost_ref,
                     wout_ref, w1_ref, w2_ref, o_ref):
    def mix(r):
        lru = jax.nn.gelu(rg_ref[r, :]) * lru_ref[r, :]
        og = _head_rms(of_ref[r, :] + ob_ref[r, :], gm_ref, GLA_DV) * jax.nn.silu(gg_ref[r, :])
        return lru, og

    _residual_ffn(x_ref, mix, mod_ref, gpost_ref, gfpre_ref, gfpost_ref, wout_ref, w1_ref, w2_ref, o_ref)


def _post_call(kernel_fn, name, x, mix_ins, mod, mod_row0, mod_stride, consts):
    shape = x.shape
    if mod_stride == 0:
        x = x.reshape(1, -1, D_MODEL)
        mix_ins = [a.reshape(1, -1, a.shape[2]) for a in mix_ins]
    B, T, _ = x.shape
    tm = min(POST_TILE, T)
    ins = [x] + list(mix_ins) + [mod] + list(consts)
    in_specs = ([_tok_spec(tm, D_MODEL)] + [_tok_spec(tm, a.shape[2]) for a in mix_ins] + [_mod_spec(mod_row0, mod_stride)]
                + [_const_spec(a.shape) for a in consts])
    return pl.pallas_call(
        kernel_fn,
        grid=(B, T // tm),
        in_specs=in_specs,
        out_specs=_tok_spec(tm, D_MODEL),
        out_shape=jax.ShapeDtypeStruct(x.shape, F32),
        compiler_params=_cparams("parallel", "parallel"),
        name=name,
    )(*ins).reshape(shape)


def _inproj_odd_kernel(x_ref, xp_ref, xn_ref, mod_ref, g_ref, wmain_ref, wlr_in_ref, wlr_hi_ref, wlr_lo_ref, blr_ref,
                       cw_ref, cb_ref, wg_ref, bg_ref, lam_ref,
                       rg_out, gq_out, gk_out, gv_out, gg_out, la_out, af_out, uf_out, ab_out, ub_out):
    tm = x_ref.shape[0]
    W = LRU_WIDTH
    sub = min(ODD_GROUP_ROWS, tm)
    n_groups = tm // sub
    m = sub + 2 * CONV_HALO
    first = pl.program_id(1) == 0
    last = pl.program_id(1) == pl.num_programs(1) - 1
    rid = lax.broadcasted_iota(jnp.int32, (m, W), 0)
    neg_half_c_sp = (-0.5 * LRU_C) * _softplus(-lam_ref[...])

    def modulated(xv):
        return (_rms(xv, g_ref[...]) * (1.0 + mod_ref[1]) + mod_ref[0]).astype(BF16)

    for q in range(n_groups):
        rows = slice(q * sub, (q + 1) * sub)
        prev = xp_ref[...] if q == 0 else x_ref[q * sub - CONV_HALO:q * sub, :]
        nxt = xn_ref[...] if q == n_groups - 1 else x_ref[(q + 1) * sub:(q + 1) * sub + CONV_HALO, :]
        hb_ext = modulated(jnp.concatenate([prev, x_ref[rows, :], nxt], axis=0))
        hb = hb_ext[CONV_HALO:CONV_HALO + sub]

        glr_hi, glr_lo = _split2(_dot(hb, wlr_in_ref[...]))

        rx = _dot(hb_ext, wmain_ref[:, :W])
        outside = None
        if q == 0:
            outside = first & (rid < CONV_HALO)
        if q == n_groups - 1:
            tail = last & (rid >= sub + CONV_HALO)
            outside = tail if outside is None else outside | tail
        if outside is not None:
            rx = jnp.where(outside, 0.0, rx)

        mm = _dot(hb, wmain_ref[:, W:3 * W])
        rg_out[rows, :] = mm[:, :W]
        gq_out[rows, :] = mm[:, W:W + GLA_KW] * GLA_DK ** -0.5
        gk_out[rows, :] = mm[:, W + GLA_KW:]

        z = _dot(glr_hi, wlr_hi_ref[...]) + (_dot(glr_hi, wlr_lo_ref[...]) + _dot(glr_lo, wlr_hi_ref[...])) + blr_ref[...]
        la_out[rows, :] = _log_sigmoid(z) * (1.0 / GLA_TAU)

        xc = cb_ref[...] + rx * cw_ref[CONV_LEFT:CONV_LEFT + 1, :]
        for tap in range(CONV_W):
            off = tap - CONV_LEFT
            if off != 0:
                xc = xc + pltpu.roll(rx, (-off) % m, axis=0) * cw_ref[tap:tap + 1, :]
        xc = xc[CONV_HALO:CONV_HALO + sub, :]
        xcb = xc.astype(BF16)
        half_xc = 0.5 * xc
        for d, (a_out, u_out) in enumerate(((af_out, uf_out), (ab_out, ub_out))):
            cols = slice(2 * d * W, (2 * d + 2) * W)
            th = jnp.tanh(_dot(xcb, wg_ref[:, cols]) + bg_ref[:, cols])
            if d == 0:
                gv_out[rows, :] = _dot(hb, wmain_ref[:, 3 * W:4 * W]).astype(BF16)
            else:
                gg_out[rows, :] = _dot(hb, wmain_ref[:, 4 * W:])
            th_r = th[:, :W]
            th_i = th[:, W:]
            log_a = neg_half_c_sp[d:d + 1, :] * th_r + neg_half_c_sp[d:d + 1, :]
            a = jnp.exp(log_a)
            a_out[rows, :] = a
            v = jnp.tanh(log_a) * (-1.0 - a * a)
            u_out[rows, :] = jnp.where(v > 0.0, v * lax.rsqrt(v), 0.0) * (half_xc * th_i + half_xc)


def _inproj_odd_call(x, mod, mod_row0, mod_stride, prm, lru_prm):
    B, T, _ = x.shape
    tm = min(TOKEN_TILE, T)
    halo_per_tile = tm // CONV_HALO
    n_halo = T // CONV_HALO
    prev_spec = pl.BlockSpec((None, CONV_HALO, D_MODEL), lambda b, i: (b, jnp.maximum(i * halo_per_tile - 1, 0), 0))
    next_spec = pl.BlockSpec((None, CONV_HALO, D_MODEL),
                             lambda b, i: (b, jnp.minimum((i + 1) * halo_per_tile, n_halo - 1), 0))
    consts = [prm["g_mix_pre"], prm["w_main"], prm["w_lr_in"], prm["w_lr_hi"], prm["w_lr_lo"], prm["b_lr"],
              lru_prm["conv_w"], lru_prm["conv_b"], lru_prm["w_gate"], lru_prm["b_gate"], lru_prm["lam"]]
    ins = [x, x, x, mod] + consts
    in_specs = ([_tok_spec(tm, D_MODEL), prev_spec, next_spec, _mod_spec(mod_row0, mod_stride)]
                + [_const_spec(a.shape) for a in consts])
    widths = [(LRU_WIDTH, F32), (GLA_KW, F32), (GLA_KW, F32), (GLA_WIDTH, BF16), (GLA_WIDTH, F32),
              (N_DIR * GLA_KW, F32)] + [(LRU_WIDTH, F32)] * 4
    return pl.pallas_call(
        _inproj_odd_kernel,
        grid=(B, T // tm),
        in_specs=in_specs,
        out_specs=[_tok_spec(tm, w) for w, _ in widths],
        out_shape=[jax.ShapeDtypeStruct((B, T, w), dt) for w, dt in widths],
        compiler_params=_cparams("parallel", "parallel"),
        name="inproj_odd",
    )(*ins)


def _lru_kernel(*refs, zero_init):
    if zero_init:
        af_s, uf_s, ab_s, ub_s, o_ref, last_ref, hb_s = refs
    else:
        af_s, uf_s, ab_s, ub_s, h0_ref, o_ref, last_ref, hb_s = refs
    T, W = af_s.shape
    nblk = T // SUBLANES
    rid8 = lax.broadcasted_iota(jnp.int32, (SUBLANES, W), 0)

    def block_scan(a, u, reverse):
        for dist in (1, 2, 4):
            if reverse:
                keep = rid8 < SUBLANES - dist
                shift = SUBLANES - dist
            else:
                keep = rid8 >= dist
                shift = dist
            a_n = jnp.where(keep, pltpu.roll(a, shift, axis=0), 1.0)
            u_n = jnp.where(keep, pltpu.roll(u, shift, axis=0), 0.0)
            u = a * u_n + u
            a = a * a_n
        return a, u

    def body(blk, carry):
        hf, hb = carry
        rf = pl.multiple_of(blk * SUBLANES, SUBLANES)
        a, u = block_scan(af_s[pl.ds(rf, SUBLANES), :], uf_s[pl.ds(rf, SUBLANES), :], False)
        out_f = u + a * hf
        o_ref[pl.ds(rf, SUBLANES), :] = out_f
        rb = pl.multiple_of((nblk - 1 - blk) * SUBLANES, SUBLANES)
        a, u = block_scan(ab_s[pl.ds(rb, SUBLANES), :], ub_s[pl.ds(rb, SUBLANES), :], True)
        out_b = u + a * hb
        hb_s[pl.ds(rb, SUBLANES), :] = out_b
        return out_f[SUBLANES - 1:SUBLANES, :], out_b[0:1, :]

    if zero_init:
        init = (jnp.zeros((1, W), F32), jnp.zeros((1, W), F32))
    else:
        init = (h0_ref[0:1, :], h0_ref[1:2, :])
    hf, hb = lax.fori_loop(0, nblk, body, init)
    last_ref[0:1, :] = hf
    last_ref[1:2, :] = hb
    o_ref[...] = o_ref[...] + hb_s[...]


def _lru_call(coeffs, h0):
    B, T, W = coeffs[0].shape
    zero_init = h0 is None
    seq = pl.BlockSpec((None, T, W), lambda b: (b, 0, 0))
    st = pl.BlockSpec((None, N_DIR, W), lambda b: (b, 0, 0))
    ins = list(coeffs)
    in_specs = [seq] * len(ins)
    if not zero_init:
        ins.append(h0)
        in_specs.append(st)
    return pl.pallas_call(
        functools.partial(_lru_kernel, zero_init=zero_init),
        grid=(B,),
        in_specs=in_specs,
        out_specs=[seq, st],
        out_shape=[jax.ShapeDtypeStruct((B, T, W), F32), jax.ShapeDtypeStruct((B, N_DIR, W), F32)],
        scratch_shapes=[pltpu.VMEM((T, W), F32)],
        compiler_params=_cparams("parallel"),
        name="rglru",
    )(*ins)


GLA_LEVELS = GLA_CHUNK.bit_length() - 1


def _gla_constants():
    L = GLA_CHUNK
    idx = np.arange(L)
    sel = np.zeros((N_DIR, (GLA_LEVELS + 3) * L, L), np.float32)
    lvl_mask = np.zeros((N_DIR, GLA_LEVELS + 1, L, L), np.float32)
    for d in range(N_DIR):
        tri = (idx[None, :] <= idx[:, None]) if d == 0 else (idx[None, :] >= idx[:, None])
        tri = tri.astype(np.float32)
        last = L - 1 if d == 0 else 0
        sel[d, 0:L] = tri
        for lv in range(GLA_LEVELS):
            m = L >> (lv + 1)
            start = (idx // (2 * m)) * (2 * m)
            second = (idx - start) >= m
            boundary = start + (m - 1 if d == 0 else m)
            query_role = second if d == 0 else ~second
            diff = tri - tri[boundary]
            sel[d, (1 + lv) * L:(2 + lv) * L] = np.where(query_role[:, None], diff, -diff)
            same = start[:, None] == start[None, :]
            lvl_mask[d, lv] = same & query_role[:, None] & ~query_role[None, :]
        lvl_mask[d, GLA_LEVELS] = np.eye(L)
        sel[d, (GLA_LEVELS + 1) * L:(GLA_LEVELS + 2) * L] = tri[last][None, :] - tri
        sel[d, (GLA_LEVELS + 2) * L:] = tri[last][None, :]
    assert sel.min() >= 0.0 and sel.max() <= 1.0
    sel3 = np.concatenate([sel] * 3, axis=2)
    lvl_mask = np.tile(lvl_mask, (1, 1, 1, GLA_HEADS))
    return jnp.asarray(sel3, BF16), jnp.asarray(lvl_mask, F32)


def _gla_kernel(*refs, zero_init):
    if zero_init:
        (qf_ref, kf_ref, vf_ref, af_ref, qb_ref, kb_ref, vb_ref, ab_ref, sel_ref, msk_ref, of_ref, ob_ref, s_ref) = refs
    else:
        (qf_ref, kf_ref, vf_ref, af_ref, qb_ref, kb_ref, vb_ref, ab_ref, sel_ref, msk_ref, s0_ref,
         of_ref, ob_ref, s_ref) = refs
    L = GLA_CHUNK
    H = GLA_HEADS
    KW = GLA_KW
    VW = GLA_WIDTH
    DK = GLA_DK
    DV = GLA_DV

    @pl.when(pl.program_id(1) == 0)
    def _init():
        if zero_init:
            s_ref[...] = jnp.zeros(s_ref.shape, F32)
        else:
            s_ref[...] = s0_ref[...]

    zero_k = jnp.zeros((L, KW), BF16)
    zero_v = jnp.zeros((L, VW), BF16)
    lane_k = lax.broadcasted_iota(jnp.int32, (L, KW), 1) // DK
    lane_v = lax.broadcasted_iota(jnp.int32, (L, VW), 1) // DV

    def stack_heads(a, lane_head, zero):
        return jnp.concatenate([jnp.where(lane_head == h, a, zero) for h in range(H)], axis=0)

    def tile_rows(a):
        return jnp.concatenate([a] * H, axis=0)

    dirs = ((qf_ref, kf_ref, vf_ref, af_ref, of_ref), (qb_ref, kb_ref, vb_ref, ab_ref, ob_ref))
    work = []
    for i in range(GLA_STEP_CHUNKS):
        for d, (q_ref, k_ref, v_ref, a_ref, o_ref) in enumerate(dirs):
            j = i if d == 0 else GLA_STEP_CHUNKS - 1 - i
            rows = slice(j * L, (j + 1) * L)
            sums = _dot(sel_ref[d], jnp.concatenate(_split3(a_ref[rows, :]), axis=0))
            q_b = q_ref[rows, :].astype(BF16)
            work.append(dict(d=d, rows=rows, sums=sums, q_b=q_b, q_st=stack_heads(q_b, lane_k, zero_k),
                             k_st=stack_heads(k_ref[rows, :].astype(BF16), lane_k, zero_k)))

    for lv in range(GLA_LEVELS + 1):
        for wk in work:
            if lv == GLA_LEVELS:
                part = _dot_nt(wk["q_b"], wk["k_st"])
            else:
                wl = jnp.exp(wk["sums"][(1 + lv) * L:(2 + lv) * L]).astype(BF16)
                part = _dot_nt(wk["q_b"] * wl, wk["k_st"] * tile_rows(wl))
            part = part * msk_ref[wk["d"], lv]
            wk["scores"] = part if lv == 0 else wk["scores"] + part

    state = [s_ref[d] for d in range(N_DIR)]
    for wk in work:
        d, rows, sums = wk["d"], wk["rows"], wk["sums"]
        _, k_ref, v_ref, _, o_ref = dirs[d]
        v = v_ref[rows, :]
        s_prev = state[d]
        intra = _dot(wk["scores"].astype(BF16), stack_heads(v, lane_v, zero_v))
        q_in = wk["q_st"] * tile_rows(jnp.exp(sums[0:L]).astype(BF16))
        inter = _dot(q_in, s_prev.astype(BF16))
        o_ref[rows, :] = intra + jnp.concatenate([inter[h * L:(h + 1) * L] for h in range(H)], axis=1)

        k_out = k_ref[rows, :] * jnp.exp(sums[(GLA_LEVELS + 1) * L:(GLA_LEVELS + 2) * L])
        dec = jnp.exp(sums[(GLA_LEVELS + 2) * L:])
        both_t = jnp.concatenate([k_out, dec], axis=0).T
        k_out_t = both_t[:, :L].astype(BF16)
        upd = [_dot(k_out_t[h * DK:(h + 1) * DK], v[:, h * DV:(h + 1) * DV]) for h in range(H)]
        state[d] = both_t[:, L:L + 1] * s_prev + jnp.concatenate(upd, axis=0)
    for d in range(N_DIR):
        s_ref[d] = state[d]


def _gla_call(gq, gk, gv, log_a, sel3, lvl_mask, s0):
    B, T, _ = gq.shape
    L = GLA_CHUNK * GLA_STEP_CHUNKS
    nc = T // L
    zero_init = s0 is None
    fwd = lambda w: pl.BlockSpec((None, L, w), lambda b, c: (b, c, 0))
    bwd = lambda w: pl.BlockSpec((None, L, w), lambda b, c: (b, nc - 1 - c, 0))
    a_fwd = pl.BlockSpec((None, L, GLA_KW), lambda b, c: (b, c, 0))
    a_bwd = pl.BlockSpec((None, L, GLA_KW), lambda b, c: (b, nc - 1 - c, 1))
    s_spec = pl.BlockSpec((None, N_DIR, GLA_KW, GLA_DV), lambda b, c: (b, 0, 0, 0))
    ins = [gq, gk, gv, log_a, gq, gk, gv, log_a, sel3, lvl_mask]
    in_specs = [fwd(GLA_KW), fwd(GLA_KW), fwd(GLA_WIDTH), a_fwd, bwd(GLA_KW), bwd(GLA_KW), bwd(GLA_WIDTH), a_bwd,
                _const_spec(sel3.shape), _const_spec(lvl_mask.shape)]
    if not zero_init:
        ins.append(s0)
        in_specs.append(s_spec)
    return pl.pallas_call(
        functools.partial(_gla_kernel, zero_init=zero_init),
        grid=(B, nc),
        in_specs=in_specs,
        out_specs=[fwd(GLA_WIDTH), bwd(GLA_WIDTH), s_spec],
        out_shape=[jax.ShapeDtypeStruct((B, T, GLA_WIDTH), F32), jax.ShapeDtypeStruct((B, T, GLA_WIDTH), F32),
                   jax.ShapeDtypeStruct((B, N_DIR, GLA_KW, GLA_DV), F32)],
        compiler_params=_cparams("parallel", "arbitrary"),
        name="gla",
    )(*ins)


def _block_diag(blocks):
    n, r, c = blocks.shape
    eye = jnp.eye(n, dtype=blocks.dtype)
    return (eye[:, None, :, None] * blocks[:, :, None, :]).reshape(n * r, n * c)


def _pad_cols(a, width):
    return jnp.pad(a, ((0, 0), (0, width - a.shape[1])))


def _rope_tables(n_tokens):
    rows = n_tokens // GRID_W
    row = jnp.repeat(jnp.arange(rows), GRID_W).astype(F32)
    col = jnp.tile(jnp.arange(GRID_W), rows).astype(F32)
    inv = jnp.power(ROPE_THETA, -jnp.arange(ROPE_PAIRS_PER_AXIS, dtype=F32) / ROPE_PAIRS_PER_AXIS)
    ang = jnp.concatenate([row[:, None] * inv, col[:, None] * inv], axis=-1)
    cos, sin = jnp.cos(ang), jnp.sin(ang)
    reps = LANES // HEAD_DIM
    return jnp.tile(jnp.concatenate([cos, cos], axis=-1), (1, reps)), jnp.tile(jnp.concatenate([-sin, sin], axis=-1), (1, reps))


def _cache_variants(cache, ones_lane=False):
    z = jnp.zeros_like(cache[:, 0])
    if ones_lane:
        z = z.at[..., 0].set(1.0)
    return jnp.concatenate([cache[:, 0], z, z, cache[:, 0], cache[:, 1], z, z, cache[:, 1]], axis=-1).astype(BF16)


def _mlstm_tri():
    idx = np.arange(MLSTM_CHUNK)
    upper = idx[:, None] <= idx[None, :]
    tri_t = np.stack([upper, upper.T]).astype(np.float32)
    return jnp.asarray(np.concatenate([tri_t] * 3, axis=1), BF16)


def kernel(x_prompt, x_sample, cache_attn_k, cache_attn_v, state_mlstm_C, state_mlstm_n, state_mlstm_m, state_lru_h, state_gla_S, c, c_ctx, ada_w, ada_b, norm_mix_pre, norm_mix_post, norm_ffn_pre, norm_ffn_post, w_out, ffn_w1, ffn_w2, w_in_even, attn_q_norm, attn_k_norm, mlstm_i_bias, mlstm_f_bias, mlstm_norm, w_in_odd, lru_conv_w, lru_conv_b, lru_w_r, lru_b_r, lru_w_i, lru_b_i, lru_lambda, gla_w_lr, gla_b_lr, gla_norm):
    depth = ada_w.shape[0]
    Bp = x_prompt.shape[0]
    Bs = x_sample.shape[0]
    row = lambda a: a.reshape(1, -1)

    n_rows = -(-(1 + Bs) // SUBLANES) * SUBLANES
    cvec = jnp.zeros((n_rows, D_MODEL), F32).at[0].set(c_ctx).at[1:1 + Bs].set(c)
    mod_all = _ada_call(cvec, ada_w, ada_b).reshape(depth, n_rows, 6, 1, D_MODEL)

    rope_tabs = _rope_tables(x_sample.shape[1])
    ones64 = _block_diag(jnp.ones((ATT_HEADS, HEAD_DIM, HEAD_DIM), BF16))
    tri_t = _mlstm_tri()
    sel3, lvl_mask = _gla_constants()

    xp, xs = x_prompt, x_sample
    outs = {}
    for l in range(depth):
        mod = mod_all[l]
        tail = [row(norm_mix_post[l]), row(norm_ffn_pre[l]), row(norm_ffn_post[l]),
                w_out[l].astype(BF16), ffn_w1[l].astype(BF16), ffn_w2[l].astype(BF16)]
        if l % 2 == 0:
            e = l // 2
            w_in = w_in_even[e]
            o1 = ATT_WIDTH + 2 * ATT_KV_WIDTH
            o2 = o1 + 4 * MLSTM_WIDTH
            prm = {
                "g_mix_pre": row(norm_mix_pre[l]),
                "w_qkv": w_in[:, :o1].astype(BF16),
                "w_m": jnp.concatenate([w_in[:, o1 + MLSTM_WIDTH:o1 + 2 * MLSTM_WIDTH], w_in[:, o1 + 3 * MLSTM_WIDTH:o2]],
                                       axis=1).astype(BF16),
                "w_mt": jnp.concatenate([w_in[:, o1:o1 + MLSTM_WIDTH], w_in[:, o1 + 2 * MLSTM_WIDTH:o1 + 3 * MLSTM_WIDTH],
                                         w_in[:, o2:]], axis=1).T.astype(BF16),
                "gate_bias_t": jnp.concatenate([mlstm_i_bias[e].reshape(-1), mlstm_f_bias[e].reshape(-1)]).reshape(-1, 1),
                "q_gain": row(jnp.tile(attn_q_norm[e], ATT_HEADS)),
                "k_gain": row(jnp.tile(attn_k_norm[e], ATT_KV_HEADS)),
                "ones64": ones64,
            }
            consts = [row(mlstm_norm[e])] + tail
            for path in ("prompt", "sample"):
                if path == "prompt":
                    x, r0, rs, tabs, cache, st0 = xp, 0, 0, None, None, None
                else:
                    x, r0, rs, tabs = xs, 1, 1, rope_tabs
                    cache = (_cache_variants(cache_attn_k[:, e]), _cache_variants(cache_attn_v[:, e], ones_lane=True))
                    st0 = (state_mlstm_C[:, e], state_mlstm_n[:, e].reshape(Bs, N_DIR * MLSTM_HEADS, MLSTM_HD),
                           _pad_cols(state_mlstm_m[:, e].reshape(Bs, -1), LANES).reshape(Bs, 1, LANES))
                q, kk, vv, k_n, v_n, mq_t, mk, mv_t, mo, gates_t = _inproj_even_call(x, mod, r0, rs, prm, tabs)
                attn = _attn_call(q, kk, vv, cache)
                hf, hb, c_fin, n_fin, m_fin = _mlstm_call(mq_t, mk, mv_t, gates_t, tri_t, st0)
                x_new = _post_call(_post_even_kernel, "post_even", x, [attn, hf, hb, mo], mod, r0, rs, consts)
                if path == "prompt":
                    xp = x_new
                    T = x.shape[1]
                    outs.setdefault("k", []).append(jnp.swapaxes(k_n.reshape(Bp, T, ATT_KV_HEADS, HEAD_DIM), 1, 2))
                    outs.setdefault("v", []).append(jnp.swapaxes(v_n.reshape(Bp, T, ATT_KV_HEADS, HEAD_DIM), 1, 2))
                    outs.setdefault("C", []).append(c_fin)
                    outs.setdefault("n", []).append(n_fin.reshape(Bp, N_DIR, MLSTM_HEADS, MLSTM_HD))
                    outs.setdefault("m", []).append(m_fin[:, 0, :N_DIR * MLSTM_HEADS].reshape(Bp, N_DIR, MLSTM_HEADS))
                else:
                    xs = x_new
        else:
            o = l // 2
            w_in = w_in_odd[o]
            o1 = 2 * LRU_WIDTH + 2 * GLA_KW + 2 * GLA_WIDTH
            w_lr = _block_diag(gla_w_lr[o])
            w_lr = jnp.pad(w_lr, ((0, LANES - w_lr.shape[0]), (0, 0)))
            w_lr_hi = w_lr.astype(BF16)
            prm = {
                "g_mix_pre": row(norm_mix_pre[l]),
                "w_main": w_in[:, :o1].astype(BF16),
                "w_lr_in": _pad_cols(w_in[:, o1:], LANES).astype(BF16),
                "w_lr_hi": w_lr_hi,
                "w_lr_lo": (w_lr - w_lr_hi.astype(F32)).astype(BF16),
                "b_lr": row(gla_b_lr[o]),
            }
            lru_prm = {
                "conv_w": lru_conv_w[o],
                "conv_b": row(lru_conv_b[o]),
                "w_gate": (0.5 * jnp.concatenate([_block_diag(lru_w_r[o, 0]), _block_diag(lru_w_i[o, 0]),
                                                  _block_diag(lru_w_r[o, 1]), _block_diag(lru_w_i[o, 1])], axis=1)).astype(BF16),
                "b_gate": 0.5 * row(jnp.stack([lru_b_r[o, 0], lru_b_i[o, 0], lru_b_r[o, 1], lru_b_i[o, 1]])),
                "lam": lru_lambda[o],
            }
            consts = [row(gla_norm[o])] + tail
            for path in ("prompt", "sample"):
                if path == "prompt":
                    x, r0, rs, h0, s0 = xp, 0, 0, None, None
                else:
                    x, r0, rs, h0 = xs, 1, 1, state_lru_h[:, o]
                    s0 = state_gla_S[:, o].reshape(Bs, N_DIR, GLA_KW, GLA_DV)
                rg, gq, gk, gv, gg, log_a, *lru_coeffs = _inproj_odd_call(x, mod, r0, rs, prm, lru_prm)
                lru, h_last = _lru_call(lru_coeffs, h0)
                of, ob, s_fin = _gla_call(gq, gk, gv, log_a, sel3, lvl_mask, s0)
                x_new = _post_call(_post_odd_kernel, "post_odd", x, [lru, rg, of, ob, gg], mod, r0, rs, consts)
                if path == "prompt":
                    xp = x_new
                    outs.setdefault("h", []).append(h_last)
                    outs.setdefault("S", []).append(s_fin.reshape(Bp, N_DIR, GLA_HEADS, GLA_DK, GLA_DV))
                else:
                    xs = x_new

    stack = lambda name: jnp.stack(outs[name], axis=1)
    return (xp, xs, stack("k"), stack("v"), stack("C"), stack("n"), stack("m"), stack("h"), stack("S"))
```
